```python
import math, functools
import jax, jax.numpy as jnp
from jax import lax
import numpy as np

D_MODEL = 1024
BATCH = 2
SEQ = 8192
DEPTH = 2
DEC_BATCH = 128
DEC_SEQ = 8
PAST_LEN = 2048
PAGE_SIZE = 128

SSM_WIDTH = D_MODEL // 2
SSM_GROUP = 16
SSM_GROUPS = SSM_WIDTH // SSM_GROUP
SSM_STATE = 64
HEAD_DIM = 64
ATT_WIDTH = D_MODEL - SSM_WIDTH
N_HEADS = ATT_WIDTH // HEAD_DIM
IDX_HEADS = 8
IDX_DIM = 32
TOPK_MAX = 256
Q_BLOCK = 128
D_FF = 2816
D_IN = SSM_WIDTH + 3 * ATT_WIDTH + IDX_HEADS * IDX_DIM + IDX_DIM + IDX_HEADS
LN_EPS = 1e-5
DT_MIN = 1e-3
DT_MAX = 1e-1

kernel_name = "hymba_s5_dsa_macaron_deepnorm_step"


def layer_norm(x, g, b):
    xf = x.astype(jnp.float32)
    mu = jnp.mean(xf, axis=-1, keepdims=True)
    var = jnp.mean(jnp.square(xf - mu), axis=-1, keepdims=True)
    return ((xf - mu) * lax.rsqrt(var + LN_EPS) * g.astype(jnp.float32) + b.astype(jnp.float32)).astype(x.dtype)


def swiglu(x, w_up, w_down):
    gate, up = jnp.split(x @ w_up, 2, axis=-1)
    return (jax.nn.silu(gate) * up) @ w_down


def s5_mixer(u, h0_re, h0_im, p):
    f32 = jnp.float32
    uf = u.astype(f32)
    lam_re = p['ssm_lambda_re'].astype(f32)
    lam_im = p['ssm_lambda_im'].astype(f32)
    step = jnp.exp(p['ssm_log_step'].astype(f32))[:, None]
    mag = jnp.exp(lam_re * step)
    ang = lam_im * step
    ab_re, ab_im = mag * jnp.cos(ang), mag * jnp.sin(ang)
    nr, ni = ab_re - 1.0, ab_im
    den = lam_re * lam_re + lam_im * lam_im
    f_re = (nr * lam_re + ni * lam_im) / den
    f_im = (ni * lam_re - nr * lam_im) / den
    b_re = p['ssm_b_re'].astype(f32)
    b_im = p['ssm_b_im'].astype(f32)
    bb_re = f_re[..., None] * b_re - f_im[..., None] * b_im
    bb_im = f_re[..., None] * b_im + f_im[..., None] * b_re
    bu_re = jnp.einsum('blgh,gph->blgp', uf, bb_re)
    bu_im = jnp.einsum('blgh,gph->blgp', uf, bb_im)
    h0r = h0_re.astype(f32)
    h0i = h0_im.astype(f32)
    bu_re = bu_re.at[:, 0].add(ab_re * h0r - ab_im * h0i)
    bu_im = bu_im.at[:, 0].add(ab_re * h0i + ab_im * h0r)
    a_re = jnp.broadcast_to(ab_re, bu_re.shape)
    a_im = jnp.broadcast_to(ab_im, bu_im.shape)

    def combine(e1, e2):
        a1r, a1i, b1r, b1i = e1
        a2r, a2i, b2r, b2i = e2
        return (a1r * a2r - a1i * a2i,
                a1r * a2i + a1i * a2r,
                a2r * b1r - a2i * b1i + b2r,
                a2r * b1i + a2i * b1r + b2i)

    _, _, h_re, h_im = lax.associative_scan(combine, (a_re, a_im, bu_re, bu_im), axis=1)
    y = (jnp.einsum('blgp,ghp->blgh', h_re, p['ssm_c_re'].astype(f32))
         - jnp.einsum('blgp,ghp->blgh', h_im, p['ssm_c_im'].astype(f32))
         + p['ssm_d'].astype(f32) * uf)
    return y.astype(u.dtype), h_re[:, -1], h_im[:, -1]


def gather_rows(rows, ids):
    return jax.vmap(lambda r, i: r[i])(rows, ids)


def index_topk(q_idx, w_idx, k_idx, qpos, topk):
    s = jax.nn.relu(jnp.einsum('bqhd,bsd->bqhs', q_idx, k_idx).astype(jnp.float32))
    score = jnp.einsum('bqh,bqhs->bqs', w_idx.astype(jnp.float32), s)
    kpos = jnp.arange(k_idx.shape[1])
    causal = kpos[None, :] <= qpos[:, None]
    score = jnp.where(causal[None], score, -jnp.inf)
    _, idx = lax.top_k(score, topk)
    valid = idx <= qpos[None, :, None]
    return idx, valid


def attend_selected(q, k_sel, v_sel, valid):
    logits = jnp.einsum('bqhd,bqkhd->bqhk', q, k_sel).astype(jnp.float32) * (HEAD_DIM ** -0.5)
    logits = jnp.where(valid[:, :, None, :], logits, -jnp.inf)
    probs = jax.nn.softmax(logits, axis=-1).astype(v_sel.dtype)
    return jnp.einsum('bqhk,bqkhd->bqhd', probs, v_sel)


def prompt_sparse_attention(q, k, v, q_idx, k_idx, w_idx):
    B, L = q.shape[0], q.shape[1]
    topk = min(TOPK_MAX, L // 4)
    nb = L // Q_BLOCK

    def to_blocks(t):
        return jnp.moveaxis(t.reshape(B, nb, Q_BLOCK, *t.shape[2:]), 1, 0)

    def one_block(args):
        blk, qb, qib, wb = args
        qpos = blk * Q_BLOCK + jnp.arange(Q_BLOCK)
        idx, valid = index_topk(qib, wb, k_idx, qpos, topk)
        return attend_selected(qb, gather_rows(k, idx), gather_rows(v, idx), valid)

    out = lax.map(one_block, (jnp.arange(nb), to_blocks(q), to_blocks(q_idx), to_blocks(w_idx)))
    return jnp.moveaxis(out, 0, 1).reshape(B, L, N_HEADS, HEAD_DIM)


def sample_sparse_attention(q, k_new, v_new, q_idx, k_idx_new, w_idx, cache_k_l, cache_v_l, cache_kidx_l, page_table):
    DB, S = q.shape[0], q.shape[1]
    n_pages = page_table.shape[1]
    past = n_pages * PAGE_SIZE
    topk = min(TOPK_MAX, (past + S) // 4)
    k_idx_past = cache_kidx_l[page_table].reshape(DB, past, IDX_DIM)
    k_idx_all = jnp.concatenate([k_idx_past, k_idx_new.astype(k_idx_past.dtype)], axis=1)
    qpos = past + jnp.arange(S)
    idx, valid = index_topk(q_idx, w_idx, k_idx_all, qpos, topk)
    in_past = idx < past
    pidx = jnp.minimum(idx, past - 1)
    phys = jnp.take_along_axis(page_table, (pidx // PAGE_SIZE).reshape(DB, -1), axis=1).reshape(idx.shape)
    flat = phys * PAGE_SIZE + pidx % PAGE_SIZE
    nidx = jnp.clip(idx - past, 0, S - 1)
    k_pool = cache_k_l.reshape(-1, N_HEADS, HEAD_DIM)
    v_pool = cache_v_l.reshape(-1, N_HEADS, HEAD_DIM)

    def per_query(t):
        return jnp.moveaxis(t[:, :, None], 1, 0)

    def one_query(args):
        qi, fi, ni, pi, vi = args
        mask = pi[..., None, None]
        ks = jnp.where(mask, k_pool[fi], gather_rows(k_new, ni).astype(k_pool.dtype))
        vs = jnp.where(mask, v_pool[fi], gather_rows(v_new, ni).astype(v_pool.dtype))
        return attend_selected(qi, ks.astype(qi.dtype), vs.astype(qi.dtype), vi)

    out = lax.map(one_query, (per_query(q), per_query(flat), per_query(nidx), per_query(in_past), per_query(valid)))
    return jnp.moveaxis(out, 0, 1).reshape(DB, S, N_HEADS, HEAD_DIM)


def decoder_layer(x, h0_re, h0_im, p, attention_fn):
    alpha = (2.0 * DEPTH) ** 0.25
    lead = x.shape[:2]
    x = layer_norm(alpha * x + 0.5 * swiglu(x, p['ffn1_up'], p['ffn1_down']), p['ln_g'][0], p['ln_b'][0])
    offs = np.cumsum([SSM_WIDTH, ATT_WIDTH, ATT_WIDTH, ATT_WIDTH, IDX_HEADS * IDX_DIM, IDX_DIM]).tolist()
    u, q, k, v, qi, ki, wi = jnp.split(x @ p['w_in'], offs, axis=-1)
    y_ssm, h_re, h_im = s5_mixer(u.reshape(*lead, SSM_GROUPS, SSM_GROUP), h0_re, h0_im, p)
    g = jax.nn.gelu(y_ssm.reshape(*lead, SSM_WIDTH))
    y_ssm = g * jax.nn.sigmoid(g @ p['w_glu'] + p['b_glu'])
    q = q.reshape(*lead, N_HEADS, HEAD_DIM)
    k = k.reshape(*lead, N_HEADS, HEAD_DIM)
    v = v.reshape(*lead, N_HEADS, HEAD_DIM)
    qi = qi.reshape(*lead, IDX_HEADS, IDX_DIM)
    y_att = attention_fn(q, k, v, qi, ki, wi).reshape(*lead, ATT_WIDTH)
    mix = jnp.concatenate([y_ssm, y_att.astype(y_ssm.dtype)], axis=-1) @ p['w_out']
    x = layer_norm(alpha * x + mix, p['ln_g'][1], p['ln_b'][1])
    x = layer_norm(alpha * x + 0.5 * swiglu(x, p['ffn2_up'], p['ffn2_down']), p['ln_g'][2], p['ln_b'][2])
    return x, k, v, ki, h_re, h_im


def setup_inputs(seed: int = 0) -> dict:
    key = jax.random.key(seed)
    ks = jax.random.split(key, 32)
    f32 = jnp.float32
    n_pages = PAST_LEN // PAGE_SIZE
    used = DEC_BATCH * n_pages
    n_pool = used + max(1, used // 4)
    beta = (8.0 * DEPTH) ** -0.25
    nrm = lambda k, s, sc: jax.random.normal(k, s, f32) * sc

    x_prompt = nrm(ks[0], (BATCH, SEQ, D_MODEL), 1.0)
    x_sample = nrm(ks[1], (DEC_BATCH, DEC_SEQ, D_MODEL), 1.0)
    cache_k = nrm(ks[2], (DEPTH, n_pool, PAGE_SIZE, N_HEADS, HEAD_DIM), 1.0)
    cache_v = nrm(ks[3], (DEPTH, n_pool, PAGE_SIZE, N_HEADS, HEAD_DIM), beta)
    cache_kidx = nrm(ks[4], (DEPTH, n_pool, PAGE_SIZE, IDX_DIM), 1.0)
    state_ssm_re = nrm(ks[5], (DEPTH, DEC_BATCH, SSM_GROUPS, SSM_STATE), 0.1)
    state_ssm_im = nrm(ks[6], (DEPTH, DEC_BATCH, SSM_GROUPS, SSM_STATE), 0.1)
    page_table = jax.random.permutation(ks[7], n_pool)[:used].reshape(DEC_BATCH, n_pages).astype(jnp.int32)

    col_scale = jnp.concatenate([jnp.ones((SSM_WIDTH + 2 * ATT_WIDTH,), f32),
                                 jnp.full((ATT_WIDTH,), beta, f32),
                                 jnp.ones((IDX_HEADS * IDX_DIM + IDX_DIM + IDX_HEADS,), f32)])
    w_in = nrm(ks[8], (DEPTH, D_MODEL, D_IN), D_MODEL ** -0.5) * col_scale
    n_idx = jnp.arange(SSM_STATE, dtype=f32)
    ssm_lambda_re = -0.5 + nrm(ks[9], (DEPTH, SSM_GROUPS, SSM_STATE), 0.01)
    ssm_lambda_im = math.pi * n_idx + nrm(ks[10], (DEPTH, SSM_GROUPS, SSM_STATE), 0.01)
    ssm_log_step = jax.random.uniform(ks[11], (DEPTH, SSM_GROUPS), f32, math.log(DT_MIN), math.log(DT_MAX))
    ssm_b_re = nrm(ks[12], (DEPTH, SSM_GROUPS, SSM_STATE, SSM_GROUP), (2.0 * SSM_GROUP) ** -0.5)
    ssm_b_im = nrm(ks[13], (DEPTH, SSM_GROUPS, SSM_STATE, SSM_GROUP), (2.0 * SSM_GROUP) ** -0.5)
    ssm_c_re = nrm(ks[14], (DEPTH, SSM_GROUPS, SSM_GROUP, SSM_STATE), SSM_STATE ** -0.5)
    ssm_c_im = nrm(ks[15], (DEPTH, SSM_GROUPS, SSM_GROUP, SSM_STATE), SSM_STATE ** -0.5)
    ssm_d = nrm(ks[16], (DEPTH, SSM_GROUPS, SSM_GROUP), 1.0)
    w_glu = nrm(ks[17], (DEPTH, SSM_WIDTH, SSM_WIDTH), SSM_WIDTH ** -0.5)
    b_glu = nrm(ks[18], (DEPTH, SSM_WIDTH), 0.02)
    w_out = nrm(ks[19], (DEPTH, D_MODEL, D_MODEL), beta * D_MODEL ** -0.5)
    ffn1_up = nrm(ks[20], (DEPTH, D_MODEL, 2 * D_FF), D_MODEL ** -0.5)
    ffn1_down = nrm(ks[21], (DEPTH, D_FF, D_MODEL), beta * D_FF ** -0.5)
    ffn2_up = nrm(ks[22], (DEPTH, D_MODEL, 2 * D_FF), D_MODEL ** -0.5)
    ffn2_down = nrm(ks[23], (DEPTH, D_FF, D_MODEL), beta * D_FF ** -0.5)
    ln_g = 1.0 + nrm(ks[24], (DEPTH, 3, D_MODEL), 0.02)
    ln_b = nrm(ks[25], (DEPTH, 3, D_MODEL), 0.02)
    return {"x_prompt": x_prompt, "x_sample": x_sample, "cache_k": cache_k, "cache_v": cache_v,
            "cache_kidx": cache_kidx, "state_ssm_re": state_ssm_re, "state_ssm_im": state_ssm_im,
            "page_table": page_table, "w_in": w_in, "ssm_lambda_re": ssm_lambda_re,
            "ssm_lambda_im": ssm_lambda_im, "ssm_log_step": ssm_log_step, "ssm_b_re": ssm_b_re,
            "ssm_b_im": ssm_b_im, "ssm_c_re": ssm_c_re, "ssm_c_im": ssm_c_im, "ssm_d": ssm_d,
            "w_glu": w_glu, "b_glu": b_glu, "w_out": w_out, "ffn1_up": ffn1_up, "ffn1_down": ffn1_down,
            "ffn2_up": ffn2_up, "ffn2_down": ffn2_down, "ln_g": ln_g, "ln_b": ln_b}


def reference(x_prompt, x_sample, cache_k, cache_v, cache_kidx, state_ssm_re, state_ssm_im, page_table,
              w_in, ssm_lambda_re, ssm_lambda_im, ssm_log_step, ssm_b_re, ssm_b_im, ssm_c_re, ssm_c_im,
              ssm_d, w_glu, b_glu, w_out, ffn1_up, ffn1_down, ffn2_up, ffn2_down, ln_g, ln_b):
    yp, ys = x_prompt, x_sample
    kp, vp, kip, hrp, hip = [], [], [], [], []
    kss, vss, kis, hrs, his = [], [], [], [], []
    zeros_state = jnp.zeros((x_prompt.shape[0], SSM_GROUPS, SSM_STATE), jnp.float32)
    for l in range(DEPTH):
        p = {'w_in': w_in[l], 'ssm_lambda_re': ssm_lambda_re[l], 'ssm_lambda_im': ssm_lambda_im[l],
             'ssm_log_step': ssm_log_step[l], 'ssm_b_re': ssm_b_re[l], 'ssm_b_im': ssm_b_im[l],
             'ssm_c_re': ssm_c_re[l], 'ssm_c_im': ssm_c_im[l], 'ssm_d': ssm_d[l], 'w_glu': w_glu[l],
             'b_glu': b_glu[l], 'w_out': w_out[l], 'ffn1_up': ffn1_up[l], 'ffn1_down': ffn1_down[l],
             'ffn2_up': ffn2_up[l], 'ffn2_down': ffn2_down[l], 'ln_g': ln_g[l], 'ln_b': ln_b[l]}
        yp, k1, v1, ki1, hr1, hi1 = decoder_layer(yp, zeros_state, zeros_state, p, prompt_sparse_attention)
        kp.append(k1); vp.append(v1); kip.append(ki1); hrp.append(hr1); hip.append(hi1)
        attn_s = functools.partial(sample_sparse_attention, cache_k_l=cache_k[l], cache_v_l=cache_v[l],
                                   cache_kidx_l=cache_kidx[l], page_table=page_table)
        ys, k2, v2, ki2, hr2, hi2 = decoder_layer(ys, state_ssm_re[l], state_ssm_im[l], p, attn_s)
        kss.append(k2); vss.append(v2); kis.append(ki2); hrs.append(hr2); his.append(hi2)
    return (yp, ys,
            jnp.stack(kp), jnp.stack(vp), jnp.stack(kip), jnp.stack(hrp), jnp.stack(hip),
            jnp.stack(kss), jnp.stack(vss), jnp.stack(kis), jnp.stack(hrs), jnp.stack(his))
```

```python
import functools
import math

import jax
import jax.numpy as jnp
from jax import lax
from jax.experimental import pallas as pl
from jax.experimental.pallas import tpu as pltpu

F32 = jnp.float32
BF16 = jnp.bfloat16

SSM_GROUP = 16
SSM_STATE = 64
HEAD_DIM = 64
IDX_HEADS = 8
IDX_DIM = 32
TOPK_MAX = 256
LN_EPS = 1e-5

LANE = 128
SUBLANE = 8
VMEM_LIMIT_BYTES = 56 * 1024 * 1024

NEG_INF = float("-inf")
KEY_LOWEST = -2139095040
KEY_POS_INF = 0x7F800000


def _cparams(*sem):
    return pltpu.CompilerParams(dimension_semantics=sem, vmem_limit_bytes=VMEM_LIMIT_BYTES)


def _layer_norm(y, g, b):
    mu = jnp.mean(y, axis=-1, keepdims=True)
    yc = y - mu
    var = jnp.mean(yc * yc, axis=-1, keepdims=True)
    return yc * lax.rsqrt(var + LN_EPS) * g + b


def _split_bf16(x):
    hi = x.astype(BF16)
    lo = (x - hi.astype(F32)).astype(BF16)
    return hi, lo


def _dot(a, b):
    return jnp.dot(a, b, preferred_element_type=F32)


def _dot_nt(a, b):
    return lax.dot_general(a, b, (((1,), (1,)), ((), ())), preferred_element_type=F32)


def _dot3(a_hi, a_lo, b_hi, b_lo):
    return _dot(a_hi, b_hi) + _dot(a_lo, b_hi) + _dot(a_hi, b_lo)


def _ffn_kernel(x_ref, wg_ref, wu_ref, wd_ref, g_ref, b_ref, o_ref, xb_ref, acc_ref, *, alpha):
    j = pl.program_id(1)

    @pl.when(j == 0)
    def _():
        xb_ref[...] = x_ref[...].astype(BF16)
        acc_ref[...] = jnp.zeros_like(acc_ref)

    xb = xb_ref[...]
    gate = _dot(xb, wg_ref[...])
    up = _dot(xb, wu_ref[...])
    act = (gate * jax.nn.sigmoid(gate)) * up
    acc_ref[...] += _dot(act.astype(BF16), wd_ref[...])

    @pl.when(j == pl.num_programs(1) - 1)
    def _():
        y = alpha * x_ref[...] + 0.5 * acc_ref[...]
        o_ref[...] = _layer_norm(y, g_ref[...], b_ref[...])


def _ffn(x, w_up, w_down, g, b, alpha, tm, tf):
    m, d = x.shape
    d_ff = w_down.shape[0]
    nf = d_ff // tf
    assert m % tm == 0 and d_ff % tf == 0
    return pl.pallas_call(
        functools.partial(_ffn_kernel, alpha=alpha),
        grid=(m // tm, nf),
        in_specs=[
            pl.BlockSpec((tm, d), lambda i, j: (i, 0)),
            pl.BlockSpec((d, tf), lambda i, j: (0, j)),
            pl.BlockSpec((d, tf), lambda i, j: (0, j + nf)),
            pl.BlockSpec((tf, d), lambda i, j: (j, 0)),
            pl.BlockSpec((1, d), lambda i, j: (0, 0)),
            pl.BlockSpec((1, d), lambda i, j: (0, 0)),
        ],
        out_specs=pl.BlockSpec((tm, d), lambda i, j: (i, 0)),
        out_shape=jax.ShapeDtypeStruct((m, d), F32),
        scratch_shapes=[pltpu.VMEM((tm, d), BF16), pltpu.VMEM((tm, d), F32)],
        compiler_params=_cparams("parallel", "arbitrary"),
        name="swiglu_ln",
    )(x, w_up, w_up, w_down, g, b)


def _inproj_kernel(x_ref, w_ref, *o_refs):
    y = _dot(x_ref[...].astype(BF16), w_ref[...])
    off = 0
    for o_ref in o_refs:
        n = o_ref.shape[1]
        o_ref[...] = y[:, off:off + n]
        off += n


def _inproj(x, w, widths, tm):
    m, d = x.shape
    n = w.shape[1]
    assert sum(widths) == n and m % tm == 0
    return pl.pallas_call(
        _inproj_kernel,
        grid=(m // tm,),
        in_specs=[pl.BlockSpec((tm, d), lambda i: (i, 0)),
                  pl.BlockSpec((d, n), lambda i: (0, 0))],
        out_specs=[pl.BlockSpec((tm, wd), lambda i: (i, 0)) for wd in widths],
        out_shape=[jax.ShapeDtypeStruct((m, wd), F32) for wd in widths],
        compiler_params=_cparams("parallel"),
        name="in_proj",
    )(x, w)


def _s5_param_kernel(lre_ref, lim_ref, ls_ref, bre_ref, bim_ref, abre_ref, abim_ref, bbre_ref, bbim_ref):
    lam_re = lre_ref[...]
    lam_im = lim_ref[...]
    step = jnp.exp(ls_ref[...])
    mag = jnp.exp(lam_re * step)
    ang = lam_im * step
    ab_re = mag * jnp.cos(ang)
    ab_im = mag * jnp.sin(ang)
    nr = ab_re - 1.0
    ni = ab_im
    den = lam_re * lam_re + lam_im * lam_im
    f_re = (nr * lam_re + ni * lam_im) / den
    f_im = (ni * lam_re - nr * lam_im) / den
    abre_ref[...] = ab_re
    abim_ref[...] = ab_im
    b_re = bre_ref[...]
    b_im = bim_ref[...]
    bbre_ref[...] = f_re[:, None, :] * b_re - f_im[:, None, :] * b_im
    bbim_ref[...] = f_re[:, None, :] * b_im + f_im[:, None, :] * b_re


def _s5_params(lam_re, lam_im, log_step, b_re, b_im):
    g, p = lam_re.shape
    h = b_re.shape[2]
    b_re_t = jnp.swapaxes(b_re, 1, 2)
    b_im_t = jnp.swapaxes(b_im, 1, 2)
    return pl.pallas_call(
        _s5_param_kernel,
        out_shape=[jax.ShapeDtypeStruct((g, p), F32), jax.ShapeDtypeStruct((g, p), F32),
                   jax.ShapeDtypeStruct((g, h, p), F32), jax.ShapeDtypeStruct((g, h, p), F32)],
        name="s5_params",
    )(lam_re, lam_im, log_step.reshape(g, 1), b_re_t, b_im_t)


def _block_diag_in(bb_t):
    g, h, p = bb_t.shape
    per = LANE // h
    x = bb_t.reshape(g // per, per, h, p)
    eye = jnp.eye(per, dtype=F32)
    return jnp.einsum("jahp,ab->jahbp", x, eye).reshape(g // per, per * h, per * p)


def _block_diag_out(c):
    g, h, p = c.shape
    per = LANE // h
    x = c.reshape(g // per, per, h, p)
    eye = jnp.eye(per, dtype=F32)
    return jnp.einsum("jahp,ab->japbh", x, eye).reshape(g // per, per * p, per * h)


def _hi_lo_stack(w_re, w_im):
    w = jnp.stack([w_re, w_im])
    hi = w.astype(BF16)
    lo = (w - hi.astype(F32)).astype(BF16)
    return jnp.stack([hi, lo])


def _s5_kernel(u_ref, h0re_ref, h0im_ref, abre_ref, abim_ref, wb_ref, wc_ref, d_ref,
               y_ref, hre_out, him_out, hre_s, him_s, stre_s, stim_s, *, n_steps, rows_per_step, row_group):
    c = pl.program_id(1)
    nb = wb_ref.shape[2]
    sw = wb_ref.shape[4]

    @pl.when(c == 0)
    def _():
        stre_s[...] = h0re_ref[0]
        stim_s[...] = h0im_ref[0]

    u = u_ref[0]
    u_hi, u_lo = _split_bf16(u)
    for j in range(nb):
        uh = u_hi[:, j * LANE:(j + 1) * LANE]
        ul = u_lo[:, j * LANE:(j + 1) * LANE]
        hre_s[:, j * sw:(j + 1) * sw] = _dot3(uh, ul, wb_ref[0, 0, j], wb_ref[1, 0, j])
        him_s[:, j * sw:(j + 1) * sw] = _dot3(uh, ul, wb_ref[0, 1, j], wb_ref[1, 1, j])

    a_re = abre_ref[...]
    a_im = abim_ref[...]

    def group_body(rg, carry):
        r0 = pl.multiple_of(rg * row_group, row_group)

        def step_body(t, h):
            h_re, h_im = h
            row = t * rows_per_step + r0
            b_re = hre_s[pl.ds(row, row_group), :]
            b_im = him_s[pl.ds(row, row_group), :]
            n_re = a_re * h_re - a_im * h_im + b_re
            n_im = a_re * h_im + a_im * h_re + b_im
            hre_s[pl.ds(row, row_group), :] = n_re
            him_s[pl.ds(row, row_group), :] = n_im
            return n_re, n_im

        h0 = (stre_s[pl.ds(r0, row_group), :], stim_s[pl.ds(r0, row_group), :])
        h_re, h_im = lax.fori_loop(0, n_steps, step_body, h0)
        stre_s[pl.ds(r0, row_group), :] = h_re
        stim_s[pl.ds(r0, row_group), :] = h_im
        return carry

    lax.fori_loop(0, rows_per_step // row_group, group_body, 0)

    for j in range(nb):
        hr_hi, hr_lo = _split_bf16(hre_s[:, j * sw:(j + 1) * sw])
        hi_hi, hi_lo = _split_bf16(him_s[:, j * sw:(j + 1) * sw])
        yj = (_dot3(hr_hi, hr_lo, wc_ref[0, 0, j], wc_ref[1, 0, j])
              - _dot3(hi_hi, hi_lo, wc_ref[0, 1, j], wc_ref[1, 1, j]))
        sl = slice(j * LANE, (j + 1) * LANE)
        y_ref[0, :, sl] = yj + d_ref[:, sl] * u[:, sl]

    @pl.when(c == pl.num_programs(1) - 1)
    def _():
        hre_out[0] = stre_s[...]
        him_out[0] = stim_s[...]


def _s5_scan(u, h0_re, h0_im, ab_re, ab_im, wb, wc, d, n_steps, rows_per_step):
    nbatch, rows, w = u.shape
    s = ab_re.shape[1]
    r = rows_per_step
    chunk_rows = n_steps * r
    assert rows % chunk_rows == 0
    row_group = min(r, SUBLANE)
    assert r % row_group == 0
    kern = functools.partial(_s5_kernel, n_steps=n_steps, rows_per_step=r, row_group=row_group)
    const = lambda shape: pl.BlockSpec(shape, lambda n, c: (0,) * len(shape))
    return pl.pallas_call(
        kern,
        grid=(nbatch, rows // chunk_rows),
        in_specs=[
            pl.BlockSpec((1, chunk_rows, w), lambda n, c: (n, c, 0)),
            pl.BlockSpec((1, r, s), lambda n, c: (n, 0, 0)),
            pl.BlockSpec((1, r, s), lambda n, c: (n, 0, 0)),
            const((1, s)), const((1, s)), const(wb.shape), const(wc.shape), const((1, w)),
        ],
        out_specs=[
            pl.BlockSpec((1, chunk_rows, w), lambda n, c: (n, c, 0)),
            pl.BlockSpec((1, r, s), lambda n, c: (n, 0, 0)),
            pl.BlockSpec((1, r, s), lambda n, c: (n, 0, 0)),
        ],
        out_shape=[jax.ShapeDtypeStruct((nbatch, rows, w), F32),
                   jax.ShapeDtypeStruct((nbatch, r, s), F32),
                   jax.ShapeDtypeStruct((nbatch, r, s), F32)],
        scratch_shapes=[pltpu.VMEM((chunk_rows, s), F32), pltpu.VMEM((chunk_rows, s), F32),
                        pltpu.VMEM((r, s), F32), pltpu.VMEM((r, s), F32)],
        compiler_params=_cparams("parallel", "arbitrary"),
        name="s5_scan",
    )(u, h0_re, h0_im, ab_re, ab_im, wb, wc, d)


def _key_to_f32(k):
    bits = jnp.where(k >= 0, k, k ^ jnp.int32(0x7FFFFFFF))
    return lax.bitcast_convert_type(bits, F32)


def _kth_threshold(count_ge, shape, topk):
    kf = float(topk)
    c_zero = count_ge(jnp.zeros(shape, F32))
    c_low = count_ge(jnp.full(shape, _LOWEST_F32, F32))
    pos = c_zero >= kf
    lo = jnp.where(pos, 0, KEY_LOWEST).astype(jnp.int32)
    hi = jnp.where(pos, KEY_POS_INF + 1, 0).astype(jnp.int32)
    cnt_lo = jnp.where(pos, c_zero, c_low)

    def body(_, carry):
        lo, hi, cnt_lo = carry
        mid = lo + ((hi - lo) >> 1)
        cnt = count_ge(_key_to_f32(mid))
        ok = cnt >= kf
        return jnp.where(ok, mid, lo), jnp.where(ok, hi, mid), jnp.where(ok, cnt, cnt_lo)

    lo, hi, cnt_lo = lax.fori_loop(0, 31, body, (lo, hi, cnt_lo))
    return _key_to_f32(lo), cnt_lo


_LOWEST_F32 = -3.4028234663852886e38


def _tie_index_bound(count_eq_le, count_gt, cnt_ge, shape, topk, n_keys):
    need = float(topk) - count_gt
    tied = cnt_ge > float(topk)
    lo = jnp.full(shape, -1, jnp.int32)
    hi = jnp.full(shape, n_keys - 1, jnp.int32)

    def body(_, carry):
        lo, hi = carry
        mid = lo + ((hi - lo) >> 1)
        ok = count_eq_le(mid) >= need
        return jnp.where(ok, lo, mid), jnp.where(ok, mid, hi)

    n_iter = max(1, math.ceil(math.log2(n_keys + 1)))
    lo, hi = lax.fori_loop(0, n_iter, body, (lo, hi))
    return jnp.where(tied, hi, n_keys)


def _prompt_attn_kernel(q_ref, qi_ref, w_ref, kit_ref, kt_ref, v_ref, o_ref,
                        s_ref, m_ref, l_ref, acc_ref, *, topk, q_block, chunk, n_heads):
    qb = pl.program_id(1)
    n_keys = kt_ref.shape[1] * chunk
    n_chunks = (qb * q_block + q_block + chunk - 1) // chunk
    q_pos = qb * q_block + lax.broadcasted_iota(jnp.int32, (q_block, chunk), 0)
    k_off = lax.broadcasted_iota(jnp.int32, (q_block, chunk), 1)

    qi = qi_ref[0, 0].astype(BF16)
    w_col = w_ref[0, 0]

    def score_body(c, carry):
        st = _dot(qi, kit_ref[0, c])
        st = jnp.maximum(st, 0.0) * w_col
        sc = jnp.sum(st.reshape(IDX_HEADS, q_block, chunk), axis=0)
        s_ref[c] = jnp.where(c * chunk + k_off <= q_pos, sc, NEG_INF)
        return carry

    lax.fori_loop(0, n_chunks, score_body, 0)

    col = (q_block, 1)

    def count(pred):
        def body(c, cnt):
            return cnt + jnp.sum(pred(c, s_ref[c]).astype(F32), axis=1, keepdims=True)
        return lax.fori_loop(0, n_chunks, body, jnp.zeros(col, F32))

    thr, cnt_ge = _kth_threshold(lambda t: count(lambda c, s: s >= t), col, topk)

    @pl.when(jnp.max(cnt_ge) > float(topk))
    def _():
        cnt_gt = count(lambda c, s: s > thr)
        bound = _tie_index_bound(
            lambda j: count(lambda c, s: (s == thr) & (c * chunk + k_off <= j)),
            cnt_gt, cnt_ge, col, topk, n_keys)

        def drop_body(c, carry):
            s = s_ref[c]
            s_ref[c] = jnp.where((s == thr) & (c * chunk + k_off > bound), NEG_INF, s)
            return carry

        lax.fori_loop(0, n_chunks, drop_body, 0)

    m_ref[...] = jnp.full(m_ref.shape, NEG_INF, F32)
    l_ref[...] = jnp.zeros(l_ref.shape, F32)
    acc_ref[...] = jnp.zeros(acc_ref.shape, F32)
    q = (q_ref[0] * (HEAD_DIM ** -0.5)).astype(BF16)

    def attn_body(c, carry):
        bias = jnp.where(s_ref[c] >= thr, 0.0, NEG_INF)
        for h in range(n_heads):
            qh = q[:, h * HEAD_DIM:(h + 1) * HEAD_DIM]
            logits = _dot(qh, kt_ref[0, c, h * HEAD_DIM:(h + 1) * HEAD_DIM, :]) + bias
            m_old = m_ref[h]
            m_new = jnp.maximum(m_old, jnp.max(logits, axis=1, keepdims=True))
            m_safe = jnp.where(m_new == NEG_INF, 0.0, m_new)
            p = jnp.exp(logits - m_safe)
            scale = jnp.exp(m_old - m_safe)
            l_ref[h] = scale * l_ref[h] + jnp.sum(p, axis=1, keepdims=True)
            pair = h // 2
            pv = _dot(p.astype(BF16), v_ref[0, c, :, pair * LANE:(pair + 1) * LANE])
            acc_ref[h] = scale * acc_ref[h] + pv
            m_ref[h] = m_new
        return carry

    lax.fori_loop(0, n_chunks, attn_body, 0)

    lane = lax.broadcasted_iota(jnp.int32, (q_block, LANE), 1)
    for pair in range(n_heads // 2):
        even = acc_ref[2 * pair] / l_ref[2 * pair]
        odd = acc_ref[2 * pair + 1] / l_ref[2 * pair + 1]
        o_ref[0, :, pair * LANE:(pair + 1) * LANE] = jnp.where(lane < HEAD_DIM, even, odd)


def _prompt_attention(q, qi_hm, w_col, ki_t, k_t, v, topk, q_block, chunk):
    b, l, aw = q.shape
    n_heads = aw // HEAD_DIM
    assert n_heads % 2 == 0 and l % chunk == 0 and l % q_block == 0 and chunk % q_block == 0
    nqb = l // q_block
    nch = l // chunk
    kern = functools.partial(_prompt_attn_kernel, topk=topk, q_block=q_block, chunk=chunk, n_heads=n_heads)
    rows = IDX_HEADS * q_block
    return pl.pallas_call(
        kern,
        grid=(b, nqb),
        in_specs=[
            pl.BlockSpec((1, q_block, aw), lambda i, j: (i, j, 0)),
            pl.BlockSpec((1, 1, rows, IDX_DIM), lambda i, j: (i, j, 0, 0)),
            pl.BlockSpec((1, 1, rows, 1), lambda i, j: (i, j, 0, 0)),
            pl.BlockSpec((1, nch, IDX_DIM, chunk), lambda i, j: (i, 0, 0, 0)),
            pl.BlockSpec((1, nch, aw, chunk), lambda i, j: (i, 0, 0, 0)),
            pl.BlockSpec((1, nch, chunk, aw), lambda i, j: (i, 0, 0, 0)),
        ],
        out_specs=pl.BlockSpec((1, q_block, aw), lambda i, j: (i, j, 0)),
        out_shape=jax.ShapeDtypeStruct((b, l, aw), F32),
        scratch_shapes=[pltpu.VMEM((nch, q_block, chunk), F32),
                        pltpu.VMEM((n_heads, q_block, 1), F32),
                        pltpu.VMEM((n_heads, q_block, 1), F32),
                        pltpu.VMEM((n_heads, q_block, LANE), F32)],
        compiler_params=_cparams("parallel", "arbitrary"),
        name="prompt_attention",
    )(q, qi_hm, w_col, ki_t, k_t, v)


def _sample_select_kernel(pt_ref, qi_ref, w_ref, kpage_ref, knew_ref, bias_ref, s_ref, kpad_ref, *, topk, page):
    p = pl.program_id(1)
    n_pages = pl.num_programs(1)
    n_q = knew_ref.shape[1]
    qi = qi_ref[0].astype(BF16)
    w_col = w_ref[0]

    def scores(keys):
        st = _dot_nt(qi, keys.astype(BF16))
        st = jnp.maximum(st, 0.0) * w_col
        return jnp.sum(st.reshape(IDX_HEADS, n_q, page), axis=0)

    s_ref[p] = scores(kpage_ref[...])

    @pl.when(p == n_pages - 1)
    def _():
        kpad_ref[...] = jnp.zeros(kpad_ref.shape, F32)
        kpad_ref[0:n_q, :] = knew_ref[0]
        q_row = lax.broadcasted_iota(jnp.int32, (n_q, page), 0)
        k_col = lax.broadcasted_iota(jnp.int32, (n_q, page), 1)
        s_ref[n_pages] = jnp.where(k_col <= q_row, scores(kpad_ref[...]), NEG_INF)

        col = (n_q, 1)
        key_idx = (lax.broadcasted_iota(jnp.int32, s_ref.shape, 0) * page
                   + lax.broadcasted_iota(jnp.int32, s_ref.shape, 2))

        def count(pred):
            hits = jnp.sum(pred(s_ref[...]).astype(F32), axis=0)
            return jnp.sum(hits, axis=1, keepdims=True)

        thr, cnt_ge = _kth_threshold(lambda t: count(lambda s: s >= t[None]), col, topk)

        @pl.when(jnp.max(cnt_ge) > float(topk))
        def _():
            cnt_gt = count(lambda s: s > thr[None])
            bound = _tie_index_bound(
                lambda j: count(lambda s: (s == thr[None]) & (key_idx <= j[None])),
                cnt_gt, cnt_ge, col, topk, (n_pages + 1) * page)
            s = s_ref[...]
            s_ref[...] = jnp.where((s == thr[None]) & (key_idx > bound[None]), NEG_INF, s)

        bias_ref[0] = jnp.where(s_ref[...] >= thr[None], 0.0, NEG_INF)


def _sample_select(page_table, qi_hm, w_col, cache_kidx, ki_new, layer, topk):
    db, n_pages = page_table.shape
    page = cache_kidx.shape[2]
    n_q = ki_new.shape[1]
    rows = IDX_HEADS * n_q
    kern = functools.partial(_sample_select_kernel, topk=topk, page=page)
    grid_spec = pltpu.PrefetchScalarGridSpec(
        num_scalar_prefetch=1,
        grid=(db, n_pages),
        in_specs=[
            pl.BlockSpec((1, rows, IDX_DIM), lambda b, p, pt: (b, 0, 0)),
            pl.BlockSpec((1, rows, 1), lambda b, p, pt: (b, 0, 0)),
            pl.BlockSpec((None, None, page, IDX_DIM), lambda b, p, pt: (layer, pt[b, p], 0, 0)),
            pl.BlockSpec((1, n_q, IDX_DIM), lambda b, p, pt: (b, 0, 0)),
        ],
        out_specs=pl.BlockSpec((1, n_pages + 1, n_q, page), lambda b, p, pt: (b, 0, 0, 0)),
        scratch_shapes=[pltpu.VMEM((n_pages + 1, n_q, page), F32), pltpu.VMEM((page, IDX_DIM), F32)],
    )
    return pl.pallas_call(
        kern,
        grid_spec=grid_spec,
        out_shape=jax.ShapeDtypeStruct((db, n_pages + 1, n_q, page), F32),
        compiler_params=_cparams("parallel", "arbitrary"),
        name="sample_select",
    )(page_table, qi_hm, w_col, cache_kidx, ki_new)


def _sample_attn_kernel(pt_ref, q_ref, bias_ref, kpage_ref, vpage_ref, knew_ref, vnew_ref, o_ref,
                        qz_ref, m_ref, l_ref, acc_ref, kpad_ref, vpad_ref, *, n_heads):
    p = pl.program_id(1)
    n_pages = pl.num_programs(1)
    n_q = q_ref.shape[1]
    aw = q_ref.shape[2]
    rows = n_heads * n_q
    head_of_row = lax.broadcasted_iota(jnp.int32, (rows, aw), 0) // n_q
    head_of_lane = lax.broadcasted_iota(jnp.int32, (rows, aw), 1) // HEAD_DIM

    @pl.when(p == 0)
    def _():
        q = q_ref[0] * (HEAD_DIM ** -0.5)
        q_rep = jnp.concatenate([q] * n_heads, axis=0)
        qz_ref[...] = jnp.where(head_of_row == head_of_lane, q_rep, 0.0).astype(BF16)
        m_ref[...] = jnp.full(m_ref.shape, NEG_INF, F32)
        l_ref[...] = jnp.zeros(l_ref.shape, F32)
        acc_ref[...] = jnp.zeros(acc_ref.shape, F32)

    def attend(keys, values, bias):
        logits = _dot_nt(qz_ref[...], keys.astype(BF16)) + jnp.concatenate([bias] * n_heads, axis=0)
        m_old = m_ref[...]
        m_new = jnp.maximum(m_old, jnp.max(logits, axis=1, keepdims=True))
        m_safe = jnp.where(m_new == NEG_INF, 0.0, m_new)
        pr = jnp.exp(logits - m_safe)
        scale = jnp.exp(m_old - m_safe)
        l_ref[...] = scale * l_ref[...] + jnp.sum(pr, axis=1, keepdims=True)
        acc_ref[...] = scale * acc_ref[...] + _dot(pr.astype(BF16), values.astype(BF16))
        m_ref[...] = m_new

    attend(kpage_ref[...], vpage_ref[...], bias_ref[0, p])

    @pl.when(p == n_pages - 1)
    def _():
        kpad_ref[...] = jnp.zeros(kpad_ref.shape, F32)
        vpad_ref[...] = jnp.zeros(vpad_ref.shape, F32)
        kpad_ref[0:n_q, :] = knew_ref[0]
        vpad_ref[0:n_q, :] = vnew_ref[0]
        attend(kpad_ref[...], vpad_ref[...], bias_ref[0, n_pages])
        out = jnp.where(head_of_row == head_of_lane, acc_ref[...] / l_ref[...], 0.0)
        o_ref[0] = jnp.sum(out.reshape(n_heads, n_q, aw), axis=0)


def _sample_attention(page_table, q, bias, cache_k, cache_v, k_new, v_new, layer):
    db, n_pages = page_table.shape
    page = cache_k.shape[2]
    _, n_q, aw = q.shape
    n_heads = aw // HEAD_DIM
    rows = n_heads * n_q
    kern = functools.partial(_sample_attn_kernel, n_heads=n_heads)
    per_b = lambda shape: pl.BlockSpec(shape, lambda b, p, pt: (b,) + (0,) * (len(shape) - 1))
    page_spec = pl.BlockSpec((None, None, page, aw), lambda b, p, pt: (layer, pt[b, p], 0, 0))
    grid_spec = pltpu.PrefetchScalarGridSpec(
        num_scalar_prefetch=1,
        grid=(db, n_pages),
        in_specs=[per_b((1, n_q, aw)), per_b((1, n_pages + 1, n_q, page)), page_spec, page_spec,
                  per_b((1, n_q, aw)), per_b((1, n_q, aw))],
        out_specs=per_b((1, n_q, aw)),
        scratch_shapes=[pltpu.VMEM((rows, aw), BF16), pltpu.VMEM((rows, 1), F32), pltpu.VMEM((rows, 1), F32),
                        pltpu.VMEM((rows, aw), F32), pltpu.VMEM((page, aw), F32), pltpu.VMEM((page, aw), F32)],
    )
    return pl.pallas_call(
        kern,
        grid_spec=grid_spec,
        out_shape=jax.ShapeDtypeStruct((db, n_q, aw), F32),
        compiler_params=_cparams("parallel", "arbitrary"),
        name="sample_attention",
    )(page_table, q, bias, cache_k, cache_v, k_new, v_new)


def _mix_kernel(x_ref, ys_ref, ya_ref, wglu_ref, bglu_ref, wos_ref, woa_ref, g_ref, b_ref, o_ref, *, alpha):
    y = ys_ref[...]
    gel = 0.5 * y * (1.0 + jnp.tanh(math.sqrt(2.0 / math.pi) * (y + 0.044715 * (y * y * y))))
    gate = jax.nn.sigmoid(_dot(gel.astype(BF16), wglu_ref[...]) + bglu_ref[...])
    y_ssm = gel * gate
    mix = _dot(y_ssm.astype(BF16), wos_ref[...]) + _dot(ya_ref[...].astype(BF16), woa_ref[...])
    o_ref[...] = _layer_norm(alpha * x_ref[...] + mix, g_ref[...], b_ref[...])


def _mix(x, y_ssm, y_att, w_glu, b_glu, w_out_ssm, w_out_att, g, b, alpha, tm):
    m, d = x.shape
    sw = y_ssm.shape[1]
    aw = y_att.shape[1]
    const = lambda shape: pl.BlockSpec(shape, lambda i: (0, 0))
    return pl.pallas_call(
        functools.partial(_mix_kernel, alpha=alpha),
        grid=(m // tm,),
        in_specs=[pl.BlockSpec((tm, d), lambda i: (i, 0)),
                  pl.BlockSpec((tm, sw), lambda i: (i, 0)),
                  pl.BlockSpec((tm, aw), lambda i: (i, 0)),
                  const((sw, sw)), const((1, sw)), const((sw, d)), const((aw, d)), const((1, d)), const((1, d))],
        out_specs=pl.BlockSpec((tm, d), lambda i: (i, 0)),
        out_shape=jax.ShapeDtypeStruct((m, d), F32),
        compiler_params=_cparams("parallel"),
        name="glu_outproj_ln",
    )(x, y_ssm, y_att, w_glu, b_glu, w_out_ssm, w_out_att, g, b)


def _row_tile(m, cap=512):
    t = cap
    while m % t:
        t //= 2
    return t


def _ff_tile(d_ff, cap=512):
    best = LANE
    for t in range(LANE, cap + 1, LANE):
        if d_ff % t == 0:
            best = t
    return best


def _head_major(x, n_blocks, block, n_heads, dim):
    b = x.shape[0]
    x = x.reshape(b, n_blocks, block, n_heads, dim)
    return jnp.swapaxes(x, 2, 3).reshape(b, n_blocks, n_heads * block, dim)


def kernel(x_prompt, x_sample, cache_k, cache_v, cache_kidx, state_ssm_re, state_ssm_im, page_table,
           w_in, ssm_lambda_re, ssm_lambda_im, ssm_log_step, ssm_b_re, ssm_b_im, ssm_c_re, ssm_c_im,
           ssm_d, w_glu, b_glu, w_out, ffn1_up, ffn1_down, ffn2_up, ffn2_down, ln_g, ln_b):
    bsz, seq, d_model = x_prompt.shape
    db, dseq, _ = x_sample.shape
    depth = w_in.shape[0]
    sw = d_model // 2
    aw = d_model - sw
    n_groups = sw // SSM_GROUP
    n_state = n_groups * SSM_STATE
    n_heads = aw // HEAD_DIM
    n_pool, page = cache_k.shape[1], cache_k.shape[2]
    n_pages = page_table.shape[1]
    past = n_pages * page
    alpha = (2.0 * depth) ** 0.25
    mp = bsz * seq
    ms = db * dseq
    m_all = mp + ms
    tm = _row_tile(m_all)
    q_block = min(128, seq)
    chunk = min(512, seq)
    topk_p = min(TOPK_MAX, seq // 4)
    topk_s = min(TOPK_MAX, (past + dseq) // 4)
    s5_rows = min(256, seq)
    s_groups = max(1, ms // 256)
    s_per = db // s_groups

    cache_k2 = cache_k.reshape(depth, n_pool, page, aw)
    cache_v2 = cache_v.reshape(depth, n_pool, page, aw)
    pad_w = LANE - IDX_DIM - IDX_HEADS
    widths = (sw, aw, aw, aw, IDX_HEADS * IDX_DIM, LANE)

    x = jnp.concatenate([x_prompt.reshape(mp, d_model), x_sample.reshape(ms, d_model)], axis=0)
    outs = {name: [] for name in ("kp", "vp", "kip", "hrp", "hip", "ks", "vs", "kis", "hrs", "his")}

    for l in range(depth):
        g_ln = ln_g[l][:, None, :]
        b_ln = ln_b[l][:, None, :]
        tf1 = _ff_tile(ffn1_down.shape[1])
        x = _ffn(x, ffn1_up[l].astype(BF16), ffn1_down[l].astype(BF16), g_ln[0], b_ln[0], alpha, tm, tf1)

        w_l = jnp.pad(w_in[l], ((0, 0), (0, pad_w))).astype(BF16)
        u, q, k, v, qi, kiwi = _inproj(x, w_l, widths, tm)
        ki = kiwi[:, :IDX_DIM]
        wi = kiwi[:, IDX_DIM:IDX_DIM + IDX_HEADS]

        ab_re, ab_im, bb_re_t, bb_im_t = _s5_params(ssm_lambda_re[l], ssm_lambda_im[l], ssm_log_step[l],
                                                     ssm_b_re[l], ssm_b_im[l])
        wb = _hi_lo_stack(_block_diag_in(bb_re_t), _block_diag_in(bb_im_t))
        wc = _hi_lo_stack(_block_diag_out(ssm_c_re[l]), _block_diag_out(ssm_c_im[l]))
        ab_re = ab_re.reshape(1, n_state)
        ab_im = ab_im.reshape(1, n_state)
        d_row = ssm_d[l].reshape(1, sw)

        zeros_state = jnp.zeros((bsz, 1, n_state), F32)
        y_ssm_p, hr_p, hi_p = _s5_scan(u[:mp].reshape(bsz, seq, sw), zeros_state, zeros_state,
                                       ab_re, ab_im, wb, wc, d_row, s5_rows, 1)
        u_s = u[mp:].reshape(s_groups, s_per, dseq, sw)
        u_s = jnp.swapaxes(u_s, 1, 2).reshape(s_groups, dseq * s_per, sw)
        y_ssm_s, hr_s, hi_s = _s5_scan(u_s, state_ssm_re[l].reshape(s_groups, s_per, n_state),
                                       state_ssm_im[l].reshape(s_groups, s_per, n_state),
                                       ab_re, ab_im, wb, wc, d_row, dseq, s_per)
        y_ssm_s = jnp.swapaxes(y_ssm_s.reshape(s_groups, dseq, s_per, sw), 1, 2).reshape(ms, sw)

        nqb = seq // q_block
        nch = seq // chunk
        q_p = q[:mp].reshape(bsz, seq, aw)
        qi_hm = _head_major(qi[:mp].reshape(bsz, seq, IDX_HEADS * IDX_DIM), nqb, q_block, IDX_HEADS, IDX_DIM)
        w_hm = _head_major(wi[:mp].reshape(bsz, seq, IDX_HEADS), nqb, q_block, IDX_HEADS, 1)
        ki_p = ki[:mp].reshape(bsz, nch, chunk, IDX_DIM)
        k_p = k[:mp].reshape(bsz, nch, chunk, aw)
        v_p = v[:mp].reshape(bsz, nch, chunk, aw)
        y_att_p = _prompt_attention(q_p, qi_hm, w_hm,
                                    jnp.swapaxes(ki_p, 2, 3).astype(BF16),
                                    jnp.swapaxes(k_p, 2, 3).astype(BF16),
                                    v_p.astype(BF16), topk_p, q_block, chunk)

        q_s = q[mp:].reshape(db, dseq, aw)
        k_s = k[mp:].reshape(db, dseq, aw)
        v_s = v[mp:].reshape(db, dseq, aw)
        ki_s = ki[mp:].reshape(db, dseq, IDX_DIM)
        qi_s = _head_major(qi[mp:].reshape(db, dseq, IDX_HEADS * IDX_DIM), 1, dseq, IDX_HEADS, IDX_DIM)[:, 0]
        w_s = _head_major(wi[mp:].reshape(db, dseq, IDX_HEADS), 1, dseq, IDX_HEADS, 1)[:, 0]
        bias = _sample_select(page_table, qi_s, w_s, cache_kidx, ki_s, l, topk_s)
        y_att_s = _sample_attention(page_table, q_s, bias, cache_k2, cache_v2, k_s, v_s, l)

        y_ssm = jnp.concatenate([y_ssm_p.reshape(mp, sw), y_ssm_s], axis=0)
        y_att = jnp.concatenate([y_att_p.reshape(mp, aw), y_att_s.reshape(ms, aw)], axis=0)
        w_o = w_out[l].astype(BF16)
        x = _mix(x, y_ssm, y_att, w_glu[l].astype(BF16), b_glu[l].reshape(1, sw), w_o[:sw], w_o[sw:],
                 g_ln[1], b_ln[1], alpha, tm)
        tf2 = _ff_tile(ffn2_down.shape[1])
        x = _ffn(x, ffn2_up[l].astype(BF16), ffn2_down[l].astype(BF16), g_ln[2], b_ln[2], alpha, tm, tf2)

        outs["kp"].append(k[:mp].reshape(bsz, seq, n_heads, HEAD_DIM))
        outs["vp"].append(v[:mp].reshape(bsz, seq, n_heads, HEAD_DIM))
        outs["kip"].append(ki[:mp].reshape(bsz, seq, IDX_DIM))
        outs["hrp"].append(hr_p.reshape(bsz, n_groups, SSM_STATE))
        outs["hip"].append(hi_p.reshape(bsz, n_groups, SSM_STATE))
        outs["ks"].append(k_s.reshape(db, dseq, n_heads, HEAD_DIM))
        outs["vs"].append(v_s.reshape(db, dseq, n_heads, HEAD_DIM))
        outs["kis"].append(ki_s)
        outs["hrs"].append(hr_s.reshape(db, n_groups, SSM_STATE))
        outs["his"].append(hi_s.reshape(db, n_groups, SSM_STATE))

    st = lambda name: jnp.stack(outs[name])
    return (x[:mp].reshape(bsz, seq, d_model), x[mp:].reshape(db, dseq, d_model),
            st("kp"), st("vp"), st("kip"), st("hrp"), st("hip"),
            st("ks"), st("vs"), st("kis"), st("hrs"), st("his"))
```

```python
import functools
import math

import jax
import jax.numpy as jnp
from jax import lax
from jax.experimental import pallas as pl
from jax.experimental.pallas import tpu as pltpu

F32 = jnp.float32
BF16 = jnp.bfloat16

SSM_GROUP = 16
SSM_STATE = 64
HEAD_DIM = 64
IDX_HEADS = 8
IDX_DIM = 32
TOPK_MAX = 256
LN_EPS = 1e-5

LANE = 128
SUBLANE = 8
VMEM_LIMIT_BYTES = 56 * 1024 * 1024

NEG_INF = float("-inf")
LOG2_E = 1.4426950408889634
_LOWEST_F32 = -3.4028234663852886e38
KEY_LOWEST = -2139095040
KEY_POS_INF = 0x7F800000


def _cparams(*sem):
    return pltpu.CompilerParams(dimension_semantics=sem, vmem_limit_bytes=VMEM_LIMIT_BYTES)


def _layer_norm(y, g, b):
    mu = jnp.mean(y, axis=-1, keepdims=True)
    yc = y - mu
    var = jnp.mean(yc * yc, axis=-1, keepdims=True)
    return yc * lax.rsqrt(var + LN_EPS) * g + b


def _split_bf16(x):
    hi = x.astype(BF16)
    lo = (x - hi.astype(F32)).astype(BF16)
    return hi, lo


def _dot(a, b):
    return jnp.dot(a, b, preferred_element_type=F32)


def _dot_nt(a, b):
    return lax.dot_general(a, b, (((1,), (1,)), ((), ())), preferred_element_type=F32)


def _dot3(a_hi, a_lo, b_hi, b_lo):
    return _dot(a_hi, b_hi) + _dot(a_lo, b_hi) + _dot(a_hi, b_lo)


def _ffn_kernel(x_ref, wg_ref, wu_ref, wd_ref, g_ref, b_ref, o_ref, xb_ref, acc_ref, *, alpha):
    j = pl.program_id(1)

    @pl.when(j == 0)
    def _():
        xb_ref[...] = x_ref[...].astype(BF16)
        acc_ref[...] = jnp.zeros_like(acc_ref)

    xb = xb_ref[...]
    gate = _dot(xb, wg_ref[...])
    up = _dot(xb, wu_ref[...])
    act = (gate * jax.nn.sigmoid(gate)) * up
    acc_ref[...] += _dot(act.astype(BF16), wd_ref[...])

    @pl.when(j == pl.num_programs(1) - 1)
    def _():
        y = alpha * x_ref[...] + 0.5 * acc_ref[...]
        o_ref[...] = _layer_norm(y, g_ref[...], b_ref[...])


def _ffn(x, w_up, w_down, g, b, alpha, tm, tf):
    m, d = x.shape
    d_ff = w_down.shape[0]
    nf = d_ff // tf
    assert m % tm == 0 and d_ff % tf == 0
    return pl.pallas_call(
        functools.partial(_ffn_kernel, alpha=alpha),
        grid=(m // tm, nf),
        in_specs=[
            pl.BlockSpec((tm, d), lambda i, j: (i, 0)),
            pl.BlockSpec((d, tf), lambda i, j: (0, j)),
            pl.BlockSpec((d, tf), lambda i, j: (0, j + nf)),
            pl.BlockSpec((tf, d), lambda i, j: (j, 0)),
            pl.BlockSpec((1, d), lambda i, j: (0, 0)),
            pl.BlockSpec((1, d), lambda i, j: (0, 0)),
        ],
        out_specs=pl.BlockSpec((tm, d), lambda i, j: (i, 0)),
        out_shape=jax.ShapeDtypeStruct((m, d), F32),
        scratch_shapes=[pltpu.VMEM((tm, d), BF16), pltpu.VMEM((tm, d), F32)],
        compiler_params=_cparams("parallel", "arbitrary"),
        name="swiglu_ln",
    )(x, w_up, w_up, w_down, g, b)


def _inproj_kernel(x_ref, w_ref, *o_refs):
    y = _dot(x_ref[...].astype(BF16), w_ref[...])
    off = 0
    for o_ref in o_refs:
        n = o_ref.shape[1]
        o_ref[...] = y[:, off:off + n]
        off += n


def _inproj(x, w, widths, tm):
    m, d = x.shape
    n = w.shape[1]
    assert sum(widths) == n and m % tm == 0
    return pl.pallas_call(
        _inproj_kernel,
        grid=(m // tm,),
        in_specs=[pl.BlockSpec((tm, d), lambda i: (i, 0)),
                  pl.BlockSpec((d, n), lambda i: (0, 0))],
        out_specs=[pl.BlockSpec((tm, wd), lambda i: (i, 0)) for wd in widths],
        out_shape=[jax.ShapeDtypeStruct((m, wd), F32) for wd in widths],
        compiler_params=_cparams("parallel"),
        name="in_proj",
    )(x, w)


def _s5_param_kernel(lre_ref, lim_ref, ls_ref, bre_ref, bim_ref, abre_ref, abim_ref, bbre_ref, bbim_ref):
    lam_re = lre_ref[...]
    lam_im = lim_ref[...]
    step = jnp.exp(ls_ref[...])
    mag = jnp.exp(lam_re * step)
    ang = lam_im * step
    ab_re = mag * jnp.cos(ang)
    ab_im = mag * jnp.sin(ang)
    nr = ab_re - 1.0
    ni = ab_im
    den = lam_re * lam_re + lam_im * lam_im
    f_re = (nr * lam_re + ni * lam_im) / den
    f_im = (ni * lam_re - nr * lam_im) / den
    abre_ref[...] = ab_re
    abim_ref[...] = ab_im
    b_re = bre_ref[...]
    b_im = bim_ref[...]
    bbre_ref[...] = f_re[:, None, :] * b_re - f_im[:, None, :] * b_im
    bbim_ref[...] = f_re[:, None, :] * b_im + f_im[:, None, :] * b_re


def _s5_params(lam_re, lam_im, log_step, b_re, b_im):
    g, p = lam_re.shape
    h = b_re.shape[2]
    b_re_t = jnp.swapaxes(b_re, 1, 2)
    b_im_t = jnp.swapaxes(b_im, 1, 2)
    return pl.pallas_call(
        _s5_param_kernel,
        out_shape=[jax.ShapeDtypeStruct((g, p), F32), jax.ShapeDtypeStruct((g, p), F32),
                   jax.ShapeDtypeStruct((g, h, p), F32), jax.ShapeDtypeStruct((g, h, p), F32)],
        name="s5_params",
    )(lam_re, lam_im, log_step.reshape(g, 1), b_re_t, b_im_t)


def _block_diag_in(bb_t):
    g, h, p = bb_t.shape
    per = LANE // h
    x = bb_t.reshape(g // per, per, h, p)
    eye = jnp.eye(per, dtype=F32)
    return jnp.einsum("jahp,ab->jahbp", x, eye).reshape(g // per, per * h, per * p)


def _block_diag_out(c):
    g, h, p = c.shape
    per = LANE // h
    x = c.reshape(g // per, per, h, p)
    eye = jnp.eye(per, dtype=F32)
    return jnp.einsum("jahp,ab->japbh", x, eye).reshape(g // per, per * p, per * h)


def _hi_lo_stack(w_re, w_im):
    w = jnp.stack([w_re, w_im])
    hi = w.astype(BF16)
    lo = (w - hi.astype(F32)).astype(BF16)
    return jnp.stack([hi, lo])


def _s5_kernel(u_ref, h0re_ref, h0im_ref, abre_ref, abim_ref, wb_ref, wc_ref, d_ref,
               y_ref, hre_out, him_out, hre_s, him_s, stre_s, stim_s, *, n_steps, rows_per_step, row_group):
    c = pl.program_id(1)
    nb = wb_ref.shape[2]
    sw = wb_ref.shape[4]

    @pl.when(c == 0)
    def _():
        stre_s[...] = h0re_ref[0]
        stim_s[...] = h0im_ref[0]

    u = u_ref[0]
    u_hi, u_lo = _split_bf16(u)
    for j in range(nb):
        uh = u_hi[:, j * LANE:(j + 1) * LANE]
        ul = u_lo[:, j * LANE:(j + 1) * LANE]
        hre_s[:, j * sw:(j + 1) * sw] = _dot3(uh, ul, wb_ref[0, 0, j], wb_ref[1, 0, j])
        him_s[:, j * sw:(j + 1) * sw] = _dot3(uh, ul, wb_ref[0, 1, j], wb_ref[1, 1, j])

    a_re = abre_ref[...]
    a_im = abim_ref[...]

    def group_body(rg, carry):
        r0 = pl.multiple_of(rg * row_group, row_group)

        def step_body(t, h):
            h_re, h_im = h
            row = t * rows_per_step + r0
            b_re = hre_s[pl.ds(row, row_group), :]
            b_im = him_s[pl.ds(row, row_group), :]
            n_re = a_re * h_re - a_im * h_im + b_re
            n_im = a_re * h_im + a_im * h_re + b_im
            hre_s[pl.ds(row, row_group), :] = n_re
            him_s[pl.ds(row, row_group), :] = n_im
            return n_re, n_im

        h0 = (stre_s[pl.ds(r0, row_group), :], stim_s[pl.ds(r0, row_group), :])
        h_re, h_im = lax.fori_loop(0, n_steps, step_body, h0)
        stre_s[pl.ds(r0, row_group), :] = h_re
        stim_s[pl.ds(r0, row_group), :] = h_im
        return carry

    lax.fori_loop(0, rows_per_step // row_group, group_body, 0)

    for j in range(nb):
        hr_hi, hr_lo = _split_bf16(hre_s[:, j * sw:(j + 1) * sw])
        hi_hi, hi_lo = _split_bf16(him_s[:, j * sw:(j + 1) * sw])
        yj = (_dot3(hr_hi, hr_lo, wc_ref[0, 0, j], wc_ref[1, 0, j])
              - _dot3(hi_hi, hi_lo, wc_ref[0, 1, j], wc_ref[1, 1, j]))
        sl = slice(j * LANE, (j + 1) * LANE)
        y_ref[0, :, sl] = yj + d_ref[:, sl] * u[:, sl]

    @pl.when(c == pl.num_programs(1) - 1)
    def _():
        hre_out[0] = stre_s[...]
        him_out[0] = stim_s[...]


def _s5_scan(u, h0_re, h0_im, ab_re, ab_im, wb, wc, d, n_steps, rows_per_step):
    nbatch, rows, w = u.shape
    s = ab_re.shape[1]
    r = rows_per_step
    chunk_rows = n_steps * r
    assert rows % chunk_rows == 0
    row_group = min(r, SUBLANE)
    assert r % row_group == 0
    kern = functools.partial(_s5_kernel, n_steps=n_steps, rows_per_step=r, row_group=row_group)
    const = lambda shape: pl.BlockSpec(shape, lambda n, c: (0,) * len(shape))
    return pl.pallas_call(
        kern,
        grid=(nbatch, rows // chunk_rows),
        in_specs=[
            pl.BlockSpec((1, chunk_rows, w), lambda n, c: (n, c, 0)),
            pl.BlockSpec((1, r, s), lambda n, c: (n, 0, 0)),
            pl.BlockSpec((1, r, s), lambda n, c: (n, 0, 0)),
            const((1, s)), const((1, s)), const(wb.shape), const(wc.shape), const((1, w)),
        ],
        out_specs=[
            pl.BlockSpec((1, chunk_rows, w), lambda n, c: (n, c, 0)),
            pl.BlockSpec((1, r, s), lambda n, c: (n, 0, 0)),
            pl.BlockSpec((1, r, s), lambda n, c: (n, 0, 0)),
        ],
        out_shape=[jax.ShapeDtypeStruct((nbatch, rows, w), F32),
                   jax.ShapeDtypeStruct((nbatch, r, s), F32),
                   jax.ShapeDtypeStruct((nbatch, r, s), F32)],
        scratch_shapes=[pltpu.VMEM((chunk_rows, s), F32), pltpu.VMEM((chunk_rows, s), F32),
                        pltpu.VMEM((r, s), F32), pltpu.VMEM((r, s), F32)],
        compiler_params=_cparams("parallel", "arbitrary"),
        name="s5_scan",
    )(u, h0_re, h0_im, ab_re, ab_im, wb, wc, d)


def _key_to_f32(k):
    bits = jnp.where(k >= 0, k, k ^ jnp.int32(0x7FFFFFFF))
    return lax.bitcast_convert_type(bits, F32)


def _kth_threshold(count_ge, shape, topk):
    kf = float(topk)
    c_zero = count_ge(jnp.zeros(shape, F32))
    c_low = count_ge(jnp.full(shape, _LOWEST_F32, F32))
    pos = c_zero >= kf
    lo = jnp.where(pos, 0, KEY_LOWEST).astype(jnp.int32)
    hi = jnp.where(pos, KEY_POS_INF + 1, 0).astype(jnp.int32)
    cnt_lo = jnp.where(pos, c_zero, c_low)

    def body(_, carry):
        lo, hi, cnt_lo = carry
        mid = lo + ((hi - lo) >> 1)
        cnt = count_ge(_key_to_f32(mid))
        ok = cnt >= kf
        return jnp.where(ok, mid, lo), jnp.where(ok, hi, mid), jnp.where(ok, cnt, cnt_lo)

    lo, hi, cnt_lo = lax.fori_loop(0, 31, body, (lo, hi, cnt_lo))
    return _key_to_f32(lo), cnt_lo


def _tie_index_bound(count_eq_le, count_gt, cnt_ge, shape, topk, n_keys):
    need = float(topk) - count_gt
    tied = cnt_ge > float(topk)
    lo = jnp.full(shape, -1, jnp.int32)
    hi = jnp.full(shape, n_keys - 1, jnp.int32)

    def body(_, carry):
        lo, hi = carry
        mid = lo + ((hi - lo) >> 1)
        ok = count_eq_le(mid) >= need
        return jnp.where(ok, lo, mid), jnp.where(ok, mid, hi)

    n_iter = max(1, math.ceil(math.log2(n_keys + 1)))
    lo, hi = lax.fori_loop(0, n_iter, body, (lo, hi))
    return jnp.where(tied, hi, n_keys)


def _prompt_attn_kernel(qt_ref, qit_ref, w_ref, ki_ref, k_ref, vt_ref, o_ref,
                        s_ref, m_ref, l_ref, acc_ref, *, topk, q_block, chunk, n_heads):
    qb = pl.program_id(1)
    n_keys = k_ref.shape[1] * chunk
    n_chunks = (qb * q_block + q_block + chunk - 1) // chunk
    k_row = lax.broadcasted_iota(jnp.int32, (chunk, q_block), 0)
    q_pos = qb * q_block + lax.broadcasted_iota(jnp.int32, (chunk, q_block), 1)

    qit = qit_ref[0, 0].astype(BF16)
    w_row = w_ref[0, 0]

    def score_body(c, carry):
        st = _dot(ki_ref[0, c], qit)
        st = jnp.maximum(st, 0.0) * w_row
        sc = st[:, 0:q_block]
        for h in range(1, IDX_HEADS):
            sc = sc + st[:, h * q_block:(h + 1) * q_block]
        s_ref[c] = jnp.where(c * chunk + k_row <= q_pos, sc, NEG_INF)
        return carry

    lax.fori_loop(0, n_chunks, score_body, 0)

    row = (1, q_block)
    fold = min(chunk, 64)

    def count(pred):
        def body(c, acc):
            hits = jnp.where(pred(c, s_ref[c]), 1.0, 0.0)
            return acc + jnp.sum(hits.reshape(chunk // fold, fold, q_block), axis=0)
        acc = lax.fori_loop(0, n_chunks, body, jnp.zeros((fold, q_block), F32))
        return jnp.sum(acc, axis=0, keepdims=True)

    thr, cnt_ge = _kth_threshold(lambda t: count(lambda c, s: s >= t), row, topk)

    @pl.when(jnp.max(cnt_ge) > float(topk))
    def _():
        cnt_gt = count(lambda c, s: s > thr)
        bound = _tie_index_bound(
            lambda j: count(lambda c, s: (s == thr) & (c * chunk + k_row <= j)),
            cnt_gt, cnt_ge, row, topk, n_keys)

        def drop_body(c, carry):
            s = s_ref[c]
            s_ref[c] = jnp.where((s == thr) & (c * chunk + k_row > bound), NEG_INF, s)
            return carry

        lax.fori_loop(0, n_chunks, drop_body, 0)

    m_ref[...] = jnp.full(m_ref.shape, NEG_INF, F32)
    l_ref[...] = jnp.zeros(l_ref.shape, F32)
    acc_ref[...] = jnp.zeros(acc_ref.shape, F32)
    qt = qt_ref[0, 0] * (HEAD_DIM ** -0.5 * LOG2_E)
    head_of_dim = lax.broadcasted_iota(jnp.int32, (LANE, 2 * q_block), 0) // HEAD_DIM
    head_of_col = lax.broadcasted_iota(jnp.int32, (LANE, 2 * q_block), 1) // q_block
    q_pairs = []
    for j in range(n_heads // 2):
        qj = qt[j * LANE:(j + 1) * LANE, :]
        q_pairs.append(jnp.where(head_of_dim == head_of_col, jnp.concatenate([qj, qj], axis=1), 0.0).astype(BF16))

    def attn_body(c, carry):
        sel = s_ref[c] >= thr
        logits = [_dot(k_ref[0, c, :, j * LANE:(j + 1) * LANE], q_pairs[j]) for j in range(n_heads // 2)]
        for h in range(n_heads):
            j, hh = divmod(h, 2)
            x = jnp.where(sel, logits[j][:, hh * q_block:(hh + 1) * q_block], NEG_INF)
            m_old = m_ref[h]
            m_new = jnp.maximum(m_old, jnp.max(x, axis=0, keepdims=True))
            m_safe = jnp.where(m_new == NEG_INF, 0.0, m_new)
            p = jnp.exp2(x - m_safe)
            scale = jnp.exp2(m_old - m_safe)
            l_ref[h] = scale * l_ref[h] + jnp.sum(p, axis=0, keepdims=True)
            pv = _dot(vt_ref[0, c, h * HEAD_DIM:(h + 1) * HEAD_DIM, :], p.astype(BF16))
            acc_ref[h] = scale * acc_ref[h] + pv
            m_ref[h] = m_new
        return carry

    lax.fori_loop(0, n_chunks, attn_body, 0)

    out_t = jnp.concatenate([acc_ref[h] / l_ref[h] for h in range(n_heads)], axis=0)
    o_ref[0] = out_t.T


def _prompt_attention(q_t, qi_t, w_row, ki, k, v_t, topk, q_block, chunk):
    b, nqb, aw, _ = q_t.shape
    nch = k.shape[1]
    l = nch * chunk
    n_heads = aw // HEAD_DIM
    assert n_heads % 2 == 0 and nqb * q_block == l and chunk % q_block == 0 and 2 * HEAD_DIM == LANE
    kern = functools.partial(_prompt_attn_kernel, topk=topk, q_block=q_block, chunk=chunk, n_heads=n_heads)
    cols = IDX_HEADS * q_block
    return pl.pallas_call(
        kern,
        grid=(b, nqb),
        in_specs=[
            pl.BlockSpec((1, 1, aw, q_block), lambda i, j: (i, j, 0, 0)),
            pl.BlockSpec((1, 1, IDX_DIM, cols), lambda i, j: (i, j, 0, 0)),
            pl.BlockSpec((1, 1, 1, cols), lambda i, j: (i, j, 0, 0)),
            pl.BlockSpec((1, nch, chunk, IDX_DIM), lambda i, j: (i, 0, 0, 0)),
            pl.BlockSpec((1, nch, chunk, aw), lambda i, j: (i, 0, 0, 0)),
            pl.BlockSpec((1, nch, aw, chunk), lambda i, j: (i, 0, 0, 0)),
        ],
        out_specs=pl.BlockSpec((1, q_block, aw), lambda i, j: (i, j, 0)),
        out_shape=jax.ShapeDtypeStruct((b, l, aw), F32),
        scratch_shapes=[pltpu.VMEM((nch, chunk, q_block), F32),
                        pltpu.VMEM((n_heads, 1, q_block), F32),
                        pltpu.VMEM((n_heads, 1, q_block), F32),
                        pltpu.VMEM((n_heads, HEAD_DIM, q_block), F32)],
        compiler_params=_cparams("parallel", "arbitrary"),
        name="prompt_attention",
    )(q_t, qi_t, w_row, ki, k, v_t)


def _sample_select_kernel(pt_ref, qi_ref, w_ref, kpage_ref, knew_ref, bias_ref, s_ref, kpad_ref, *, topk, page):
    p = pl.program_id(1)
    n_pages = pl.num_programs(1)
    n_q = knew_ref.shape[1]
    qi = qi_ref[0].astype(BF16)
    w_col = w_ref[0]

    def scores(keys):
        st = _dot_nt(qi, keys.astype(BF16))
        st = jnp.maximum(st, 0.0) * w_col
        return jnp.sum(st.reshape(IDX_HEADS, n_q, page), axis=0)

    s_ref[p] = scores(kpage_ref[...])

    @pl.when(p == n_pages - 1)
    def _():
        kpad_ref[...] = jnp.zeros(kpad_ref.shape, F32)
        kpad_ref[0:n_q, :] = knew_ref[0]
        q_row = lax.broadcasted_iota(jnp.int32, (n_q, page), 0)
        k_col = lax.broadcasted_iota(jnp.int32, (n_q, page), 1)
        s_ref[n_pages] = jnp.where(k_col <= q_row, scores(kpad_ref[...]), NEG_INF)

        col = (n_q, 1)
        key_idx = (lax.broadcasted_iota(jnp.int32, s_ref.shape, 0) * page
                   + lax.broadcasted_iota(jnp.int32, s_ref.shape, 2))

        def count(pred):
            hits = jnp.sum(pred(s_ref[...]).astype(F32), axis=0)
            return jnp.sum(hits, axis=1, keepdims=True)

        thr, cnt_ge = _kth_threshold(lambda t: count(lambda s: s >= t[None]), col, topk)

        @pl.when(jnp.max(cnt_ge) > float(topk))
        def _():
            cnt_gt = count(lambda s: s > thr[None])
            bound = _tie_index_bound(
                lambda j: count(lambda s: (s == thr[None]) & (key_idx <= j[None])),
                cnt_gt, cnt_ge, col, topk, (n_pages + 1) * page)
            s = s_ref[...]
            s_ref[...] = jnp.where((s == thr[None]) & (key_idx > bound[None]), NEG_INF, s)

        bias_ref[0] = jnp.where(s_ref[...] >= thr[None], 0.0, NEG_INF)


def _sample_select(page_table, qi_hm, w_col, cache_kidx, ki_new, layer, topk):
    db, n_pages = page_table.shape
    page = cache_kidx.shape[2]
    n_q = ki_new.shape[1]
    rows = IDX_HEADS * n_q
    kern = functools.partial(_sample_select_kernel, topk=topk, page=page)
    grid_spec = pltpu.PrefetchScalarGridSpec(
        num_scalar_prefetch=1,
        grid=(db, n_pages),
        in_specs=[
            pl.BlockSpec((1, rows, IDX_DIM), lambda b, p, pt: (b, 0, 0)),
            pl.BlockSpec((1, rows, 1), lambda b, p, pt: (b, 0, 0)),
            pl.BlockSpec((None, None, page, IDX_DIM), lambda b, p, pt: (layer, pt[b, p], 0, 0)),
            pl.BlockSpec((1, n_q, IDX_DIM), lambda b, p, pt: (b, 0, 0)),
        ],
        out_specs=pl.BlockSpec((1, n_pages + 1, n_q, page), lambda b, p, pt: (b, 0, 0, 0)),
        scratch_shapes=[pltpu.VMEM((n_pages + 1, n_q, page), F32), pltpu.VMEM((page, IDX_DIM), F32)],
    )
    return pl.pallas_call(
        kern,
        grid_spec=grid_spec,
        out_shape=jax.ShapeDtypeStruct((db, n_pages + 1, n_q, page), F32),
        compiler_params=_cparams("parallel", "arbitrary"),
        name="sample_select",
    )(page_table, qi_hm, w_col, cache_kidx, ki_new)


def _sample_attn_kernel(pt_ref, q_ref, bias_ref, kpage_ref, vpage_ref, knew_ref, vnew_ref, o_ref,
                        qz_ref, m_ref, l_ref, acc_ref, kpad_ref, vpad_ref, *, n_heads):
    p = pl.program_id(1)
    n_pages = pl.num_programs(1)
    n_q = q_ref.shape[1]
    aw = q_ref.shape[2]
    rows = n_heads * n_q
    head_of_row = lax.broadcasted_iota(jnp.int32, (rows, aw), 0) // n_q
    head_of_lane = lax.broadcasted_iota(jnp.int32, (rows, aw), 1) // HEAD_DIM

    @pl.when(p == 0)
    def _():
        q = q_ref[0] * (HEAD_DIM ** -0.5)
        q_rep = jnp.concatenate([q] * n_heads, axis=0)
        qz_ref[...] = jnp.where(head_of_row == head_of_lane, q_rep, 0.0).astype(BF16)
        m_ref[...] = jnp.full(m_ref.shape, NEG_INF, F32)
        l_ref[...] = jnp.zeros(l_ref.shape, F32)
        acc_ref[...] = jnp.zeros(acc_ref.shape, F32)

    def attend(keys, values, bias):
        logits = _dot_nt(qz_ref[...], keys.astype(BF16)) + jnp.concatenate([bias] * n_heads, axis=0)
        m_old = m_ref[...]
        m_new = jnp.maximum(m_old, jnp.max(logits, axis=1, keepdims=True))
        m_safe = jnp.where(m_new == NEG_INF, 0.0, m_new)
        pr = jnp.exp(logits - m_safe)
        scale = jnp.exp(m_old - m_safe)
        l_ref[...] = scale * l_ref[...] + jnp.sum(pr, axis=1, keepdims=True)
        acc_ref[...] = scale * acc_ref[...] + _dot(pr.astype(BF16), values.astype(BF16))
        m_ref[...] = m_new

    attend(kpage_ref[...], vpage_ref[...], bias_ref[0, p])

    @pl.when(p == n_pages - 1)
    def _():
        kpad_ref[...] = jnp.zeros(kpad_ref.shape, F32)
        vpad_ref[...] = jnp.zeros(vpad_ref.shape, F32)
        kpad_ref[0:n_q, :] = knew_ref[0]
        vpad_ref[0:n_q, :] = vnew_ref[0]
        attend(kpad_ref[...], vpad_ref[...], bias_ref[0, n_pages])
        out = jnp.where(head_of_row == head_of_lane, acc_ref[...] / l_ref[...], 0.0)
        o_ref[0] = jnp.sum(out.reshape(n_heads, n_q, aw), axis=0)


def _sample_attention(page_table, q, bias, cache_k, cache_v, k_new, v_new, layer):
    db, n_pages = page_table.shape
    page = cache_k.shape[2]
    _, n_q, aw = q.shape
    n_heads = aw // HEAD_DIM
    rows = n_heads * n_q
    kern = functools.partial(_sample_attn_kernel, n_heads=n_heads)
    per_b = lambda shape: pl.BlockSpec(shape, lambda b, p, pt: (b,) + (0,) * (len(shape) - 1))
    page_spec = pl.BlockSpec((None, None, page, aw), lambda b, p, pt: (layer, pt[b, p], 0, 0))
    grid_spec = pltpu.PrefetchScalarGridSpec(
        num_scalar_prefetch=1,
        grid=(db, n_pages),
        in_specs=[per_b((1, n_q, aw)), per_b((1, n_pages + 1, n_q, page)), page_spec, page_spec,
                  per_b((1, n_q, aw)), per_b((1, n_q, aw))],
        out_specs=per_b((1, n_q, aw)),
        scratch_shapes=[pltpu.VMEM((rows, aw), BF16), pltpu.VMEM((rows, 1), F32), pltpu.VMEM((rows, 1), F32),
                        pltpu.VMEM((rows, aw), F32), pltpu.VMEM((page, aw), F32), pltpu.VMEM((page, aw), F32)],
    )
    return pl.pallas_call(
        kern,
        grid_spec=grid_spec,
        out_shape=jax.ShapeDtypeStruct((db, n_q, aw), F32),
        compiler_params=_cparams("parallel", "arbitrary"),
        name="sample_attention",
    )(page_table, q, bias, cache_k, cache_v, k_new, v_new)


def _mix_kernel(x_ref, ys_ref, ya_ref, wglu_ref, bglu_ref, wos_ref, woa_ref, g_ref, b_ref, o_ref, *, alpha):
    y = ys_ref[...]
    gel = 0.5 * y * (1.0 + jnp.tanh(math.sqrt(2.0 / math.pi) * (y + 0.044715 * (y * y * y))))
    gate = jax.nn.sigmoid(_dot(gel.astype(BF16), wglu_ref[...]) + bglu_ref[...])
    y_ssm = gel * gate
    mix = _dot(y_ssm.astype(BF16), wos_ref[...]) + _dot(ya_ref[...].astype(BF16), woa_ref[...])
    o_ref[...] = _layer_norm(alpha * x_ref[...] + mix, g_ref[...], b_ref[...])


def _mix(x, y_ssm, y_att, w_glu, b_glu, w_out_ssm, w_out_att, g, b, alpha, tm):
    m, d = x.shape
    sw = y_ssm.shape[1]
    aw = y_att.shape[1]
    const = lambda shape: pl.BlockSpec(shape, lambda i: (0, 0))
    return pl.pallas_call(
        functools.partial(_mix_kernel, alpha=alpha),
        grid=(m // tm,),
        in_specs=[pl.BlockSpec((tm, d), lambda i: (i, 0)),
                  pl.BlockSpec((tm, sw), lambda i: (i, 0)),
                  pl.BlockSpec((tm, aw), lambda i: (i, 0)),
                  const((sw, sw)), const((1, sw)), const((sw, d)), const((aw, d)), const((1, d)), const((1, d))],
        out_specs=pl.BlockSpec((tm, d), lambda i: (i, 0)),
        out_shape=jax.ShapeDtypeStruct((m, d), F32),
        compiler_params=_cparams("parallel"),
        name="glu_outproj_ln",
    )(x, y_ssm, y_att, w_glu, b_glu, w_out_ssm, w_out_att, g, b)


def _row_tile(m, cap=512):
    t = cap
    while m % t:
        t //= 2
    return t


def _ff_tile(d_ff, cap=512):
    best = LANE
    for t in range(LANE, cap + 1, LANE):
        if d_ff % t == 0:
            best = t
    return best


def _head_major(x, n_blocks, block, n_heads, dim):
    b = x.shape[0]
    x = x.reshape(b, n_blocks, block, n_heads, dim)
    return jnp.swapaxes(x, 2, 3).reshape(b, n_blocks, n_heads * block, dim)


def kernel(x_prompt, x_sample, cache_k, cache_v, cache_kidx, state_ssm_re, state_ssm_im, page_table,
           w_in, ssm_lambda_re, ssm_lambda_im, ssm_log_step, ssm_b_re, ssm_b_im, ssm_c_re, ssm_c_im,
           ssm_d, w_glu, b_glu, w_out, ffn1_up, ffn1_down, ffn2_up, ffn2_down, ln_g, ln_b):
    bsz, seq, d_model = x_prompt.shape
    db, dseq, _ = x_sample.shape
    depth = w_in.shape[0]
    sw = d_model // 2
    aw = d_model - sw
    n_groups = sw // SSM_GROUP
    n_state = n_groups * SSM_STATE
    n_heads = aw // HEAD_DIM
    n_pool, page = cache_k.shape[1], cache_k.shape[2]
    n_pages = page_table.shape[1]
    past = n_pages * page
    alpha = (2.0 * depth) ** 0.25
    mp = bsz * seq
    ms = db * dseq
    m_all = mp + ms
    tm = _row_tile(m_all)
    q_block = min(128, seq)
    chunk = min(512, seq)
    topk_p = min(TOPK_MAX, seq // 4)
    topk_s = min(TOPK_MAX, (past + dseq) // 4)
    s5_rows = min(256, seq)
    s_groups = max(1, ms // 256)
    s_per = db // s_groups

    cache_k2 = cache_k.reshape(depth, n_pool, page, aw)
    cache_v2 = cache_v.reshape(depth, n_pool, page, aw)
    pad_w = LANE - IDX_DIM - IDX_HEADS
    widths = (sw, aw, aw, aw, IDX_HEADS * IDX_DIM, LANE)

    x = jnp.concatenate([x_prompt.reshape(mp, d_model), x_sample.reshape(ms, d_model)], axis=0)
    outs = {name: [] for name in ("kp", "vp", "kip", "hrp", "hip", "ks", "vs", "kis", "hrs", "his")}

    for l in range(depth):
        g_ln = ln_g[l][:, None, :]
        b_ln = ln_b[l][:, None, :]
        tf1 = _ff_tile(ffn1_down.shape[1])
        x = _ffn(x, ffn1_up[l].astype(BF16), ffn1_down[l].astype(BF16), g_ln[0], b_ln[0], alpha, tm, tf1)

        w_l = jnp.pad(w_in[l], ((0, 0), (0, pad_w))).astype(BF16)
        u, q, k, v, qi, kiwi = _inproj(x, w_l, widths, tm)
        ki = kiwi[:, :IDX_DIM]
        wi = kiwi[:, IDX_DIM:IDX_DIM + IDX_HEADS]

        ab_re, ab_im, bb_re_t, bb_im_t = _s5_params(ssm_lambda_re[l], ssm_lambda_im[l], ssm_log_step[l],
                                                     ssm_b_re[l], ssm_b_im[l])
        wb = _hi_lo_stack(_block_diag_in(bb_re_t), _block_diag_in(bb_im_t))
        wc = _hi_lo_stack(_block_diag_out(ssm_c_re[l]), _block_diag_out(ssm_c_im[l]))
        ab_re = ab_re.reshape(1, n_state)
        ab_im = ab_im.reshape(1, n_state)
        d_row = ssm_d[l].reshape(1, sw)

        zeros_state = jnp.zeros((bsz, 1, n_state), F32)
        y_ssm_p, hr_p, hi_p = _s5_scan(u[:mp].reshape(bsz, seq, sw), zeros_state, zeros_state,
                                       ab_re, ab_im, wb, wc, d_row, s5_rows, 1)
        u_s = u[mp:].reshape(s_groups, s_per, dseq, sw)
        u_s = jnp.swapaxes(u_s, 1, 2).reshape(s_groups, dseq * s_per, sw)
        y_ssm_s, hr_s, hi_s = _s5_scan(u_s, state_ssm_re[l].reshape(s_groups, s_per, n_state),
                                       state_ssm_im[l].reshape(s_groups, s_per, n_state),
                                       ab_re, ab_im, wb, wc, d_row, dseq, s_per)
        y_ssm_s = jnp.swapaxes(y_ssm_s.reshape(s_groups, dseq, s_per, sw), 1, 2).reshape(ms, sw)

        nqb = seq // q_block
        nch = seq // chunk
        q_t = jnp.swapaxes(q[:mp].reshape(bsz, nqb, q_block, aw), 2, 3)
        qi_t = jnp.transpose(qi[:mp].reshape(bsz, nqb, q_block, IDX_HEADS, IDX_DIM), (0, 1, 4, 3, 2))
        qi_t = qi_t.reshape(bsz, nqb, IDX_DIM, IDX_HEADS * q_block)
        w_t = jnp.swapaxes(wi[:mp].reshape(bsz, nqb, q_block, IDX_HEADS), 2, 3)
        w_t = w_t.reshape(bsz, nqb, 1, IDX_HEADS * q_block)
        ki_p = ki[:mp].reshape(bsz, nch, chunk, IDX_DIM).astype(BF16)
        k_p = k[:mp].reshape(bsz, nch, chunk, aw).astype(BF16)
        v_t = jnp.swapaxes(v[:mp].reshape(bsz, nch, chunk, aw), 2, 3).astype(BF16)
        y_att_p = _prompt_attention(q_t, qi_t, w_t, ki_p, k_p, v_t, topk_p, q_block, chunk)

        q_s = q[mp:].reshape(db, dseq, aw)
        k_s = k[mp:].reshape(db, dseq, aw)
        v_s = v[mp:].reshape(db, dseq, aw)
        ki_s = ki[mp:].reshape(db, dseq, IDX_DIM)
        qi_s = _head_major(qi[mp:].reshape(db, dseq, IDX_HEADS * IDX_DIM), 1, dseq, IDX_HEADS, IDX_DIM)[:, 0]
        w_s = _head_major(wi[mp:].reshape(db, dseq, IDX_HEADS), 1, dseq, IDX_HEADS, 1)[:, 0]
        bias = _sample_select(page_table, qi_s, w_s, cache_kidx, ki_s, l, topk_s)
        y_att_s = _sample_attention(page_table, q_s, bias, cache_k2, cache_v2, k_s, v_s, l)

        y_ssm = jnp.concatenate([y_ssm_p.reshape(mp, sw), y_ssm_s], axis=0)
        y_att = jnp.concatenate([y_att_p.reshape(mp, aw), y_att_s.reshape(ms, aw)], axis=0)
        w_o = w_out[l].astype(BF16)
        x = _mix(x, y_ssm, y_att, w_glu[l].astype(BF16), b_glu[l].reshape(1, sw), w_o[:sw], w_o[sw:],
                 g_ln[1], b_ln[1], alpha, tm)
        tf2 = _ff_tile(ffn2_down.shape[1])
        x = _ffn(x, ffn2_up[l].astype(BF16), ffn2_down[l].astype(BF16), g_ln[2], b_ln[2], alpha, tm, tf2)

        outs["kp"].append(k[:mp].reshape(bsz, seq, n_heads, HEAD_DIM))
        outs["vp"].append(v[:mp].reshape(bsz, seq, n_heads, HEAD_DIM))
        outs["kip"].append(ki[:mp].reshape(bsz, seq, IDX_DIM))
        outs["hrp"].append(hr_p.reshape(bsz, n_groups, SSM_STATE))
        outs["hip"].append(hi_p.reshape(bsz, n_groups, SSM_STATE))
        outs["ks"].append(k_s.reshape(db, dseq, n_heads, HEAD_DIM))
        outs["vs"].append(v_s.reshape(db, dseq, n_heads, HEAD_DIM))
        outs["kis"].append(ki_s)
        outs["hrs"].append(hr_s.reshape(db, n_groups, SSM_STATE))
        outs["his"].append(hi_s.reshape(db, n_groups, SSM_STATE))

    st = lambda name: jnp.stack(outs[name])
    return (x[:mp].reshape(bsz, seq, d_model), x[mp:].reshape(db, dseq, d_model),
            st("kp"), st("vp"), st("kip"), st("hrp"), st("hip"),
            st("ks"), st("vs"), st("kis"), st("hrs"), st("his"))
```

```python
import functools
import math

import jax
import jax.numpy as jnp
from jax import lax
from jax.experimental import pallas as pl
from jax.experimental.pallas import tpu as pltpu

F32 = jnp.float32
BF16 = jnp.bfloat16

SSM_GROUP = 16
SSM_STATE = 64
HEAD_DIM = 64
IDX_HEADS = 8
IDX_DIM = 32
TOPK_MAX = 256
LN_EPS = 1e-5

LANE = 128
SUBLANE = 8
VMEM_LIMIT_BYTES = 56 * 1024 * 1024

NEG_INF = float("-inf")
LOG2_E = 1.4426950408889634
_LOWEST_F32 = -3.4028234663852886e38
KEY_LOWEST = -2139095040
KEY_POS_INF = 0x7F800000


def _cparams(*sem):
    return pltpu.CompilerParams(dimension_semantics=sem, vmem_limit_bytes=VMEM_LIMIT_BYTES)


def _layer_norm(y, g, b):
    mu = jnp.mean(y, axis=-1, keepdims=True)
    yc = y - mu
    var = jnp.mean(yc * yc, axis=-1, keepdims=True)
    return yc * lax.rsqrt(var + LN_EPS) * g + b


def _split_bf16(x):
    hi = x.astype(BF16)
    lo = (x - hi.astype(F32)).astype(BF16)
    return hi, lo


def _dot(a, b):
    return jnp.dot(a, b, preferred_element_type=F32)


def _dot_nt(a, b):
    return lax.dot_general(a, b, (((1,), (1,)), ((), ())), preferred_element_type=F32)


def _dot3(a_hi, a_lo, b_hi, b_lo):
    return _dot(a_hi, b_hi) + _dot(a_lo, b_hi) + _dot(a_hi, b_lo)


def _ffn_kernel(x_ref, wg_ref, wu_ref, wd_ref, g_ref, b_ref, o_ref, xb_ref, acc_ref, *, alpha):
    j = pl.program_id(1)

    @pl.when(j == 0)
    def _():
        xb_ref[...] = x_ref[...].astype(BF16)
        acc_ref[...] = jnp.zeros_like(acc_ref)

    xb = xb_ref[...]
    gate = _dot(xb, wg_ref[...])
    up = _dot(xb, wu_ref[...])
    act = (gate * jax.nn.sigmoid(gate)) * up
    acc_ref[...] += _dot(act.astype(BF16), wd_ref[...])

    @pl.when(j == pl.num_programs(1) - 1)
    def _():
        y = alpha * x_ref[...] + 0.5 * acc_ref[...]
        o_ref[...] = _layer_norm(y, g_ref[...], b_ref[...])


def _ffn(x, w_up, w_down, g, b, alpha, tm, tf):
    m, d = x.shape
    d_ff = w_down.shape[0]
    nf = d_ff // tf
    assert m % tm == 0 and d_ff % tf == 0
    return pl.pallas_call(
        functools.partial(_ffn_kernel, alpha=alpha),
        grid=(m // tm, nf),
        in_specs=[
            pl.BlockSpec((tm, d), lambda i, j: (i, 0)),
            pl.BlockSpec((d, tf), lambda i, j: (0, j)),
            pl.BlockSpec((d, tf), lambda i, j: (0, j + nf)),
            pl.BlockSpec((tf, d), lambda i, j: (j, 0)),
            pl.BlockSpec((1, d), lambda i, j: (0, 0)),
            pl.BlockSpec((1, d), lambda i, j: (0, 0)),
        ],
        out_specs=pl.BlockSpec((tm, d), lambda i, j: (i, 0)),
        out_shape=jax.ShapeDtypeStruct((m, d), F32),
        scratch_shapes=[pltpu.VMEM((tm, d), BF16), pltpu.VMEM((tm, d), F32)],
        compiler_params=_cparams("parallel", "arbitrary"),
        name="swiglu_ln",
    )(x, w_up, w_up, w_down, g, b)


def _inproj_kernel(x_ref, w_ref, *o_refs):
    y = _dot(x_ref[...].astype(BF16), w_ref[...])
    off = 0
    for o_ref in o_refs:
        n = o_ref.shape[1]
        o_ref[...] = y[:, off:off + n]
        off += n


def _inproj(x, w, widths, tm):
    m, d = x.shape
    n = w.shape[1]
    assert sum(widths) == n and m % tm == 0
    return pl.pallas_call(
        _inproj_kernel,
        grid=(m // tm,),
        in_specs=[pl.BlockSpec((tm, d), lambda i: (i, 0)),
                  pl.BlockSpec((d, n), lambda i: (0, 0))],
        out_specs=[pl.BlockSpec((tm, wd), lambda i: (i, 0)) for wd in widths],
        out_shape=[jax.ShapeDtypeStruct((m, wd), F32) for wd in widths],
        compiler_params=_cparams("parallel"),
        name="in_proj",
    )(x, w)


def _s5_param_kernel(lre_ref, lim_ref, ls_ref, bre_ref, bim_ref, abre_ref, abim_ref, bbre_ref, bbim_ref):
    lam_re = lre_ref[...]
    lam_im = lim_ref[...]
    step = jnp.exp(ls_ref[...])
    mag = jnp.exp(lam_re * step)
    ang = lam_im * step
    ab_re = mag * jnp.cos(ang)
    ab_im = mag * jnp.sin(ang)
    nr = ab_re - 1.0
    ni = ab_im
    den = lam_re * lam_re + lam_im * lam_im
    f_re = (nr * lam_re + ni * lam_im) / den
    f_im = (ni * lam_re - nr * lam_im) / den
    abre_ref[...] = ab_re
    abim_ref[...] = ab_im
    b_re = bre_ref[...]
    b_im = bim_ref[...]
    bbre_ref[...] = f_re[:, None, :] * b_re - f_im[:, None, :] * b_im
    bbim_ref[...] = f_re[:, None, :] * b_im + f_im[:, None, :] * b_re


def _s5_params(lam_re, lam_im, log_step, b_re, b_im):
    g, p = lam_re.shape
    h = b_re.shape[2]
    b_re_t = jnp.swapaxes(b_re, 1, 2)
    b_im_t = jnp.swapaxes(b_im, 1, 2)
    return pl.pallas_call(
        _s5_param_kernel,
        out_shape=[jax.ShapeDtypeStruct((g, p), F32), jax.ShapeDtypeStruct((g, p), F32),
                   jax.ShapeDtypeStruct((g, h, p), F32), jax.ShapeDtypeStruct((g, h, p), F32)],
        name="s5_params",
    )(lam_re, lam_im, log_step.reshape(g, 1), b_re_t, b_im_t)


def _block_diag_in(bb_t):
    g, h, p = bb_t.shape
    per = LANE // h
    x = bb_t.reshape(g // per, per, h, p)
    eye = jnp.eye(per, dtype=F32)
    return jnp.einsum("jahp,ab->jahbp", x, eye).reshape(g // per, per * h, per * p)


def _block_diag_out(c):
    g, h, p = c.shape
    per = LANE // h
    x = c.reshape(g // per, per, h, p)
    eye = jnp.eye(per, dtype=F32)
    return jnp.einsum("jahp,ab->japbh", x, eye).reshape(g // per, per * p, per * h)


def _hi_lo_stack(w_re, w_im):
    w = jnp.stack([w_re, w_im])
    hi = w.astype(BF16)
    lo = (w - hi.astype(F32)).astype(BF16)
    return jnp.stack([hi, lo])


def _s5_kernel(u_ref, h0re_ref, h0im_ref, abre_ref, abim_ref, wb_ref, wc_ref, d_ref,
               y_ref, hre_out, him_out, hre_s, him_s, stre_s, stim_s, *, n_steps, rows_per_step, row_group):
    c = pl.program_id(1)
    nb = wb_ref.shape[2]
    sw = wb_ref.shape[4]

    @pl.when(c == 0)
    def _():
        stre_s[...] = h0re_ref[0]
        stim_s[...] = h0im_ref[0]

    u = u_ref[0]
    u_hi, u_lo = _split_bf16(u)
    for j in range(nb):
        uh = u_hi[:, j * LANE:(j + 1) * LANE]
        ul = u_lo[:, j * LANE:(j + 1) * LANE]
        hre_s[:, j * sw:(j + 1) * sw] = _dot3(uh, ul, wb_ref[0, 0, j], wb_ref[1, 0, j])
        him_s[:, j * sw:(j + 1) * sw] = _dot3(uh, ul, wb_ref[0, 1, j], wb_ref[1, 1, j])

    a_re = abre_ref[...]
    a_im = abim_ref[...]

    def group_body(rg, carry):
        r0 = pl.multiple_of(rg * row_group, row_group)

        def step_body(t, h):
            h_re, h_im = h
            row = t * rows_per_step + r0
            b_re = hre_s[pl.ds(row, row_group), :]
            b_im = him_s[pl.ds(row, row_group), :]
            n_re = a_re * h_re - a_im * h_im + b_re
            n_im = a_re * h_im + a_im * h_re + b_im
            hre_s[pl.ds(row, row_group), :] = n_re
            him_s[pl.ds(row, row_group), :] = n_im
            return n_re, n_im

        h0 = (stre_s[pl.ds(r0, row_group), :], stim_s[pl.ds(r0, row_group), :])
        h_re, h_im = lax.fori_loop(0, n_steps, step_body, h0)
        stre_s[pl.ds(r0, row_group), :] = h_re
        stim_s[pl.ds(r0, row_group), :] = h_im
        return carry

    lax.fori_loop(0, rows_per_step // row_group, group_body, 0)

    for j in range(nb):
        hr_hi, hr_lo = _split_bf16(hre_s[:, j * sw:(j + 1) * sw])
        hi_hi, hi_lo = _split_bf16(him_s[:, j * sw:(j + 1) * sw])
        yj = (_dot3(hr_hi, hr_lo, wc_ref[0, 0, j], wc_ref[1, 0, j])
              - _dot3(hi_hi, hi_lo, wc_ref[0, 1, j], wc_ref[1, 1, j]))
        sl = slice(j * LANE, (j + 1) * LANE)
        y_ref[0, :, sl] = yj + d_ref[:, sl] * u[:, sl]

    @pl.when(c == pl.num_programs(1) - 1)
    def _():
        hre_out[0] = stre_s[...]
        him_out[0] = stim_s[...]


def _s5_scan(u, h0_re, h0_im, ab_re, ab_im, wb, wc, d, n_steps, rows_per_step):
    nbatch, rows, w = u.shape
    s = ab_re.shape[1]
    r = rows_per_step
    chunk_rows = n_steps * r
    assert rows % chunk_rows == 0
    row_group = min(r, SUBLANE)
    assert r % row_group == 0
    kern = functools.partial(_s5_kernel, n_steps=n_steps, rows_per_step=r, row_group=row_group)
    const = lambda shape: pl.BlockSpec(shape, lambda n, c: (0,) * len(shape))
    return pl.pallas_call(
        kern,
        grid=(nbatch, rows // chunk_rows),
        in_specs=[
            pl.BlockSpec((1, chunk_rows, w), lambda n, c: (n, c, 0)),
            pl.BlockSpec((1, r, s), lambda n, c: (n, 0, 0)),
            pl.BlockSpec((1, r, s), lambda n, c: (n, 0, 0)),
            const((1, s)), const((1, s)), const(wb.shape), const(wc.shape), const((1, w)),
        ],
        out_specs=[
            pl.BlockSpec((1, chunk_rows, w), lambda n, c: (n, c, 0)),
            pl.BlockSpec((1, r, s), lambda n, c: (n, 0, 0)),
            pl.BlockSpec((1, r, s), lambda n, c: (n, 0, 0)),
        ],
        out_shape=[jax.ShapeDtypeStruct((nbatch, rows, w), F32),
                   jax.ShapeDtypeStruct((nbatch, r, s), F32),
                   jax.ShapeDtypeStruct((nbatch, r, s), F32)],
        scratch_shapes=[pltpu.VMEM((chunk_rows, s), F32), pltpu.VMEM((chunk_rows, s), F32),
                        pltpu.VMEM((r, s), F32), pltpu.VMEM((r, s), F32)],
        compiler_params=_cparams("parallel", "arbitrary"),
        name="s5_scan",
    )(u, h0_re, h0_im, ab_re, ab_im, wb, wc, d)


def _key_to_f32(k):
    bits = jnp.where(k >= 0, k, k ^ jnp.int32(0x7FFFFFFF))
    return lax.bitcast_convert_type(bits, F32)


def _kth_threshold(count_ge, shape, topk):
    kf = float(topk)
    c_zero = count_ge(jnp.zeros(shape, F32))
    c_low = count_ge(jnp.full(shape, _LOWEST_F32, F32))
    pos = c_zero >= kf
    lo = jnp.where(pos, 0, KEY_LOWEST).astype(jnp.int32)
    hi = jnp.where(pos, KEY_POS_INF + 1, 0).astype(jnp.int32)
    cnt_lo = jnp.where(pos, c_zero, c_low)

    def body(_, carry):
        lo, hi, cnt_lo = carry
        mid = lo + ((hi - lo) >> 1)
        cnt = count_ge(_key_to_f32(mid))
        ok = cnt >= kf
        return jnp.where(ok, mid, lo), jnp.where(ok, hi, mid), jnp.where(ok, cnt, cnt_lo)

    lo, hi, cnt_lo = lax.fori_loop(0, 31, body, (lo, hi, cnt_lo))
    return _key_to_f32(lo), cnt_lo


def _tie_index_bound(count_eq_le, count_gt, cnt_ge, shape, topk, n_keys):
    need = float(topk) - count_gt
    tied = cnt_ge > float(topk)
    lo = jnp.full(shape, -1, jnp.int32)
    hi = jnp.full(shape, n_keys - 1, jnp.int32)

    def body(_, carry):
        lo, hi = carry
        mid = lo + ((hi - lo) >> 1)
        ok = count_eq_le(mid) >= need
        return jnp.where(ok, lo, mid), jnp.where(ok, mid, hi)

    n_iter = max(1, math.ceil(math.log2(n_keys + 1)))
    lo, hi = lax.fori_loop(0, n_iter, body, (lo, hi))
    return jnp.where(tied, hi, n_keys)


def _prompt_attn_kernel(qt_ref, qit_ref, w_ref, ki_ref, k_ref, vt_ref, o_ref,
                        s_ref, m_ref, l_ref, acc_ref, *, topk, q_block, chunk, n_heads):
    qb = pl.program_id(1)
    n_keys = k_ref.shape[1] * chunk
    n_chunks = (qb * q_block + q_block + chunk - 1) // chunk
    k_row = lax.broadcasted_iota(jnp.int32, (chunk, q_block), 0)
    q_pos = qb * q_block + lax.broadcasted_iota(jnp.int32, (chunk, q_block), 1)

    qit = qit_ref[0, 0].astype(BF16)
    w_row = w_ref[0, 0]

    def score_body(c, carry):
        st = _dot(ki_ref[0, c], qit)
        st = jnp.maximum(st, 0.0) * w_row
        sc = st[:, 0:q_block]
        for h in range(1, IDX_HEADS):
            sc = sc + st[:, h * q_block:(h + 1) * q_block]
        s_ref[c] = jnp.where(c * chunk + k_row <= q_pos, sc, NEG_INF)
        return carry

    lax.fori_loop(0, n_chunks, score_body, 0)

    row = (1, q_block)
    fold = min(chunk, 64)

    def count(pred):
        def body(c, acc):
            hits = jnp.where(pred(c, s_ref[c]), 1.0, 0.0)
            return acc + jnp.sum(hits.reshape(chunk // fold, fold, q_block), axis=0)
        acc = lax.fori_loop(0, n_chunks, body, jnp.zeros((fold, q_block), F32))
        return jnp.sum(acc, axis=0, keepdims=True)

    thr, cnt_ge = _kth_threshold(lambda t: count(lambda c, s: s >= t), row, topk)

    @pl.when(jnp.max(cnt_ge) > float(topk))
    def _():
        cnt_gt = count(lambda c, s: s > thr)
        bound = _tie_index_bound(
            lambda j: count(lambda c, s: (s == thr) & (c * chunk + k_row <= j)),
            cnt_gt, cnt_ge, row, topk, n_keys)

        def drop_body(c, carry):
            s = s_ref[c]
            s_ref[c] = jnp.where((s == thr) & (c * chunk + k_row > bound), NEG_INF, s)
            return carry

        lax.fori_loop(0, n_chunks, drop_body, 0)

    m_ref[...] = jnp.full(m_ref.shape, NEG_INF, F32)
    l_ref[...] = jnp.zeros(l_ref.shape, F32)
    acc_ref[...] = jnp.zeros(acc_ref.shape, F32)
    qt = qt_ref[0, 0] * (HEAD_DIM ** -0.5 * LOG2_E)
    head_of_dim = lax.broadcasted_iota(jnp.int32, (LANE, 2 * q_block), 0) // HEAD_DIM
    head_of_col = lax.broadcasted_iota(jnp.int32, (LANE, 2 * q_block), 1) // q_block
    q_pairs = []
    for j in range(n_heads // 2):
        qj = qt[j * LANE:(j + 1) * LANE, :]
        q_pairs.append(jnp.where(head_of_dim == head_of_col, jnp.concatenate([qj, qj], axis=1), 0.0).astype(BF16))

    def attn_body(c, carry):
        sel = s_ref[c] >= thr
        logits = [_dot(k_ref[0, c, :, j * LANE:(j + 1) * LANE], q_pairs[j]) for j in range(n_heads // 2)]
        for h in range(n_heads):
            j, hh = divmod(h, 2)
            x = jnp.where(sel, logits[j][:, hh * q_block:(hh + 1) * q_block], NEG_INF)
            m_old = m_ref[h]
            m_new = jnp.maximum(m_old, jnp.max(x, axis=0, keepdims=True))
            m_safe = jnp.where(m_new == NEG_INF, 0.0, m_new)
            p = jnp.exp2(x - m_safe)
            scale = jnp.exp2(m_old - m_safe)
            l_ref[h] = scale * l_ref[h] + jnp.sum(p, axis=0, keepdims=True)
            pv = _dot(vt_ref[0, c, h * HEAD_DIM:(h + 1) * HEAD_DIM, :], p.astype(BF16))
            acc_ref[h] = scale * acc_ref[h] + pv
            m_ref[h] = m_new
        return carry

    lax.fori_loop(0, n_chunks, attn_body, 0)

    out_t = jnp.concatenate([acc_ref[h] / l_ref[h] for h in range(n_heads)], axis=0)
    o_ref[0] = out_t.T


def _prompt_attention(q_t, qi_t, w_row, ki, k, v_t, topk, q_block, chunk):
    b, nqb, aw, _ = q_t.shape
    nch = k.shape[1]
    l = nch * chunk
    n_heads = aw // HEAD_DIM
    assert n_heads % 2 == 0 and nqb * q_block == l and chunk % q_block == 0 and 2 * HEAD_DIM == LANE
    kern = functools.partial(_prompt_attn_kernel, topk=topk, q_block=q_block, chunk=chunk, n_heads=n_heads)
    cols = IDX_HEADS * q_block
    return pl.pallas_call(
        kern,
        grid=(b, nqb),
        in_specs=[
            pl.BlockSpec((1, 1, aw, q_block), lambda i, j: (i, j, 0, 0)),
            pl.BlockSpec((1, 1, IDX_DIM, cols), lambda i, j: (i, j, 0, 0)),
            pl.BlockSpec((1, 1, 1, cols), lambda i, j: (i, j, 0, 0)),
            pl.BlockSpec((1, nch, chunk, IDX_DIM), lambda i, j: (i, 0, 0, 0)),
            pl.BlockSpec((1, nch, chunk, aw), lambda i, j: (i, 0, 0, 0)),
            pl.BlockSpec((1, nch, aw, chunk), lambda i, j: (i, 0, 0, 0)),
        ],
        out_specs=pl.BlockSpec((1, q_block, aw), lambda i, j: (i, j, 0)),
        out_shape=jax.ShapeDtypeStruct((b, l, aw), F32),
        scratch_shapes=[pltpu.VMEM((nch, chunk, q_block), F32),
                        pltpu.VMEM((n_heads, 1, q_block), F32),
                        pltpu.VMEM((n_heads, 1, q_block), F32),
                        pltpu.VMEM((n_heads, HEAD_DIM, q_block), F32)],
        compiler_params=_cparams("parallel", "arbitrary"),
        name="prompt_attention",
    )(q_t, qi_t, w_row, ki, k, v_t)


def _sample_select_kernel(pt_ref, qi_ref, w_ref, knew_ref, cache_ref, bias_ref, kbuf_ref, s_ref, sem,
                          *, topk, layer, group):
    step = pl.program_id(0)
    n_pages, page = kbuf_ref.shape[1] - 1, kbuf_ref.shape[2]
    n_q = knew_ref.shape[1]
    n_keys = (n_pages + 1) * page

    def page_copy(g, i):
        return pltpu.make_async_copy(cache_ref.at[layer, pt_ref[step * group + g, i]], kbuf_ref.at[g, i], sem.at[0])

    def start_body(g, carry):
        for i in range(n_pages):
            page_copy(g, i).start()
        return carry

    lax.fori_loop(0, group, start_body, 0)

    for g in range(group):
        kbuf_ref[g, n_pages] = jnp.zeros((page, IDX_DIM), F32)
        kbuf_ref[g, n_pages, 0:n_q, :] = knew_ref[g]

    def wait_body(g, carry):
        for i in range(n_pages):
            page_copy(g, i).wait()
        return carry

    lax.fori_loop(0, group, wait_body, 0)

    for g in range(group):
        keys = kbuf_ref[g].reshape(n_keys, IDX_DIM).astype(BF16)
        st = _dot_nt(qi_ref[g].astype(BF16), keys)
        st = jnp.maximum(st, 0.0) * w_ref[g]
        s_ref[g * n_q:(g + 1) * n_q, :] = jnp.sum(st.reshape(IDX_HEADS, n_q, n_keys), axis=0)

    rows = group * n_q
    q_of_row = lax.broadcasted_iota(jnp.int32, (rows, n_keys), 0) % n_q
    key_idx = lax.broadcasted_iota(jnp.int32, (rows, n_keys), 1)
    s_ref[...] = jnp.where(key_idx - n_pages * page <= q_of_row, s_ref[...], NEG_INF)

    col = (rows, 1)

    def count(pred):
        hits = jnp.where(pred(s_ref[...]), 1.0, 0.0)
        part = hits[:, 0:LANE]
        for t in range(1, n_keys // LANE):
            part = part + hits[:, t * LANE:(t + 1) * LANE]
        return jnp.sum(part, axis=1, keepdims=True)

    thr, cnt_ge = _kth_threshold(lambda t: count(lambda s: s >= t), col, topk)

    @pl.when(jnp.max(cnt_ge) > float(topk))
    def _():
        cnt_gt = count(lambda s: s > thr)
        bound = _tie_index_bound(lambda j: count(lambda s: (s == thr) & (key_idx <= j)),
                                 cnt_gt, cnt_ge, col, topk, n_keys)
        s = s_ref[...]
        s_ref[...] = jnp.where((s == thr) & (key_idx > bound), NEG_INF, s)

    bias_ref[...] = jnp.where(s_ref[...] >= thr, 0.0, NEG_INF).reshape(group, n_q, n_keys)


def _sample_select(page_table, qi_hm, w_col, cache_kidx, ki_new, layer, topk, group):
    db, n_pages = page_table.shape
    page = cache_kidx.shape[2]
    n_q = ki_new.shape[1]
    rows = IDX_HEADS * n_q
    n_keys = (n_pages + 1) * page
    assert db % group == 0 and n_q <= page
    kern = functools.partial(_sample_select_kernel, topk=topk, layer=layer, group=group)
    grid_spec = pltpu.PrefetchScalarGridSpec(
        num_scalar_prefetch=1,
        grid=(db // group,),
        in_specs=[
            pl.BlockSpec((group, rows, IDX_DIM), lambda s, pt: (s, 0, 0)),
            pl.BlockSpec((group, rows, 1), lambda s, pt: (s, 0, 0)),
            pl.BlockSpec((group, n_q, IDX_DIM), lambda s, pt: (s, 0, 0)),
            pl.BlockSpec(memory_space=pl.ANY),
        ],
        out_specs=pl.BlockSpec((group, n_q, n_keys), lambda s, pt: (s, 0, 0)),
        scratch_shapes=[pltpu.VMEM((group, n_pages + 1, page, IDX_DIM), F32),
                        pltpu.VMEM((group * n_q, n_keys), F32),
                        pltpu.SemaphoreType.DMA((1,))],
    )
    return pl.pallas_call(
        kern,
        grid_spec=grid_spec,
        out_shape=jax.ShapeDtypeStruct((db, n_q, n_keys), F32),
        compiler_params=_cparams("arbitrary"),
        name="sample_select",
    )(page_table, qi_hm, w_col, ki_new, cache_kidx)


def _sample_attn_kernel(pt_ref, q_ref, bias_ref, knew_ref, vnew_ref, kcache_ref, vcache_ref, o_ref,
                        kbuf_ref, vbuf_ref, lg_ref, kpad_ref, vpad_ref, sem, *, layer, n_heads):
    b = pl.program_id(0)
    n_seq = pl.num_programs(0)
    n_pages, page_rows = kbuf_ref.shape[1], kbuf_ref.shape[2]
    page = page_rows // n_heads
    n_q, aw = q_ref.shape[1], q_ref.shape[2]
    past = n_pages * page
    rows = n_heads * n_q
    slot = b % 2

    def page_copies(seq, i, to_slot):
        idx = pt_ref[seq, i]
        return (pltpu.make_async_copy(kcache_ref.at[layer, idx], kbuf_ref.at[to_slot, i], sem.at[to_slot, 0]),
                pltpu.make_async_copy(vcache_ref.at[layer, idx], vbuf_ref.at[to_slot, i], sem.at[to_slot, 1]))

    def start_fetch(seq, to_slot):
        for i in range(n_pages):
            for cp in page_copies(seq, i, to_slot):
                cp.start()

    @pl.when(b == 0)
    def _():
        start_fetch(0, 0)

    @pl.when(b + 1 < n_seq)
    def _():
        start_fetch(b + 1, 1 - slot)

    kpad_ref[...] = jnp.zeros(kpad_ref.shape, F32)
    vpad_ref[...] = jnp.zeros(vpad_ref.shape, F32)
    kpad_ref[0:n_q, :] = knew_ref[0]
    vpad_ref[0:n_q, :] = vnew_ref[0]

    q = q_ref[0] * (HEAD_DIM ** -0.5 * LOG2_E)
    head_of_row = lax.broadcasted_iota(jnp.int32, (rows, aw), 0) // n_q
    head_of_lane = lax.broadcasted_iota(jnp.int32, (rows, aw), 1) // HEAD_DIM
    own_head = head_of_row == head_of_lane
    q_diag = jnp.where(own_head, jnp.concatenate([q] * n_heads, axis=0), 0.0).astype(BF16)
    lg_ref[:, past:past + page] = _dot_nt(q_diag, kpad_ref[...].astype(BF16))

    for i in range(n_pages):
        for cp in page_copies(b, i, slot):
            cp.wait()

    k_slot = kbuf_ref.at[slot]
    v_slot = vbuf_ref.at[slot]
    for h in range(n_heads):
        k_h = k_slot[:, pl.ds(h, page, stride=n_heads), :].reshape(past, HEAD_DIM)
        q_h = q[:, h * HEAD_DIM:(h + 1) * HEAD_DIM].astype(BF16)
        lg_ref[h * n_q:(h + 1) * n_q, 0:past] = _dot_nt(q_h, k_h.astype(BF16))

    logits = lg_ref[...] + jnp.concatenate([bias_ref[0]] * n_heads, axis=0)
    m = jnp.max(logits, axis=1, keepdims=True)
    pr = jnp.exp2(logits - m)
    denom = jnp.sum(pr, axis=1, keepdims=True)
    pr = pr.astype(BF16)

    new_part = _dot(pr[:, past:past + page], vpad_ref[...].astype(BF16))
    new_part = jnp.sum(jnp.where(own_head, new_part, 0.0).reshape(n_heads, n_q, aw), axis=0)
    for h in range(n_heads):
        v_h = v_slot[:, pl.ds(h, page, stride=n_heads), :].reshape(past, HEAD_DIM)
        sl = slice(h * HEAD_DIM, (h + 1) * HEAD_DIM)
        acc = _dot(pr[h * n_q:(h + 1) * n_q, 0:past], v_h.astype(BF16)) + new_part[:, sl]
        o_ref[0, :, sl] = acc / denom[h * n_q:(h + 1) * n_q]


def _sample_attention(page_table, q, bias, cache_k, cache_v, k_new, v_new, layer):
    db, n_pages = page_table.shape
    _, n_q, aw = q.shape
    n_heads = aw // HEAD_DIM
    page_rows = cache_k.shape[2]
    page = page_rows // n_heads
    n_keys = (n_pages + 1) * page
    rows = n_heads * n_q
    kern = functools.partial(_sample_attn_kernel, layer=layer, n_heads=n_heads)
    per_b = lambda shape: pl.BlockSpec(shape, lambda b, pt: (b,) + (0,) * (len(shape) - 1))
    grid_spec = pltpu.PrefetchScalarGridSpec(
        num_scalar_prefetch=1,
        grid=(db,),
        in_specs=[per_b((1, n_q, aw)), per_b((1, n_q, n_keys)), per_b((1, n_q, aw)), per_b((1, n_q, aw)),
                  pl.BlockSpec(memory_space=pl.ANY), pl.BlockSpec(memory_space=pl.ANY)],
        out_specs=per_b((1, n_q, aw)),
        scratch_shapes=[pltpu.VMEM((2, n_pages, page_rows, HEAD_DIM), F32),
                        pltpu.VMEM((2, n_pages, page_rows, HEAD_DIM), F32),
                        pltpu.VMEM((rows, n_keys), F32),
                        pltpu.VMEM((page, aw), F32), pltpu.VMEM((page, aw), F32),
                        pltpu.SemaphoreType.DMA((2, 2))],
    )
    return pl.pallas_call(
        kern,
        grid_spec=grid_spec,
        out_shape=jax.ShapeDtypeStruct((db, n_q, aw), F32),
        compiler_params=_cparams("arbitrary"),
        name="sample_attention",
    )(page_table, q, bias, k_new, v_new, cache_k, cache_v)


def _mix_kernel(x_ref, ys_ref, ya_ref, wglu_ref, bglu_ref, wos_ref, woa_ref, g_ref, b_ref, o_ref, *, alpha):
    y = ys_ref[...]
    gel = 0.5 * y * (1.0 + jnp.tanh(math.sqrt(2.0 / math.pi) * (y + 0.044715 * (y * y * y))))
    gate = jax.nn.sigmoid(_dot(gel.astype(BF16), wglu_ref[...]) + bglu_ref[...])
    y_ssm = gel * gate
    mix = _dot(y_ssm.astype(BF16), wos_ref[...]) + _dot(ya_ref[...].astype(BF16), woa_ref[...])
    o_ref[...] = _layer_norm(alpha * x_ref[...] + mix, g_ref[...], b_ref[...])


def _mix(x, y_ssm, y_att, w_glu, b_glu, w_out_ssm, w_out_att, g, b, alpha, tm):
    m, d = x.shape
    sw = y_ssm.shape[1]
    aw = y_att.shape[1]
    const = lambda shape: pl.BlockSpec(shape, lambda i: (0, 0))
    return pl.pallas_call(
        functools.partial(_mix_kernel, alpha=alpha),
        grid=(m // tm,),
        in_specs=[pl.BlockSpec((tm, d), lambda i: (i, 0)),
                  pl.BlockSpec((tm, sw), lambda i: (i, 0)),
                  pl.BlockSpec((tm, aw), lambda i: (i, 0)),
                  const((sw, sw)), const((1, sw)), const((sw, d)), const((aw, d)), const((1, d)), const((1, d))],
        out_specs=pl.BlockSpec((tm, d), lambda i: (i, 0)),
        out_shape=jax.ShapeDtypeStruct((m, d), F32),
        compiler_params=_cparams("parallel"),
        name="glu_outproj_ln",
    )(x, y_ssm, y_att, w_glu, b_glu, w_out_ssm, w_out_att, g, b)


def _row_tile(m, cap=512):
    t = cap
    while m % t:
        t //= 2
    return t


def _ff_tile(d_ff, cap=512):
    best = LANE
    for t in range(LANE, cap + 1, LANE):
        if d_ff % t == 0:
            best = t
    return best


def _head_major(x, n_blocks, block, n_heads, dim):
    b = x.shape[0]
    x = x.reshape(b, n_blocks, block, n_heads, dim)
    return jnp.swapaxes(x, 2, 3).reshape(b, n_blocks, n_heads * block, dim)


def kernel(x_prompt, x_sample, cache_k, cache_v, cache_kidx, state_ssm_re, state_ssm_im, page_table,
           w_in, ssm_lambda_re, ssm_lambda_im, ssm_log_step, ssm_b_re, ssm_b_im, ssm_c_re, ssm_c_im,
           ssm_d, w_glu, b_glu, w_out, ffn1_up, ffn1_down, ffn2_up, ffn2_down, ln_g, ln_b):
    bsz, seq, d_model = x_prompt.shape
    db, dseq, _ = x_sample.shape
    depth = w_in.shape[0]
    sw = d_model // 2
    aw = d_model - sw
    n_groups = sw // SSM_GROUP
    n_state = n_groups * SSM_STATE
    n_heads = aw // HEAD_DIM
    n_pool, page = cache_k.shape[1], cache_k.shape[2]
    n_pages = page_table.shape[1]
    past = n_pages * page
    alpha = (2.0 * depth) ** 0.25
    mp = bsz * seq
    ms = db * dseq
    m_all = mp + ms
    tm = _row_tile(m_all)
    q_block = min(128, seq)
    chunk = min(512, seq)
    topk_p = min(TOPK_MAX, seq // 4)
    topk_s = min(TOPK_MAX, (past + dseq) // 4)
    s5_rows = min(256, seq)
    s_groups = max(1, ms // 256)
    s_per = db // s_groups

    cache_k4 = cache_k.reshape(depth, n_pool, page * n_heads, HEAD_DIM)
    cache_v4 = cache_v.reshape(depth, n_pool, page * n_heads, HEAD_DIM)
    select_group = math.gcd(db, SUBLANE)
    pad_w = LANE - IDX_DIM - IDX_HEADS
    widths = (sw, aw, aw, aw, IDX_HEADS * IDX_DIM, LANE)

    x = jnp.concatenate([x_prompt.reshape(mp, d_model), x_sample.reshape(ms, d_model)], axis=0)
    outs = {name: [] for name in ("kp", "vp", "kip", "hrp", "hip", "ks", "vs", "kis", "hrs", "his")}

    for l in range(depth):
        g_ln = ln_g[l][:, None, :]
        b_ln = ln_b[l][:, None, :]
        tf1 = _ff_tile(ffn1_down.shape[1])
        x = _ffn(x, ffn1_up[l].astype(BF16), ffn1_down[l].astype(BF16), g_ln[0], b_ln[0], alpha, tm, tf1)

        w_l = jnp.pad(w_in[l], ((0, 0), (0, pad_w))).astype(BF16)
        u, q, k, v, qi, kiwi = _inproj(x, w_l, widths, tm)
        ki = kiwi[:, :IDX_DIM]
        wi = kiwi[:, IDX_DIM:IDX_DIM + IDX_HEADS]

        q_s = q[mp:].reshape(db, dseq, aw)
        k_s = k[mp:].reshape(db, dseq, aw)
        v_s = v[mp:].reshape(db, dseq, aw)
        ki_s = ki[mp:].reshape(db, dseq, IDX_DIM)
        qi_s = _head_major(qi[mp:].reshape(db, dseq, IDX_HEADS * IDX_DIM), 1, dseq, IDX_HEADS, IDX_DIM)[:, 0]
        w_s = _head_major(wi[mp:].reshape(db, dseq, IDX_HEADS), 1, dseq, IDX_HEADS, 1)[:, 0]
        bias = _sample_select(page_table, qi_s, w_s, cache_kidx, ki_s, l, topk_s, select_group)
        y_att_s = _sample_attention(page_table, q_s, bias, cache_k4, cache_v4, k_s, v_s, l)

        ab_re, ab_im, bb_re_t, bb_im_t = _s5_params(ssm_lambda_re[l], ssm_lambda_im[l], ssm_log_step[l],
                                                     ssm_b_re[l], ssm_b_im[l])
        wb = _hi_lo_stack(_block_diag_in(bb_re_t), _block_diag_in(bb_im_t))
        wc = _hi_lo_stack(_block_diag_out(ssm_c_re[l]), _block_diag_out(ssm_c_im[l]))
        ab_re = ab_re.reshape(1, n_state)
        ab_im = ab_im.reshape(1, n_state)
        d_row = ssm_d[l].reshape(1, sw)

        zeros_state = jnp.zeros((bsz, 1, n_state), F32)
        y_ssm_p, hr_p, hi_p = _s5_scan(u[:mp].reshape(bsz, seq, sw), zeros_state, zeros_state,
                                       ab_re, ab_im, wb, wc, d_row, s5_rows, 1)
        u_s = u[mp:].reshape(s_groups, s_per, dseq, sw)
        u_s = jnp.swapaxes(u_s, 1, 2).reshape(s_groups, dseq * s_per, sw)
        y_ssm_s, hr_s, hi_s = _s5_scan(u_s, state_ssm_re[l].reshape(s_groups, s_per, n_state),
                                       state_ssm_im[l].reshape(s_groups, s_per, n_state),
                                       ab_re, ab_im, wb, wc, d_row, dseq, s_per)
        y_ssm_s = jnp.swapaxes(y_ssm_s.reshape(s_groups, dseq, s_per, sw), 1, 2).reshape(ms, sw)

        nqb = seq // q_block
        nch = seq // chunk
        q_t = jnp.swapaxes(q[:mp].reshape(bsz, nqb, q_block, aw), 2, 3)
        qi_t = jnp.transpose(qi[:mp].reshape(bsz, nqb, q_block, IDX_HEADS, IDX_DIM), (0, 1, 4, 3, 2))
        qi_t = qi_t.reshape(bsz, nqb, IDX_DIM, IDX_HEADS * q_block)
        w_t = jnp.swapaxes(wi[:mp].reshape(bsz, nqb, q_block, IDX_HEADS), 2, 3)
        w_t = w_t.reshape(bsz, nqb, 1, IDX_HEADS * q_block)
        ki_p = ki[:mp].reshape(bsz, nch, chunk, IDX_DIM).astype(BF16)
        k_p = k[:mp].reshape(bsz, nch, chunk, aw).astype(BF16)
        v_t = jnp.swapaxes(v[:mp].reshape(bsz, nch, chunk, aw), 2, 3).astype(BF16)
        y_att_p = _prompt_attention(q_t, qi_t, w_t, ki_p, k_p, v_t, topk_p, q_block, chunk)

        y_ssm = jnp.concatenate([y_ssm_p.reshape(mp, sw), y_ssm_s], axis=0)
        y_att = jnp.concatenate([y_att_p.reshape(mp, aw), y_att_s.reshape(ms, aw)], axis=0)
        w_o = w_out[l].astype(BF16)
        x = _mix(x, y_ssm, y_att, w_glu[l].astype(BF16), b_glu[l].reshape(1, sw), w_o[:sw], w_o[sw:],
                 g_ln[1], b_ln[1], alpha, tm)
        tf2 = _ff_tile(ffn2_down.shape[1])
        x = _ffn(x, ffn2_up[l].astype(BF16), ffn2_down[l].astype(BF16), g_ln[2], b_ln[2], alpha, tm, tf2)

        outs["kp"].append(k[:mp].reshape(bsz, seq, n_heads, HEAD_DIM))
        outs["vp"].append(v[:mp].reshape(bsz, seq, n_heads, HEAD_DIM))
        outs["kip"].append(ki[:mp].reshape(bsz, seq, IDX_DIM))
        outs["hrp"].append(hr_p.reshape(bsz, n_groups, SSM_STATE))
        outs["hip"].append(hi_p.reshape(bsz, n_groups, SSM_STATE))
        outs["ks"].append(k_s.reshape(db, dseq, n_heads, HEAD_DIM))
        outs["vs"].append(v_s.reshape(db, dseq, n_heads, HEAD_DIM))
        outs["kis"].append(ki_s)
        outs["hrs"].append(hr_s.reshape(db, n_groups, SSM_STATE))
        outs["his"].append(hi_s.reshape(db, n_groups, SSM_STATE))

    st = lambda name: jnp.stack(outs[name])
    return (x[:mp].reshape(bsz, seq, d_model), x[mp:].reshape(db, dseq, d_model),
            st("kp"), st("vp"), st("kip"), st("hrp"), st("hip"),
            st("ks"), st("vs"), st("kis"), st("hrs"), st("his"))
```

```python
import functools
import math

import jax
import jax.numpy as jnp
from jax import lax
from jax.experimental import pallas as pl
from jax.experimental.pallas import tpu as pltpu

F32 = jnp.float32
BF16 = jnp.bfloat16

SSM_GROUP = 16
SSM_STATE = 64
HEAD_DIM = 64
IDX_HEADS = 8
IDX_DIM = 32
TOPK_MAX = 256
LN_EPS = 1e-5

LANE = 128
SUBLANE = 8
VMEM_LIMIT_BYTES = 56 * 1024 * 1024

NEG_INF = float("-inf")
LOG2_E = 1.4426950408889634
_LOWEST_F32 = -3.4028234663852886e38
KEY_LOWEST = -2139095040
KEY_POS_INF = 0x7F800000


def _cparams(*sem):
    return pltpu.CompilerParams(dimension_semantics=sem, vmem_limit_bytes=VMEM_LIMIT_BYTES)


def _layer_norm(y, g, b):
    mu = jnp.mean(y, axis=-1, keepdims=True)
    yc = y - mu
    var = jnp.mean(yc * yc, axis=-1, keepdims=True)
    return yc * lax.rsqrt(var + LN_EPS) * g + b


def _split_bf16(x):
    hi = x.astype(BF16)
    lo = (x - hi.astype(F32)).astype(BF16)
    return hi, lo


def _dot(a, b):
    return jnp.dot(a, b, preferred_element_type=F32)


def _dot_nt(a, b):
    return lax.dot_general(a, b, (((1,), (1,)), ((), ())), preferred_element_type=F32)


def _dot3(a_hi, a_lo, b_hi, b_lo):
    return _dot(a_hi, b_hi) + _dot(a_lo, b_hi) + _dot(a_hi, b_lo)


def _ffn_kernel(x_ref, wg_ref, wu_ref, wd_ref, g_ref, b_ref, o_ref, xb_ref, acc_ref, *, alpha):
    j = pl.program_id(1)

    @pl.when(j == 0)
    def _():
        xb_ref[...] = x_ref[...].astype(BF16)
        acc_ref[...] = jnp.zeros_like(acc_ref)

    xb = xb_ref[...]
    gate = _dot(xb, wg_ref[...])
    up = _dot(xb, wu_ref[...])
    act = (gate * jax.nn.sigmoid(gate)) * up
    acc_ref[...] += _dot(act.astype(BF16), wd_ref[...])

    @pl.when(j == pl.num_programs(1) - 1)
    def _():
        y = alpha * x_ref[...] + 0.5 * acc_ref[...]
        o_ref[...] = _layer_norm(y, g_ref[...], b_ref[...])


def _ffn(x, w_up, w_down, g, b, alpha, tm, tf):
    m, d = x.shape
    d_ff = w_down.shape[0]
    nf = d_ff // tf
    assert m % tm == 0 and d_ff % tf == 0
    return pl.pallas_call(
        functools.partial(_ffn_kernel, alpha=alpha),
        grid=(m // tm, nf),
        in_specs=[
            pl.BlockSpec((tm, d), lambda i, j: (i, 0)),
            pl.BlockSpec((d, tf), lambda i, j: (0, j)),
            pl.BlockSpec((d, tf), lambda i, j: (0, j + nf)),
            pl.BlockSpec((tf, d), lambda i, j: (j, 0)),
            pl.BlockSpec((1, d), lambda i, j: (0, 0)),
            pl.BlockSpec((1, d), lambda i, j: (0, 0)),
        ],
        out_specs=pl.BlockSpec((tm, d), lambda i, j: (i, 0)),
        out_shape=jax.ShapeDtypeStruct((m, d), F32),
        scratch_shapes=[pltpu.VMEM((tm, d), BF16), pltpu.VMEM((tm, d), F32)],
        compiler_params=_cparams("parallel", "arbitrary"),
        name="swiglu_ln",
    )(x, w_up, w_up, w_down, g, b)


def _inproj_kernel(x_ref, w_ref, *o_refs):
    y = _dot(x_ref[...].astype(BF16), w_ref[...])
    off = 0
    for o_ref in o_refs:
        n = o_ref.shape[1]
        o_ref[...] = y[:, off:off + n]
        off += n


def _inproj(x, w, widths, tm):
    m, d = x.shape
    n = w.shape[1]
    assert sum(widths) == n and m % tm == 0
    return pl.pallas_call(
        _inproj_kernel,
        grid=(m // tm,),
        in_specs=[pl.BlockSpec((tm, d), lambda i: (i, 0)),
                  pl.BlockSpec((d, n), lambda i: (0, 0))],
        out_specs=[pl.BlockSpec((tm, wd), lambda i: (i, 0)) for wd in widths],
        out_shape=[jax.ShapeDtypeStruct((m, wd), F32) for wd in widths],
        compiler_params=_cparams("parallel"),
        name="in_proj",
    )(x, w)


def _s5_param_kernel(lre_ref, lim_ref, ls_ref, bre_ref, bim_ref, abre_ref, abim_ref, bbre_ref, bbim_ref):
    lam_re = lre_ref[...]
    lam_im = lim_ref[...]
    step = jnp.exp(ls_ref[...])
    mag = jnp.exp(lam_re * step)
    ang = lam_im * step
    ab_re = mag * jnp.cos(ang)
    ab_im = mag * jnp.sin(ang)
    nr = ab_re - 1.0
    ni = ab_im
    den = lam_re * lam_re + lam_im * lam_im
    f_re = (nr * lam_re + ni * lam_im) / den
    f_im = (ni * lam_re - nr * lam_im) / den
    abre_ref[...] = ab_re
    abim_ref[...] = ab_im
    b_re = bre_ref[...]
    b_im = bim_ref[...]
    bbre_ref[...] = f_re[:, None, :] * b_re - f_im[:, None, :] * b_im
    bbim_ref[...] = f_re[:, None, :] * b_im + f_im[:, None, :] * b_re


def _s5_params(lam_re, lam_im, log_step, b_re, b_im):
    g, p = lam_re.shape
    h = b_re.shape[2]
    b_re_t = jnp.swapaxes(b_re, 1, 2)
    b_im_t = jnp.swapaxes(b_im, 1, 2)
    return pl.pallas_call(
        _s5_param_kernel,
        out_shape=[jax.ShapeDtypeStruct((g, p), F32), jax.ShapeDtypeStruct((g, p), F32),
                   jax.ShapeDtypeStruct((g, h, p), F32), jax.ShapeDtypeStruct((g, h, p), F32)],
        name="s5_params",
    )(lam_re, lam_im, log_step.reshape(g, 1), b_re_t, b_im_t)


def _block_diag_in(bb_t):
    g, h, p = bb_t.shape
    per = LANE // h
    x = bb_t.reshape(g // per, per, h, p)
    eye = jnp.eye(per, dtype=F32)
    return jnp.einsum("jahp,ab->jahbp", x, eye).reshape(g // per, per * h, per * p)


def _block_diag_out(c):
    g, h, p = c.shape
    per = LANE // h
    x = c.reshape(g // per, per, h, p)
    eye = jnp.eye(per, dtype=F32)
    return jnp.einsum("jahp,ab->japbh", x, eye).reshape(g // per, per * p, per * h)


def _hi_lo_stack(w_re, w_im):
    w = jnp.stack([w_re, w_im])
    hi = w.astype(BF16)
    lo = (w - hi.astype(F32)).astype(BF16)
    return jnp.stack([hi, lo])


def _s5_kernel(u_ref, h0re_ref, h0im_ref, abre_ref, abim_ref, wb_ref, wc_ref, d_ref,
               y_ref, hre_out, him_out, hre_s, him_s, stre_s, stim_s, *, n_steps, rows_per_step, row_group):
    c = pl.program_id(1)
    nb = wb_ref.shape[2]
    sw = wb_ref.shape[4]

    @pl.when(c == 0)
    def _():
        stre_s[...] = h0re_ref[0]
        stim_s[...] = h0im_ref[0]

    u = u_ref[0]
    u_hi, u_lo = _split_bf16(u)
    for j in range(nb):
        uh = u_hi[:, j * LANE:(j + 1) * LANE]
        ul = u_lo[:, j * LANE:(j + 1) * LANE]
        hre_s[:, j * sw:(j + 1) * sw] = _dot3(uh, ul, wb_ref[0, 0, j], wb_ref[1, 0, j])
        him_s[:, j * sw:(j + 1) * sw] = _dot3(uh, ul, wb_ref[0, 1, j], wb_ref[1, 1, j])

    a_re = abre_ref[...]
    a_im = abim_ref[...]

    def group_body(rg, carry):
        r0 = pl.multiple_of(rg * row_group, row_group)

        def step_body(t, h):
            h_re, h_im = h
            row = t * rows_per_step + r0
            b_re = hre_s[pl.ds(row, row_group), :]
            b_im = him_s[pl.ds(row, row_group), :]
            n_re = a_re * h_re - a_im * h_im + b_re
            n_im = a_re * h_im + a_im * h_re + b_im
            hre_s[pl.ds(row, row_group), :] = n_re
            him_s[pl.ds(row, row_group), :] = n_im
            return n_re, n_im

        h0 = (stre_s[pl.ds(r0, row_group), :], stim_s[pl.ds(r0, row_group), :])
        h_re, h_im = lax.fori_loop(0, n_steps, step_body, h0)
        stre_s[pl.ds(r0, row_group), :] = h_re
        stim_s[pl.ds(r0, row_group), :] = h_im
        return carry

    lax.fori_loop(0, rows_per_step // row_group, group_body, 0)

    for j in range(nb):
        hr_hi, hr_lo = _split_bf16(hre_s[:, j * sw:(j + 1) * sw])
        hi_hi, hi_lo = _split_bf16(him_s[:, j * sw:(j + 1) * sw])
        yj = (_dot3(hr_hi, hr_lo, wc_ref[0, 0, j], wc_ref[1, 0, j])
              - _dot3(hi_hi, hi_lo, wc_ref[0, 1, j], wc_ref[1, 1, j]))
        sl = slice(j * LANE, (j + 1) * LANE)
        y_ref[0, :, sl] = yj + d_ref[:, sl] * u[:, sl]

    @pl.when(c == pl.num_programs(1) - 1)
    def _():
        hre_out[0] = stre_s[...]
        him_out[0] = stim_s[...]


def _s5_scan(u, h0_re, h0_im, ab_re, ab_im, wb, wc, d, n_steps, rows_per_step):
    nbatch, rows, w = u.shape
    s = ab_re.shape[1]
    r = rows_per_step
    chunk_rows = n_steps * r
    assert rows % chunk_rows == 0
    row_group = min(r, SUBLANE)
    assert r % row_group == 0
    kern = functools.partial(_s5_kernel, n_steps=n_steps, rows_per_step=r, row_group=row_group)
    const = lambda shape: pl.BlockSpec(shape, lambda n, c: (0,) * len(shape))
    return pl.pallas_call(
        kern,
        grid=(nbatch, rows // chunk_rows),
        in_specs=[
            pl.BlockSpec((1, chunk_rows, w), lambda n, c: (n, c, 0)),
            pl.BlockSpec((1, r, s), lambda n, c: (n, 0, 0)),
            pl.BlockSpec((1, r, s), lambda n, c: (n, 0, 0)),
            const((1, s)), const((1, s)), const(wb.shape), const(wc.shape), const((1, w)),
        ],
        out_specs=[
            pl.BlockSpec((1, chunk_rows, w), lambda n, c: (n, c, 0)),
            pl.BlockSpec((1, r, s), lambda n, c: (n, 0, 0)),
            pl.BlockSpec((1, r, s), lambda n, c: (n, 0, 0)),
        ],
        out_shape=[jax.ShapeDtypeStruct((nbatch, rows, w), F32),
                   jax.ShapeDtypeStruct((nbatch, r, s), F32),
                   jax.ShapeDtypeStruct((nbatch, r, s), F32)],
        scratch_shapes=[pltpu.VMEM((chunk_rows, s), F32), pltpu.VMEM((chunk_rows, s), F32),
                        pltpu.VMEM((r, s), F32), pltpu.VMEM((r, s), F32)],
        compiler_params=_cparams("parallel", "arbitrary"),
        name="s5_scan",
    )(u, h0_re, h0_im, ab_re, ab_im, wb, wc, d)


def _key_to_f32(k):
    bits = jnp.where(k >= 0, k, k ^ jnp.int32(0x7FFFFFFF))
    return lax.bitcast_convert_type(bits, F32)


def _kth_threshold(count_ge, shape, topk):
    kf = float(topk)
    c_zero = count_ge(jnp.zeros(shape, F32))
    c_low = count_ge(jnp.full(shape, _LOWEST_F32, F32))
    pos = c_zero >= kf
    lo = jnp.where(pos, 0, KEY_LOWEST).astype(jnp.int32)
    hi = jnp.where(pos, KEY_POS_INF + 1, 0).astype(jnp.int32)
    cnt_lo = jnp.where(pos, c_zero, c_low)

    def body(_, carry):
        lo, hi, cnt_lo = carry
        mid = lo + ((hi - lo) >> 1)
        cnt = count_ge(_key_to_f32(mid))
        ok = cnt >= kf
        return jnp.where(ok, mid, lo), jnp.where(ok, hi, mid), jnp.where(ok, cnt, cnt_lo)

    lo, hi, cnt_lo = lax.fori_loop(0, 31, body, (lo, hi, cnt_lo))
    return _key_to_f32(lo), cnt_lo


def _tie_index_bound(count_eq_le, count_gt, cnt_ge, shape, topk, n_keys):
    need = float(topk) - count_gt
    tied = cnt_ge > float(topk)
    lo = jnp.full(shape, -1, jnp.int32)
    hi = jnp.full(shape, n_keys - 1, jnp.int32)

    def body(_, carry):
        lo, hi = carry
        mid = lo + ((hi - lo) >> 1)
        ok = count_eq_le(mid) >= need
        return jnp.where(ok, lo, mid), jnp.where(ok, mid, hi)

    n_iter = max(1, math.ceil(math.log2(n_keys + 1)))
    lo, hi = lax.fori_loop(0, n_iter, body, (lo, hi))
    return jnp.where(tied, hi, n_keys)


def _prompt_attn_kernel(qt_ref, qit_ref, w_ref, ki_ref, k_ref, vt_ref, o_ref,
                        s_ref, m_ref, l_ref, acc_ref, *, topk, q_block, chunk, n_heads):
    qb = pl.program_id(1)
    n_keys = k_ref.shape[1] * chunk
    n_chunks = (qb * q_block + q_block + chunk - 1) // chunk
    k_row = lax.broadcasted_iota(jnp.int32, (chunk, q_block), 0)
    q_pos = qb * q_block + lax.broadcasted_iota(jnp.int32, (chunk, q_block), 1)

    qit = qit_ref[0, 0].astype(BF16)
    w_row = w_ref[0, 0]

    def score_body(c, carry):
        st = _dot(ki_ref[0, c], qit)
        st = jnp.maximum(st, 0.0) * w_row
        sc = st[:, 0:q_block]
        for h in range(1, IDX_HEADS):
            sc = sc + st[:, h * q_block:(h + 1) * q_block]
        s_ref[c] = jnp.where(c * chunk + k_row <= q_pos, sc, NEG_INF)
        return carry

    lax.fori_loop(0, n_chunks, score_body, 0)

    row = (1, q_block)
    fold = min(chunk, 64)

    def count(pred):
        def body(c, acc):
            hits = jnp.where(pred(c, s_ref[c]), 1.0, 0.0)
            return acc + jnp.sum(hits.reshape(chunk // fold, fold, q_block), axis=0)
        acc = lax.fori_loop(0, n_chunks, body, jnp.zeros((fold, q_block), F32))
        return jnp.sum(acc, axis=0, keepdims=True)

    thr, cnt_ge = _kth_threshold(lambda t: count(lambda c, s: s >= t), row, topk)

    @pl.when(jnp.max(cnt_ge) > float(topk))
    def _():
        cnt_gt = count(lambda c, s: s > thr)
        bound = _tie_index_bound(
            lambda j: count(lambda c, s: (s == thr) & (c * chunk + k_row <= j)),
            cnt_gt, cnt_ge, row, topk, n_keys)

        def drop_body(c, carry):
            s = s_ref[c]
            s_ref[c] = jnp.where((s == thr) & (c * chunk + k_row > bound), NEG_INF, s)
            return carry

        lax.fori_loop(0, n_chunks, drop_body, 0)

    m_ref[...] = jnp.full(m_ref.shape, NEG_INF, F32)
    l_ref[...] = jnp.zeros(l_ref.shape, F32)
    acc_ref[...] = jnp.zeros(acc_ref.shape, F32)
    qt = qt_ref[0, 0] * (HEAD_DIM ** -0.5 * LOG2_E)
    head_of_dim = lax.broadcasted_iota(jnp.int32, (LANE, 2 * q_block), 0) // HEAD_DIM
    head_of_col = lax.broadcasted_iota(jnp.int32, (LANE, 2 * q_block), 1) // q_block
    q_pairs = []
    for j in range(n_heads // 2):
        qj = qt[j * LANE:(j + 1) * LANE, :]
        q_pairs.append(jnp.where(head_of_dim == head_of_col, jnp.concatenate([qj, qj], axis=1), 0.0).astype(BF16))

    def attn_body(c, carry):
        sel = s_ref[c] >= thr
        logits = [_dot(k_ref[0, c, :, j * LANE:(j + 1) * LANE], q_pairs[j]) for j in range(n_heads // 2)]
        for h in range(n_heads):
            j, hh = divmod(h, 2)
            x = jnp.where(sel, logits[j][:, hh * q_block:(hh + 1) * q_block], NEG_INF)
            m_old = m_ref[h]
            m_new = jnp.maximum(m_old, jnp.max(x, axis=0, keepdims=True))
            m_safe = jnp.where(m_new == NEG_INF, 0.0, m_new)
            p = jnp.exp2(x - m_safe)
            scale = jnp.exp2(m_old - m_safe)
            l_ref[h] = scale * l_ref[h] + jnp.sum(p, axis=0, keepdims=True)
            pv = _dot(vt_ref[0, c, h * HEAD_DIM:(h + 1) * HEAD_DIM, :], p.astype(BF16))
            acc_ref[h] = scale * acc_ref[h] + pv
            m_ref[h] = m_new
        return carry

    lax.fori_loop(0, n_chunks, attn_body, 0)

    out_t = jnp.concatenate([acc_ref[h] / l_ref[h] for h in range(n_heads)], axis=0)
    o_ref[0] = out_t.T


def _prompt_attention(q_t, qi_t, w_row, ki, k, v_t, topk, q_block, chunk):
    b, nqb, aw, _ = q_t.shape
    nch = k.shape[1]
    l = nch * chunk
    n_heads = aw // HEAD_DIM
    assert n_heads % 2 == 0 and nqb * q_block == l and chunk % q_block == 0 and 2 * HEAD_DIM == LANE
    kern = functools.partial(_prompt_attn_kernel, topk=topk, q_block=q_block, chunk=chunk, n_heads=n_heads)
    cols = IDX_HEADS * q_block
    return pl.pallas_call(
        kern,
        grid=(b, nqb),
        in_specs=[
            pl.BlockSpec((1, 1, aw, q_block), lambda i, j: (i, j, 0, 0)),
            pl.BlockSpec((1, 1, IDX_DIM, cols), lambda i, j: (i, j, 0, 0)),
            pl.BlockSpec((1, 1, 1, cols), lambda i, j: (i, j, 0, 0)),
            pl.BlockSpec((1, nch, chunk, IDX_DIM), lambda i, j: (i, 0, 0, 0)),
            pl.BlockSpec((1, nch, chunk, aw), lambda i, j: (i, 0, 0, 0)),
            pl.BlockSpec((1, nch, aw, chunk), lambda i, j: (i, 0, 0, 0)),
        ],
        out_specs=pl.BlockSpec((1, q_block, aw), lambda i, j: (i, j, 0)),
        out_shape=jax.ShapeDtypeStruct((b, l, aw), F32),
        scratch_shapes=[pltpu.VMEM((nch, chunk, q_block), F32),
                        pltpu.VMEM((n_heads, 1, q_block), F32),
                        pltpu.VMEM((n_heads, 1, q_block), F32),
                        pltpu.VMEM((n_heads, HEAD_DIM, q_block), F32)],
        compiler_params=_cparams("parallel", "arbitrary"),
        name="prompt_attention",
    )(q_t, qi_t, w_row, ki, k, v_t)


def _sample_select_kernel(pt_ref, qi_ref, w_ref, knew_ref, cache_ref, bias_ref, kbuf_ref, s_ref, sem,
                          *, topk, layer, group):
    step = pl.program_id(0)
    n_pages, page = kbuf_ref.shape[1] - 1, kbuf_ref.shape[3]
    n_q = knew_ref.shape[2]
    n_keys = (n_pages + 1) * page

    def page_copy(g, i):
        return pltpu.make_async_copy(cache_ref.at[layer, pt_ref[step * group + g, i]], kbuf_ref.at[g, i], sem.at[0])

    def start_body(g, carry):
        for i in range(n_pages):
            page_copy(g, i).start()
        return carry

    lax.fori_loop(0, group, start_body, 0)

    for g in range(group):
        kbuf_ref[g, n_pages] = jnp.zeros((IDX_DIM, page), F32)
        kbuf_ref[g, n_pages, :, 0:n_q] = knew_ref[g]

    def wait_body(g, carry):
        for i in range(n_pages):
            page_copy(g, i).wait()
        return carry

    lax.fori_loop(0, group, wait_body, 0)

    for g in range(group):
        qi = qi_ref[g].astype(BF16)
        w_col = w_ref[g]
        for i in range(n_pages + 1):
            st = _dot(qi, kbuf_ref[g, i].astype(BF16))
            st = jnp.maximum(st, 0.0) * w_col
            s_ref[g * n_q:(g + 1) * n_q, i * page:(i + 1) * page] = jnp.sum(st.reshape(IDX_HEADS, n_q, page), axis=0)

    rows = group * n_q
    q_of_row = lax.broadcasted_iota(jnp.int32, (rows, n_keys), 0) % n_q
    key_idx = lax.broadcasted_iota(jnp.int32, (rows, n_keys), 1)
    s_ref[...] = jnp.where(key_idx - n_pages * page <= q_of_row, s_ref[...], NEG_INF)

    col = (rows, 1)

    def count(pred):
        hits = jnp.where(pred(s_ref[...]), 1.0, 0.0)
        part = hits[:, 0:LANE]
        for t in range(1, n_keys // LANE):
            part = part + hits[:, t * LANE:(t + 1) * LANE]
        return jnp.sum(part, axis=1, keepdims=True)

    thr, cnt_ge = _kth_threshold(lambda t: count(lambda s: s >= t), col, topk)

    @pl.when(jnp.max(cnt_ge) > float(topk))
    def _():
        cnt_gt = count(lambda s: s > thr)
        bound = _tie_index_bound(lambda j: count(lambda s: (s == thr) & (key_idx <= j)),
                                 cnt_gt, cnt_ge, col, topk, n_keys)
        s = s_ref[...]
        s_ref[...] = jnp.where((s == thr) & (key_idx > bound), NEG_INF, s)

    bias_ref[...] = jnp.where(s_ref[...] >= thr, 0.0, NEG_INF).reshape(group, n_q, n_keys)


def _sample_select(page_table, qi_hm, w_col, cache_kidx_t, ki_new_t, layer, topk, group):
    db, n_pages = page_table.shape
    page = cache_kidx_t.shape[3]
    n_q = ki_new_t.shape[2]
    rows = IDX_HEADS * n_q
    n_keys = (n_pages + 1) * page
    assert db % group == 0 and n_q <= page
    kern = functools.partial(_sample_select_kernel, topk=topk, layer=layer, group=group)
    grid_spec = pltpu.PrefetchScalarGridSpec(
        num_scalar_prefetch=1,
        grid=(db // group,),
        in_specs=[
            pl.BlockSpec((group, rows, IDX_DIM), lambda s, pt: (s, 0, 0)),
            pl.BlockSpec((group, rows, 1), lambda s, pt: (s, 0, 0)),
            pl.BlockSpec((group, IDX_DIM, n_q), lambda s, pt: (s, 0, 0)),
            pl.BlockSpec(memory_space=pl.ANY),
        ],
        out_specs=pl.BlockSpec((group, n_q, n_keys), lambda s, pt: (s, 0, 0)),
        scratch_shapes=[pltpu.VMEM((group, n_pages + 1, IDX_DIM, page), F32),
                        pltpu.VMEM((group * n_q, n_keys), F32),
                        pltpu.SemaphoreType.DMA((1,))],
    )
    return pl.pallas_call(
        kern,
        grid_spec=grid_spec,
        out_shape=jax.ShapeDtypeStruct((db, n_q, n_keys), F32),
        compiler_params=_cparams("arbitrary"),
        name="sample_select",
    )(page_table, qi_hm, w_col, ki_new_t, cache_kidx_t)


def _sample_attn_kernel(pt_ref, q_ref, bias_ref, knew_ref, vnew_ref, kcache_ref, vcache_ref, o_ref,
                        kbuf_ref, vbuf_ref, lg_ref, kpad_ref, vpad_ref, sem, *, layer, n_heads):
    b = pl.program_id(0)
    n_seq = pl.num_programs(0)
    n_pages, page = kbuf_ref.shape[1], kbuf_ref.shape[4]
    n_q, aw = q_ref.shape[1], q_ref.shape[2]
    past = n_pages * page
    rows = n_heads * n_q
    slot = b % 2

    def page_copies(seq, i, to_slot):
        idx = pt_ref[seq, i]
        return (pltpu.make_async_copy(kcache_ref.at[layer, idx], kbuf_ref.at[to_slot, i], sem.at[to_slot, 0]),
                pltpu.make_async_copy(vcache_ref.at[layer, idx], vbuf_ref.at[to_slot, i], sem.at[to_slot, 1]))

    def start_fetch(seq, to_slot):
        for i in range(n_pages):
            for cp in page_copies(seq, i, to_slot):
                cp.start()

    @pl.when(b == 0)
    def _():
        start_fetch(0, 0)

    @pl.when(b + 1 < n_seq)
    def _():
        start_fetch(b + 1, 1 - slot)

    kpad_ref[...] = jnp.zeros(kpad_ref.shape, F32)
    vpad_ref[...] = jnp.zeros(vpad_ref.shape, F32)
    kpad_ref[0:n_q, :] = knew_ref[0]
    vpad_ref[0:n_q, :] = vnew_ref[0]

    q = q_ref[0] * (HEAD_DIM ** -0.5 * LOG2_E)
    head_of_row = lax.broadcasted_iota(jnp.int32, (rows, aw), 0) // n_q
    head_of_lane = lax.broadcasted_iota(jnp.int32, (rows, aw), 1) // HEAD_DIM
    own_head = head_of_row == head_of_lane
    q_diag = jnp.where(own_head, jnp.concatenate([q] * n_heads, axis=0), 0.0).astype(BF16)
    lg_ref[:, past:past + page] = _dot_nt(q_diag, kpad_ref[...].astype(BF16))

    for i in range(n_pages):
        for cp in page_copies(b, i, slot):
            cp.wait()

    for i in range(n_pages):
        k_t = kbuf_ref[slot, i].reshape(aw, page).astype(BF16)
        lg_ref[:, i * page:(i + 1) * page] = _dot(q_diag, k_t)

    logits = lg_ref[...] + jnp.concatenate([bias_ref[0]] * n_heads, axis=0)
    m = jnp.max(logits, axis=1, keepdims=True)
    pr = jnp.exp2(logits - m)
    denom = jnp.sum(pr, axis=1, keepdims=True)
    pr = pr.astype(BF16)

    acc = _dot(pr[:, past:past + page], vpad_ref[...].astype(BF16))
    for i in range(n_pages):
        v_t = vbuf_ref[slot, i].reshape(aw, page).astype(BF16)
        acc = acc + _dot_nt(pr[:, i * page:(i + 1) * page], v_t)
    out = jnp.where(own_head, acc / denom, 0.0)
    o_ref[0] = jnp.sum(out.reshape(n_heads, n_q, aw), axis=0)


def _sample_attention(page_table, q, bias, cache_k_t, cache_v_t, k_new, v_new, layer):
    db, n_pages = page_table.shape
    _, n_q, aw = q.shape
    n_heads = aw // HEAD_DIM
    page = cache_k_t.shape[4]
    n_keys = (n_pages + 1) * page
    rows = n_heads * n_q
    kern = functools.partial(_sample_attn_kernel, layer=layer, n_heads=n_heads)
    per_b = lambda shape: pl.BlockSpec(shape, lambda b, pt: (b,) + (0,) * (len(shape) - 1))
    grid_spec = pltpu.PrefetchScalarGridSpec(
        num_scalar_prefetch=1,
        grid=(db,),
        in_specs=[per_b((1, n_q, aw)), per_b((1, n_q, n_keys)), per_b((1, n_q, aw)), per_b((1, n_q, aw)),
                  pl.BlockSpec(memory_space=pl.ANY), pl.BlockSpec(memory_space=pl.ANY)],
        out_specs=per_b((1, n_q, aw)),
        scratch_shapes=[pltpu.VMEM((2, n_pages, n_heads, HEAD_DIM, page), F32),
                        pltpu.VMEM((2, n_pages, n_heads, HEAD_DIM, page), F32),
                        pltpu.VMEM((rows, n_keys), F32),
                        pltpu.VMEM((page, aw), F32), pltpu.VMEM((page, aw), F32),
                        pltpu.SemaphoreType.DMA((2, 2))],
    )
    return pl.pallas_call(
        kern,
        grid_spec=grid_spec,
        out_shape=jax.ShapeDtypeStruct((db, n_q, aw), F32),
        compiler_params=_cparams("arbitrary"),
        name="sample_attention",
    )(page_table, q, bias, k_new, v_new, cache_k_t, cache_v_t)


def _mix_kernel(x_ref, ys_ref, ya_ref, wglu_ref, bglu_ref, wos_ref, woa_ref, g_ref, b_ref, o_ref, *, alpha):
    y = ys_ref[...]
    gel = 0.5 * y * (1.0 + jnp.tanh(math.sqrt(2.0 / math.pi) * (y + 0.044715 * (y * y * y))))
    gate = jax.nn.sigmoid(_dot(gel.astype(BF16), wglu_ref[...]) + bglu_ref[...])
    y_ssm = gel * gate
    mix = _dot(y_ssm.astype(BF16), wos_ref[...]) + _dot(ya_ref[...].astype(BF16), woa_ref[...])
    o_ref[...] = _layer_norm(alpha * x_ref[...] + mix, g_ref[...], b_ref[...])


def _mix(x, y_ssm, y_att, w_glu, b_glu, w_out_ssm, w_out_att, g, b, alpha, tm):
    m, d = x.shape
    sw = y_ssm.shape[1]
    aw = y_att.shape[1]
    const = lambda shape: pl.BlockSpec(shape, lambda i: (0, 0))
    return pl.pallas_call(
        functools.partial(_mix_kernel, alpha=alpha),
        grid=(m // tm,),
        in_specs=[pl.BlockSpec((tm, d), lambda i: (i, 0)),
                  pl.BlockSpec((tm, sw), lambda i: (i, 0)),
                  pl.BlockSpec((tm, aw), lambda i: (i, 0)),
                  const((sw, sw)), const((1, sw)), const((sw, d)), const((aw, d)), const((1, d)), const((1, d))],
        out_specs=pl.BlockSpec((tm, d), lambda i: (i, 0)),
        out_shape=jax.ShapeDtypeStruct((m, d), F32),
        compiler_params=_cparams("parallel"),
        name="glu_outproj_ln",
    )(x, y_ssm, y_att, w_glu, b_glu, w_out_ssm, w_out_att, g, b)


def _row_tile(m, cap=512):
    t = cap
    while m % t:
        t //= 2
    return t


def _ff_tile(d_ff, cap=512):
    best = LANE
    for t in range(LANE, cap + 1, LANE):
        if d_ff % t == 0:
            best = t
    return best


def _head_major(x, n_blocks, block, n_heads, dim):
    b = x.shape[0]
    x = x.reshape(b, n_blocks, block, n_heads, dim)
    return jnp.swapaxes(x, 2, 3).reshape(b, n_blocks, n_heads * block, dim)


def kernel(x_prompt, x_sample, cache_k, cache_v, cache_kidx, state_ssm_re, state_ssm_im, page_table,
           w_in, ssm_lambda_re, ssm_lambda_im, ssm_log_step, ssm_b_re, ssm_b_im, ssm_c_re, ssm_c_im,
           ssm_d, w_glu, b_glu, w_out, ffn1_up, ffn1_down, ffn2_up, ffn2_down, ln_g, ln_b):
    bsz, seq, d_model = x_prompt.shape
    db, dseq, _ = x_sample.shape
    depth = w_in.shape[0]
    sw = d_model // 2
    aw = d_model - sw
    n_groups = sw // SSM_GROUP
    n_state = n_groups * SSM_STATE
    n_heads = aw // HEAD_DIM
    page = cache_k.shape[2]
    n_pages = page_table.shape[1]
    past = n_pages * page
    alpha = (2.0 * depth) ** 0.25
    mp = bsz * seq
    ms = db * dseq
    m_all = mp + ms
    tm = _row_tile(m_all)
    q_block = min(128, seq)
    chunk = min(512, seq)
    topk_p = min(TOPK_MAX, seq // 4)
    topk_s = min(TOPK_MAX, (past + dseq) // 4)
    s5_rows = min(256, seq)
    s_groups = max(1, ms // 256)
    s_per = db // s_groups

    select_group = math.gcd(db, SUBLANE)
    cache_k_t = jnp.transpose(cache_k, (0, 1, 3, 4, 2))
    cache_v_t = jnp.transpose(cache_v, (0, 1, 3, 4, 2))
    cache_kidx_t = jnp.swapaxes(cache_kidx, 2, 3)
    pad_w = LANE - IDX_DIM - IDX_HEADS
    widths = (sw, aw, aw, aw, IDX_HEADS * IDX_DIM, LANE)

    x = jnp.concatenate([x_prompt.reshape(mp, d_model), x_sample.reshape(ms, d_model)], axis=0)
    outs = {name: [] for name in ("kp", "vp", "kip", "hrp", "hip", "ks", "vs", "kis", "hrs", "his")}

    for l in range(depth):
        g_ln = ln_g[l][:, None, :]
        b_ln = ln_b[l][:, None, :]
        tf1 = _ff_tile(ffn1_down.shape[1])
        x = _ffn(x, ffn1_up[l].astype(BF16), ffn1_down[l].astype(BF16), g_ln[0], b_ln[0], alpha, tm, tf1)

        w_l = jnp.pad(w_in[l], ((0, 0), (0, pad_w))).astype(BF16)
        u, q, k, v, qi, kiwi = _inproj(x, w_l, widths, tm)
        ki = kiwi[:, :IDX_DIM]
        wi = kiwi[:, IDX_DIM:IDX_DIM + IDX_HEADS]

        q_s = q[mp:].reshape(db, dseq, aw)
        k_s = k[mp:].reshape(db, dseq, aw)
        v_s = v[mp:].reshape(db, dseq, aw)
        ki_s = ki[mp:].reshape(db, dseq, IDX_DIM)
        qi_s = _head_major(qi[mp:].reshape(db, dseq, IDX_HEADS * IDX_DIM), 1, dseq, IDX_HEADS, IDX_DIM)[:, 0]
        w_s = _head_major(wi[mp:].reshape(db, dseq, IDX_HEADS), 1, dseq, IDX_HEADS, 1)[:, 0]
        bias = _sample_select(page_table, qi_s, w_s, cache_kidx_t, jnp.swapaxes(ki_s, 1, 2), l, topk_s, select_group)
        y_att_s = _sample_attention(page_table, q_s, bias, cache_k_t, cache_v_t, k_s, v_s, l)

        ab_re, ab_im, bb_re_t, bb_im_t = _s5_params(ssm_lambda_re[l], ssm_lambda_im[l], ssm_log_step[l],
                                                     ssm_b_re[l], ssm_b_im[l])
        wb = _hi_lo_stack(_block_diag_in(bb_re_t), _block_diag_in(bb_im_t))
        wc = _hi_lo_stack(_block_diag_out(ssm_c_re[l]), _block_diag_out(ssm_c_im[l]))
        ab_re = ab_re.reshape(1, n_state)
        ab_im = ab_im.reshape(1, n_state)
        d_row = ssm_d[l].reshape(1, sw)

        zeros_state = jnp.zeros((bsz, 1, n_state), F32)
        y_ssm_p, hr_p, hi_p = _s5_scan(u[:mp].reshape(bsz, seq, sw), zeros_state, zeros_state,
                                       ab_re, ab_im, wb, wc, d_row, s5_rows, 1)
        u_s = u[mp:].reshape(s_groups, s_per, dseq, sw)
        u_s = jnp.swapaxes(u_s, 1, 2).reshape(s_groups, dseq * s_per, sw)
        y_ssm_s, hr_s, hi_s = _s5_scan(u_s, state_ssm_re[l].reshape(s_groups, s_per, n_state),
                                       state_ssm_im[l].reshape(s_groups, s_per, n_state),
                                       ab_re, ab_im, wb, wc, d_row, dseq, s_per)
        y_ssm_s = jnp.swapaxes(y_ssm_s.reshape(s_groups, dseq, s_per, sw), 1, 2).reshape(ms, sw)

        nqb = seq // q_block
        nch = seq // chunk
        q_t = jnp.swapaxes(q[:mp].reshape(bsz, nqb, q_block, aw), 2, 3)
        qi_t = jnp.transpose(qi[:mp].reshape(bsz, nqb, q_block, IDX_HEADS, IDX_DIM), (0, 1, 4, 3, 2))
        qi_t = qi_t.reshape(bsz, nqb, IDX_DIM, IDX_HEADS * q_block)
        w_t = jnp.swapaxes(wi[:mp].reshape(bsz, nqb, q_block, IDX_HEADS), 2, 3)
        w_t = w_t.reshape(bsz, nqb, 1, IDX_HEADS * q_block)
        ki_p = ki[:mp].reshape(bsz, nch, chunk, IDX_DIM).astype(BF16)
        k_p = k[:mp].reshape(bsz, nch, chunk, aw).astype(BF16)
        v_t = jnp.swapaxes(v[:mp].reshape(bsz, nch, chunk, aw), 2, 3).astype(BF16)
        y_att_p = _prompt_attention(q_t, qi_t, w_t, ki_p, k_p, v_t, topk_p, q_block, chunk)

        y_ssm = jnp.concatenate([y_ssm_p.reshape(mp, sw), y_ssm_s], axis=0)
        y_att = jnp.concatenate([y_att_p.reshape(mp, aw), y_att_s.reshape(ms, aw)], axis=0)
        w_o = w_out[l].astype(BF16)
        x = _mix(x, y_ssm, y_att, w_glu[l].astype(BF16), b_glu[l].reshape(1, sw), w_o[:sw], w_o[sw:],
                 g_ln[1], b_ln[1], alpha, tm)
        tf2 = _ff_tile(ffn2_down.shape[1])
        x = _ffn(x, ffn2_up[l].astype(BF16), ffn2_down[l].astype(BF16), g_ln[2], b_ln[2], alpha, tm, tf2)

        outs["kp"].append(k[:mp].reshape(bsz, seq, n_heads, HEAD_DIM))
        outs["vp"].append(v[:mp].reshape(bsz, seq, n_heads, HEAD_DIM))
        outs["kip"].append(ki[:mp].reshape(bsz, seq, IDX_DIM))
        outs["hrp"].append(hr_p.reshape(bsz, n_groups, SSM_STATE))
        outs["hip"].append(hi_p.reshape(bsz, n_groups, SSM_STATE))
        outs["ks"].append(k_s.reshape(db, dseq, n_heads, HEAD_DIM))
        outs["vs"].append(v_s.reshape(db, dseq, n_heads, HEAD_DIM))
        outs["kis"].append(ki_s)
        outs["hrs"].append(hr_s.reshape(db, n_groups, SSM_STATE))
        outs["his"].append(hi_s.reshape(db, n_groups, SSM_STATE))

    st = lambda name: jnp.stack(outs[name])
    return (x[:mp].reshape(bsz, seq, d_model), x[mp:].reshape(db, dseq, d_model),
            st("kp"), st("vp"), st("kip"), st("hrp"), st("hip"),
            st("ks"), st("vs"), st("kis"), st("hrs"), st("his"))
```

```python
import functools
import math

import jax
import jax.numpy as jnp
from jax import lax
from jax.experimental import pallas as pl
from jax.experimental.pallas import tpu as pltpu

F32 = jnp.float32
BF16 = jnp.bfloat16

SSM_GROUP = 16
SSM_STATE = 64
HEAD_DIM = 64
IDX_HEADS = 8
IDX_DIM = 32
TOPK_MAX = 256
LN_EPS = 1e-5

LANE = 128
SUBLANE = 8
VMEM_LIMIT_BYTES = 56 * 1024 * 1024

NEG_INF = float("-inf")
LOG2_E = 1.4426950408889634
_LOWEST_F32 = -3.4028234663852886e38
KEY_LOWEST = -2139095040
KEY_POS_INF = 0x7F800000


def _cparams(*sem):
    return pltpu.CompilerParams(dimension_semantics=sem, vmem_limit_bytes=VMEM_LIMIT_BYTES)


def _layer_norm(y, g, b):
    mu = jnp.mean(y, axis=-1, keepdims=True)
    yc = y - mu
    var = jnp.mean(yc * yc, axis=-1, keepdims=True)
    return yc * lax.rsqrt(var + LN_EPS) * g + b


def _dot(a, b):
    return jnp.dot(a, b, preferred_element_type=F32)


def _dot_nt(a, b):
    return lax.dot_general(a, b, (((1,), (1,)), ((), ())), preferred_element_type=F32)


def _ffn_kernel(x_ref, wg_ref, wu_ref, wd_ref, g_ref, b_ref, o_ref, xb_ref, acc_ref, *, alpha):
    j = pl.program_id(1)

    @pl.when(j == 0)
    def _():
        xb_ref[...] = x_ref[...].astype(BF16)
        acc_ref[...] = jnp.zeros_like(acc_ref)

    xb = xb_ref[...]
    gate = _dot(xb, wg_ref[...])
    up = _dot(xb, wu_ref[...])
    act = (gate * jax.nn.sigmoid(gate)) * up
    acc_ref[...] += _dot(act.astype(BF16), wd_ref[...])

    @pl.when(j == pl.num_programs(1) - 1)
    def _():
        y = alpha * x_ref[...] + 0.5 * acc_ref[...]
        o_ref[...] = _layer_norm(y, g_ref[...], b_ref[...])


def _ffn(x, w_up, w_down, g, b, alpha, tm, tf):
    m, d = x.shape
    d_ff = w_down.shape[0]
    nf = d_ff // tf
    assert m % tm == 0 and d_ff % tf == 0
    return pl.pallas_call(
        functools.partial(_ffn_kernel, alpha=alpha),
        grid=(m // tm, nf),
        in_specs=[
            pl.BlockSpec((tm, d), lambda i, j: (i, 0)),
            pl.BlockSpec((d, tf), lambda i, j: (0, j)),
            pl.BlockSpec((d, tf), lambda i, j: (0, j + nf)),
            pl.BlockSpec((tf, d), lambda i, j: (j, 0)),
            pl.BlockSpec((1, d), lambda i, j: (0, 0)),
            pl.BlockSpec((1, d), lambda i, j: (0, 0)),
        ],
        out_specs=pl.BlockSpec((tm, d), lambda i, j: (i, 0)),
        out_shape=jax.ShapeDtypeStruct((m, d), F32),
        scratch_shapes=[pltpu.VMEM((tm, d), BF16), pltpu.VMEM((tm, d), F32)],
        compiler_params=_cparams("parallel", "arbitrary"),
        name="swiglu_ln",
    )(x, w_up, w_up, w_down, g, b)


def _inproj_kernel(x_ref, w_ref, *o_refs):
    y = _dot(x_ref[...].astype(BF16), w_ref[...])
    off = 0
    for o_ref in o_refs:
        n = o_ref.shape[1]
        o_ref[...] = y[:, off:off + n]
        off += n


def _inproj(x, w, widths, tm):
    m, d = x.shape
    n = w.shape[1]
    assert sum(widths) == n and m % tm == 0
    return pl.pallas_call(
        _inproj_kernel,
        grid=(m // tm,),
        in_specs=[pl.BlockSpec((tm, d), lambda i: (i, 0)),
                  pl.BlockSpec((d, n), lambda i: (0, 0))],
        out_specs=[pl.BlockSpec((tm, wd), lambda i: (i, 0)) for wd in widths],
        out_shape=[jax.ShapeDtypeStruct((m, wd), F32) for wd in widths],
        compiler_params=_cparams("parallel"),
        name="in_proj",
    )(x, w)


def _s5_param_kernel(lre_ref, lim_ref, ls_ref, bre_ref, bim_ref, abre_ref, abim_ref, bbre_ref, bbim_ref):
    lam_re = lre_ref[...]
    lam_im = lim_ref[...]
    step = jnp.exp(ls_ref[...])
    mag = jnp.exp(lam_re * step)
    ang = lam_im * step
    ab_re = mag * jnp.cos(ang)
    ab_im = mag * jnp.sin(ang)
    nr = ab_re - 1.0
    ni = ab_im
    den = lam_re * lam_re + lam_im * lam_im
    f_re = (nr * lam_re + ni * lam_im) / den
    f_im = (ni * lam_re - nr * lam_im) / den
    abre_ref[...] = ab_re
    abim_ref[...] = ab_im
    b_re = bre_ref[...]
    b_im = bim_ref[...]
    bbre_ref[...] = f_re[:, None, :] * b_re - f_im[:, None, :] * b_im
    bbim_ref[...] = f_re[:, None, :] * b_im + f_im[:, None, :] * b_re


def _s5_params(lam_re, lam_im, log_step, b_re, b_im):
    g, p = lam_re.shape
    h = b_re.shape[2]
    b_re_t = jnp.swapaxes(b_re, 1, 2)
    b_im_t = jnp.swapaxes(b_im, 1, 2)
    return pl.pallas_call(
        _s5_param_kernel,
        out_shape=[jax.ShapeDtypeStruct((g, p), F32), jax.ShapeDtypeStruct((g, p), F32),
                   jax.ShapeDtypeStruct((g, h, p), F32), jax.ShapeDtypeStruct((g, h, p), F32)],
        name="s5_params",
    )(lam_re, lam_im, log_step.reshape(g, 1), b_re_t, b_im_t)


def _block_diag_in(bb_t):
    g, h, p = bb_t.shape
    per = LANE // h
    x = bb_t.reshape(g // per, per, h, p)
    eye = jnp.eye(per, dtype=F32)
    return jnp.einsum("jahp,ab->jahbp", x, eye).reshape(g // per, per * h, per * p)


def _block_diag_out(c):
    g, h, p = c.shape
    per = LANE // h
    x = c.reshape(g // per, per, h, p)
    eye = jnp.eye(per, dtype=F32)
    return jnp.einsum("jahp,ab->japbh", x, eye).reshape(g // per, per * p, per * h)


def _re_im_stack(w_re, w_im):
    return jnp.stack([w_re, w_im]).astype(BF16)


def _s5_kernel(u_ref, h0re_ref, h0im_ref, abre_ref, abim_ref, wb_ref, wc_ref, d_ref,
               y_ref, hre_out, him_out, hre_s, him_s, stre_s, stim_s, *, n_steps, rows_per_step, row_group):
    c = pl.program_id(1)
    nb = wb_ref.shape[1]
    sw = wb_ref.shape[3]

    @pl.when(c == 0)
    def _():
        stre_s[...] = h0re_ref[0]
        stim_s[...] = h0im_ref[0]

    u = u_ref[0]
    u_b = u.astype(BF16)
    for j in range(nb):
        uj = u_b[:, j * LANE:(j + 1) * LANE]
        hre_s[:, j * sw:(j + 1) * sw] = _dot(uj, wb_ref[0, j])
        him_s[:, j * sw:(j + 1) * sw] = _dot(uj, wb_ref[1, j])

    a_re = abre_ref[...]
    a_im = abim_ref[...]

    def group_body(rg, carry):
        r0 = pl.multiple_of(rg * row_group, row_group)

        def step_body(t, h):
            h_re, h_im = h
            row = t * rows_per_step + r0
            b_re = hre_s[pl.ds(row, row_group), :]
            b_im = him_s[pl.ds(row, row_group), :]
            n_re = a_re * h_re - a_im * h_im + b_re
            n_im = a_re * h_im + a_im * h_re + b_im
            hre_s[pl.ds(row, row_group), :] = n_re
            him_s[pl.ds(row, row_group), :] = n_im
            return n_re, n_im

        h0 = (stre_s[pl.ds(r0, row_group), :], stim_s[pl.ds(r0, row_group), :])
        h_re, h_im = lax.fori_loop(0, n_steps, step_body, h0)
        stre_s[pl.ds(r0, row_group), :] = h_re
        stim_s[pl.ds(r0, row_group), :] = h_im
        return carry

    lax.fori_loop(0, rows_per_step // row_group, group_body, 0)

    for j in range(nb):
        yj = (_dot(hre_s[:, j * sw:(j + 1) * sw].astype(BF16), wc_ref[0, j])
              - _dot(him_s[:, j * sw:(j + 1) * sw].astype(BF16), wc_ref[1, j]))
        sl = slice(j * LANE, (j + 1) * LANE)
        y_ref[0, :, sl] = yj + d_ref[:, sl] * u[:, sl]

    @pl.when(c == pl.num_programs(1) - 1)
    def _():
        hre_out[0] = stre_s[...]
        him_out[0] = stim_s[...]


def _s5_scan(u, h0_re, h0_im, ab_re, ab_im, wb, wc, d, n_steps, rows_per_step):
    nbatch, rows, w = u.shape
    s = ab_re.shape[1]
    r = rows_per_step
    chunk_rows = n_steps * r
    assert rows % chunk_rows == 0
    row_group = min(r, SUBLANE)
    assert r % row_group == 0
    kern = functools.partial(_s5_kernel, n_steps=n_steps, rows_per_step=r, row_group=row_group)
    const = lambda shape: pl.BlockSpec(shape, lambda n, c: (0,) * len(shape))
    return pl.pallas_call(
        kern,
        grid=(nbatch, rows // chunk_rows),
        in_specs=[
            pl.BlockSpec((1, chunk_rows, w), lambda n, c: (n, c, 0)),
            pl.BlockSpec((1, r, s), lambda n, c: (n, 0, 0)),
            pl.BlockSpec((1, r, s), lambda n, c: (n, 0, 0)),
            const((1, s)), const((1, s)), const(wb.shape), const(wc.shape), const((1, w)),
        ],
        out_specs=[
            pl.BlockSpec((1, chunk_rows, w), lambda n, c: (n, c, 0)),
            pl.BlockSpec((1, r, s), lambda n, c: (n, 0, 0)),
            pl.BlockSpec((1, r, s), lambda n, c: (n, 0, 0)),
        ],
        out_shape=[jax.ShapeDtypeStruct((nbatch, rows, w), F32),
                   jax.ShapeDtypeStruct((nbatch, r, s), F32),
                   jax.ShapeDtypeStruct((nbatch, r, s), F32)],
        scratch_shapes=[pltpu.VMEM((chunk_rows, s), F32), pltpu.VMEM((chunk_rows, s), F32),
                        pltpu.VMEM((r, s), F32), pltpu.VMEM((r, s), F32)],
        compiler_params=_cparams("parallel", "arbitrary"),
        name="s5_scan",
    )(u, h0_re, h0_im, ab_re, ab_im, wb, wc, d)


def _key_to_f32(k):
    bits = jnp.where(k >= 0, k, k ^ jnp.int32(0x7FFFFFFF))
    return lax.bitcast_convert_type(bits, F32)


def _kth_threshold(count_ge, shape, topk):
    kf = float(topk)
    c_zero = count_ge(jnp.zeros(shape, F32))
    c_low = count_ge(jnp.full(shape, _LOWEST_F32, F32))
    pos = c_zero >= kf
    lo = jnp.where(pos, 0, KEY_LOWEST).astype(jnp.int32)
    hi = jnp.where(pos, KEY_POS_INF + 1, 0).astype(jnp.int32)
    cnt_lo = jnp.where(pos, c_zero, c_low)

    def body(_, carry):
        lo, hi, cnt_lo = carry
        mid = lo + ((hi - lo) >> 1)
        cnt = count_ge(_key_to_f32(mid))
        ok = cnt >= kf
        return jnp.where(ok, mid, lo), jnp.where(ok, hi, mid), jnp.where(ok, cnt, cnt_lo)

    lo, hi, cnt_lo = lax.fori_loop(0, 31, body, (lo, hi, cnt_lo))
    return _key_to_f32(lo), cnt_lo


def _tie_index_bound(count_eq_le, count_gt, cnt_ge, shape, topk, n_keys):
    need = float(topk) - count_gt
    tied = cnt_ge > float(topk)
    lo = jnp.full(shape, -1, jnp.int32)
    hi = jnp.full(shape, n_keys - 1, jnp.int32)

    def body(_, carry):
        lo, hi = carry
        mid = lo + ((hi - lo) >> 1)
        ok = count_eq_le(mid) >= need
        return jnp.where(ok, lo, mid), jnp.where(ok, mid, hi)

    n_iter = max(1, math.ceil(math.log2(n_keys + 1)))
    lo, hi = lax.fori_loop(0, n_iter, body, (lo, hi))
    return jnp.where(tied, hi, n_keys)


def _prompt_attn_kernel(qt_ref, qit_ref, w_ref, ki_ref, k_ref, vt_ref, o_ref,
                        s_ref, m_ref, l_ref, acc_ref, *, topk, q_block, chunk, n_heads):
    qb = pl.program_id(1)
    n_keys = k_ref.shape[1] * chunk
    n_chunks = (qb * q_block + q_block + chunk - 1) // chunk
    k_row = lax.broadcasted_iota(jnp.int32, (chunk, q_block), 0)
    q_pos = qb * q_block + lax.broadcasted_iota(jnp.int32, (chunk, q_block), 1)

    qit = qit_ref[0, 0].astype(BF16)
    w_row = w_ref[0, 0]

    def score_body(c, carry):
        st = _dot(ki_ref[0, c], qit)
        st = jnp.maximum(st, 0.0) * w_row
        sc = st[:, 0:q_block]
        for h in range(1, IDX_HEADS):
            sc = sc + st[:, h * q_block:(h + 1) * q_block]
        s_ref[c] = jnp.where(c * chunk + k_row <= q_pos, sc, NEG_INF)
        return carry

    lax.fori_loop(0, n_chunks, score_body, 0)

    row = (1, q_block)
    fold = min(chunk, 64)

    def count(pred):
        def body(c, acc):
            hits = jnp.where(pred(c, s_ref[c]), 1.0, 0.0)
            return acc + jnp.sum(hits.reshape(chunk // fold, fold, q_block), axis=0)
        acc = lax.fori_loop(0, n_chunks, body, jnp.zeros((fold, q_block), F32))
        return jnp.sum(acc, axis=0, keepdims=True)

    thr, cnt_ge = _kth_threshold(lambda t: count(lambda c, s: s >= t), row, topk)

    @pl.when(jnp.max(cnt_ge) > float(topk))
    def _():
        cnt_gt = count(lambda c, s: s > thr)
        bound = _tie_index_bound(
            lambda j: count(lambda c, s: (s == thr) & (c * chunk + k_row <= j)),
            cnt_gt, cnt_ge, row, topk, n_keys)

        def drop_body(c, carry):
            s = s_ref[c]
            s_ref[c] = jnp.where((s == thr) & (c * chunk + k_row > bound), NEG_INF, s)
            return carry

        lax.fori_loop(0, n_chunks, drop_body, 0)

    m_ref[...] = jnp.full(m_ref.shape, NEG_INF, F32)
    l_ref[...] = jnp.zeros(l_ref.shape, F32)
    acc_ref[...] = jnp.zeros(acc_ref.shape, F32)
    qt = qt_ref[0, 0] * (HEAD_DIM ** -0.5 * LOG2_E)
    head_of_dim = lax.broadcasted_iota(jnp.int32, (LANE, 2 * q_block), 0) // HEAD_DIM
    head_of_col = lax.broadcasted_iota(jnp.int32, (LANE, 2 * q_block), 1) // q_block
    q_pairs = []
    for j in range(n_heads // 2):
        qj = qt[j * LANE:(j + 1) * LANE, :]
        q_pairs.append(jnp.where(head_of_dim == head_of_col, jnp.concatenate([qj, qj], axis=1), 0.0).astype(BF16))

    def attn_body(c, carry):
        sel = s_ref[c] >= thr
        logits = [_dot(k_ref[0, c, :, j * LANE:(j + 1) * LANE], q_pairs[j]) for j in range(n_heads // 2)]
        for h in range(n_heads):
            j, hh = divmod(h, 2)
            x = jnp.where(sel, logits[j][:, hh * q_block:(hh + 1) * q_block], NEG_INF)
            m_old = m_ref[h]
            m_new = jnp.maximum(m_old, jnp.max(x, axis=0, keepdims=True))
            m_safe = jnp.where(m_new == NEG_INF, 0.0, m_new)
            p = jnp.exp2(x - m_safe)
            scale = jnp.exp2(m_old - m_safe)
            l_ref[h] = scale * l_ref[h] + jnp.sum(p, axis=0, keepdims=True)
            pv = _dot(vt_ref[0, c, h * HEAD_DIM:(h + 1) * HEAD_DIM, :], p.astype(BF16))
            acc_ref[h] = scale * acc_ref[h] + pv
            m_ref[h] = m_new
        return carry

    lax.fori_loop(0, n_chunks, attn_body, 0)

    out_t = jnp.concatenate([acc_ref[h] / l_ref[h] for h in range(n_heads)], axis=0)
    o_ref[0] = out_t.T


def _prompt_attention(q_t, qi_t, w_row, ki, k, v_t, topk, q_block, chunk):
    b, nqb, aw, _ = q_t.shape
    nch = k.shape[1]
    l = nch * chunk
    n_heads = aw // HEAD_DIM
    assert n_heads % 2 == 0 and nqb * q_block == l and chunk % q_block == 0 and 2 * HEAD_DIM == LANE
    kern = functools.partial(_prompt_attn_kernel, topk=topk, q_block=q_block, chunk=chunk, n_heads=n_heads)
    cols = IDX_HEADS * q_block
    return pl.pallas_call(
        kern,
        grid=(b, nqb),
        in_specs=[
            pl.BlockSpec((1, 1, aw, q_block), lambda i, j: (i, j, 0, 0)),
            pl.BlockSpec((1, 1, IDX_DIM, cols), lambda i, j: (i, j, 0, 0)),
            pl.BlockSpec((1, 1, 1, cols), lambda i, j: (i, j, 0, 0)),
            pl.BlockSpec((1, nch, chunk, IDX_DIM), lambda i, j: (i, 0, 0, 0)),
            pl.BlockSpec((1, nch, chunk, aw), lambda i, j: (i, 0, 0, 0)),
            pl.BlockSpec((1, nch, aw, chunk), lambda i, j: (i, 0, 0, 0)),
        ],
        out_specs=pl.BlockSpec((1, q_block, aw), lambda i, j: (i, j, 0)),
        out_shape=jax.ShapeDtypeStruct((b, l, aw), F32),
        scratch_shapes=[pltpu.VMEM((nch, chunk, q_block), F32),
                        pltpu.VMEM((n_heads, 1, q_block), F32),
                        pltpu.VMEM((n_heads, 1, q_block), F32),
                        pltpu.VMEM((n_heads, HEAD_DIM, q_block), F32)],
        compiler_params=_cparams("parallel", "arbitrary"),
        name="prompt_attention",
    )(q_t, qi_t, w_row, ki, k, v_t)


def _sample_select_kernel(pt_ref, qi_ref, w_ref, knew_ref, cache_ref, bias_ref, kbuf_ref, s_ref, sem,
                          *, topk, layer, group):
    step = pl.program_id(0)
    n_pages, page = kbuf_ref.shape[1] - 1, kbuf_ref.shape[3]
    n_q = knew_ref.shape[2]
    n_keys = (n_pages + 1) * page

    def page_copy(g, i):
        return pltpu.make_async_copy(cache_ref.at[layer, pt_ref[step * group + g, i]], kbuf_ref.at[g, i], sem.at[0])

    def start_body(g, carry):
        for i in range(n_pages):
            page_copy(g, i).start()
        return carry

    lax.fori_loop(0, group, start_body, 0)

    for g in range(group):
        kbuf_ref[g, n_pages] = jnp.zeros((IDX_DIM, page), F32)
        kbuf_ref[g, n_pages, :, 0:n_q] = knew_ref[g]

    def wait_body(g, carry):
        for i in range(n_pages):
            page_copy(g, i).wait()
        return carry

    lax.fori_loop(0, group, wait_body, 0)

    for g in range(group):
        qi = qi_ref[g].astype(BF16)
        w_col = w_ref[g]
        for i in range(n_pages + 1):
            st = _dot(qi, kbuf_ref[g, i].astype(BF16))
            st = jnp.maximum(st, 0.0) * w_col
            s_ref[g * n_q:(g + 1) * n_q, i * page:(i + 1) * page] = jnp.sum(st.reshape(IDX_HEADS, n_q, page), axis=0)

    rows = group * n_q
    q_of_row = lax.broadcasted_iota(jnp.int32, (rows, n_keys), 0) % n_q
    key_idx = lax.broadcasted_iota(jnp.int32, (rows, n_keys), 1)
    s_ref[...] = jnp.where(key_idx - n_pages * page <= q_of_row, s_ref[...], NEG_INF)

    col = (rows, 1)

    def count(pred):
        hits = jnp.where(pred(s_ref[...]), 1.0, 0.0)
        part = hits[:, 0:LANE]
        for t in range(1, n_keys // LANE):
            part = part + hits[:, t * LANE:(t + 1) * LANE]
        return jnp.sum(part, axis=1, keepdims=True)

    thr, cnt_ge = _kth_threshold(lambda t: count(lambda s: s >= t), col, topk)

    @pl.when(jnp.max(cnt_ge) > float(topk))
    def _():
        cnt_gt = count(lambda s: s > thr)
        bound = _tie_index_bound(lambda j: count(lambda s: (s == thr) & (key_idx <= j)),
                                 cnt_gt, cnt_ge, col, topk, n_keys)
        s = s_ref[...]
        s_ref[...] = jnp.where((s == thr) & (key_idx > bound), NEG_INF, s)

    bias_ref[...] = jnp.where(s_ref[...] >= thr, 0.0, NEG_INF).reshape(group, n_q, n_keys)


def _sample_select(page_table, qi_hm, w_col, cache_kidx_t, ki_new_t, layer, topk, group):
    db, n_pages = page_table.shape
    page = cache_kidx_t.shape[3]
    n_q = ki_new_t.shape[2]
    rows = IDX_HEADS * n_q
    n_keys = (n_pages + 1) * page
    assert db % group == 0 and n_q <= page
    kern = functools.partial(_sample_select_kernel, topk=topk, layer=layer, group=group)
    grid_spec = pltpu.PrefetchScalarGridSpec(
        num_scalar_prefetch=1,
        grid=(db // group,),
        in_specs=[
            pl.BlockSpec((group, rows, IDX_DIM), lambda s, pt: (s, 0, 0)),
            pl.BlockSpec((group, rows, 1), lambda s, pt: (s, 0, 0)),
            pl.BlockSpec((group, IDX_DIM, n_q), lambda s, pt: (s, 0, 0)),
            pl.BlockSpec(memory_space=pl.ANY),
        ],
        out_specs=pl.BlockSpec((group, n_q, n_keys), lambda s, pt: (s, 0, 0)),
        scratch_shapes=[pltpu.VMEM((group, n_pages + 1, IDX_DIM, page), F32),
                        pltpu.VMEM((group * n_q, n_keys), F32),
                        pltpu.SemaphoreType.DMA((1,))],
    )
    return pl.pallas_call(
        kern,
        grid_spec=grid_spec,
        out_shape=jax.ShapeDtypeStruct((db, n_q, n_keys), F32),
        compiler_params=_cparams("arbitrary"),
        name="sample_select",
    )(page_table, qi_hm, w_col, ki_new_t, cache_kidx_t)


def _sample_attn_kernel(pt_ref, q_ref, bias_ref, knew_ref, vnew_ref, kcache_ref, vcache_ref, o_ref,
                        kbuf_ref, vbuf_ref, lg_ref, kpad_ref, vpad_ref, sem, *, layer, n_heads):
    b = pl.program_id(0)
    n_seq = pl.num_programs(0)
    n_pages, page = kbuf_ref.shape[1], kbuf_ref.shape[4]
    n_q, aw = q_ref.shape[1], q_ref.shape[2]
    past = n_pages * page
    rows = n_heads * n_q
    slot = b % 2

    def page_copies(seq, i, to_slot):
        idx = pt_ref[seq, i]
        return (pltpu.make_async_copy(kcache_ref.at[layer, idx], kbuf_ref.at[to_slot, i], sem.at[to_slot, 0]),
                pltpu.make_async_copy(vcache_ref.at[layer, idx], vbuf_ref.at[to_slot, i], sem.at[to_slot, 1]))

    def start_fetch(seq, to_slot):
        for i in range(n_pages):
            for cp in page_copies(seq, i, to_slot):
                cp.start()

    @pl.when(b == 0)
    def _():
        start_fetch(0, 0)

    @pl.when(b + 1 < n_seq)
    def _():
        start_fetch(b + 1, 1 - slot)

    kpad_ref[...] = jnp.zeros(kpad_ref.shape, F32)
    vpad_ref[...] = jnp.zeros(vpad_ref.shape, F32)
    kpad_ref[0:n_q, :] = knew_ref[0]
    vpad_ref[0:n_q, :] = vnew_ref[0]

    q = q_ref[0] * (HEAD_DIM ** -0.5 * LOG2_E)
    head_of_row = lax.broadcasted_iota(jnp.int32, (rows, aw), 0) // n_q
    head_of_lane = lax.broadcasted_iota(jnp.int32, (rows, aw), 1) // HEAD_DIM
    own_head = head_of_row == head_of_lane
    q_diag = jnp.where(own_head, jnp.concatenate([q] * n_heads, axis=0), 0.0).astype(BF16)
    lg_ref[:, past:past + page] = _dot_nt(q_diag, kpad_ref[...].astype(BF16))

    for i in range(n_pages):
        for cp in page_copies(b, i, slot):
            cp.wait()

    for i in range(n_pages):
        k_t = kbuf_ref[slot, i].reshape(aw, page).astype(BF16)
        lg_ref[:, i * page:(i + 1) * page] = _dot(q_diag, k_t)

    logits = lg_ref[...] + jnp.concatenate([bias_ref[0]] * n_heads, axis=0)
    m = jnp.max(logits, axis=1, keepdims=True)
    pr = jnp.exp2(logits - m)
    denom = jnp.sum(pr, axis=1, keepdims=True)
    pr = pr.astype(BF16)

    acc = _dot(pr[:, past:past + page], vpad_ref[...].astype(BF16))
    for i in range(n_pages):
        v_t = vbuf_ref[slot, i].reshape(aw, page).astype(BF16)
        acc = acc + _dot_nt(pr[:, i * page:(i + 1) * page], v_t)
    out = jnp.where(own_head, acc / denom, 0.0)
    o_ref[0] = jnp.sum(out.reshape(n_heads, n_q, aw), axis=0)


def _sample_attention(page_table, q, bias, cache_k_t, cache_v_t, k_new, v_new, layer):
    db, n_pages = page_table.shape
    _, n_q, aw = q.shape
    n_heads = aw // HEAD_DIM
    page = cache_k_t.shape[4]
    n_keys = (n_pages + 1) * page
    rows = n_heads * n_q
    kern = functools.partial(_sample_attn_kernel, layer=layer, n_heads=n_heads)
    per_b = lambda shape: pl.BlockSpec(shape, lambda b, pt: (b,) + (0,) * (len(shape) - 1))
    grid_spec = pltpu.PrefetchScalarGridSpec(
        num_scalar_prefetch=1,
        grid=(db,),
        in_specs=[per_b((1, n_q, aw)), per_b((1, n_q, n_keys)), per_b((1, n_q, aw)), per_b((1, n_q, aw)),
                  pl.BlockSpec(memory_space=pl.ANY), pl.BlockSpec(memory_space=pl.ANY)],
        out_specs=per_b((1, n_q, aw)),
        scratch_shapes=[pltpu.VMEM((2, n_pages, n_heads, HEAD_DIM, page), F32),
                        pltpu.VMEM((2, n_pages, n_heads, HEAD_DIM, page), F32),
                        pltpu.VMEM((rows, n_keys), F32),
                        pltpu.VMEM((page, aw), F32), pltpu.VMEM((page, aw), F32),
                        pltpu.SemaphoreType.DMA((2, 2))],
    )
    return pl.pallas_call(
        kern,
        grid_spec=grid_spec,
        out_shape=jax.ShapeDtypeStruct((db, n_q, aw), F32),
        compiler_params=_cparams("arbitrary"),
        name="sample_attention",
    )(page_table, q, bias, k_new, v_new, cache_k_t, cache_v_t)


def _mix_kernel(x_ref, ys_ref, ya_ref, wglu_ref, bglu_ref, wos_ref, woa_ref, g_ref, b_ref, o_ref, *, alpha):
    y = ys_ref[...]
    gel = 0.5 * y * (1.0 + jnp.tanh(math.sqrt(2.0 / math.pi) * (y + 0.044715 * (y * y * y))))
    gate = jax.nn.sigmoid(_dot(gel.astype(BF16), wglu_ref[...]) + bglu_ref[...])
    y_ssm = gel * gate
    mix = _dot(y_ssm.astype(BF16), wos_ref[...]) + _dot(ya_ref[...].astype(BF16), woa_ref[...])
    o_ref[...] = _layer_norm(alpha * x_ref[...] + mix, g_ref[...], b_ref[...])


def _mix(x, y_ssm, y_att, w_glu, b_glu, w_out_ssm, w_out_att, g, b, alpha, tm):
    m, d = x.shape
    sw = y_ssm.shape[1]
    aw = y_att.shape[1]
    const = lambda shape: pl.BlockSpec(shape, lambda i: (0, 0))
    return pl.pallas_call(
        functools.partial(_mix_kernel, alpha=alpha),
        grid=(m // tm,),
        in_specs=[pl.BlockSpec((tm, d), lambda i: (i, 0)),
                  pl.BlockSpec((tm, sw), lambda i: (i, 0)),
                  pl.BlockSpec((tm, aw), lambda i: (i, 0)),
                  const((sw, sw)), const((1, sw)), const((sw, d)), const((aw, d)), const((1, d)), const((1, d))],
        out_specs=pl.BlockSpec((tm, d), lambda i: (i, 0)),
        out_shape=jax.ShapeDtypeStruct((m, d), F32),
        compiler_params=_cparams("parallel"),
        name="glu_outproj_ln",
    )(x, y_ssm, y_att, w_glu, b_glu, w_out_ssm, w_out_att, g, b)


def _row_tile(m, cap=512):
    t = cap
    while m % t:
        t //= 2
    return t


def _ff_tile(d_ff, cap=1536):
    best = LANE
    for t in range(LANE, cap + 1, LANE):
        if d_ff % t == 0:
            best = t
    return best


def _head_major(x, n_blocks, block, n_heads, dim):
    b = x.shape[0]
    x = x.reshape(b, n_blocks, block, n_heads, dim)
    return jnp.swapaxes(x, 2, 3).reshape(b, n_blocks, n_heads * block, dim)


def kernel(x_prompt, x_sample, cache_k, cache_v, cache_kidx, state_ssm_re, state_ssm_im, page_table,
           w_in, ssm_lambda_re, ssm_lambda_im, ssm_log_step, ssm_b_re, ssm_b_im, ssm_c_re, ssm_c_im,
           ssm_d, w_glu, b_glu, w_out, ffn1_up, ffn1_down, ffn2_up, ffn2_down, ln_g, ln_b):
    bsz, seq, d_model = x_prompt.shape
    db, dseq, _ = x_sample.shape
    depth = w_in.shape[0]
    sw = d_model // 2
    aw = d_model - sw
    n_groups = sw // SSM_GROUP
    n_state = n_groups * SSM_STATE
    n_heads = aw // HEAD_DIM
    page = cache_k.shape[2]
    n_pages = page_table.shape[1]
    past = n_pages * page
    alpha = (2.0 * depth) ** 0.25
    mp = bsz * seq
    ms = db * dseq
    tm_p = _row_tile(mp)
    tm_s = _row_tile(ms)
    q_block = min(128, seq)
    chunk = min(512, seq)
    nqb = seq // q_block
    nch = seq // chunk
    topk_p = min(TOPK_MAX, seq // 4)
    topk_s = min(TOPK_MAX, (past + dseq) // 4)
    s5_rows = min(256, seq)
    s_groups = max(1, ms // 256)
    s_per = db // s_groups
    select_group = math.gcd(db, SUBLANE)
    cache_k_t = jnp.transpose(cache_k, (0, 1, 3, 4, 2))
    cache_v_t = jnp.transpose(cache_v, (0, 1, 3, 4, 2))
    cache_kidx_t = jnp.swapaxes(cache_kidx, 2, 3)
    pad_w = LANE - IDX_DIM - IDX_HEADS
    widths = (sw, aw, aw, aw, IDX_HEADS * IDX_DIM, LANE)

    xp = x_prompt.reshape(mp, d_model)
    xs = x_sample.reshape(ms, d_model)
    outs = {name: [] for name in ("kp", "vp", "kip", "hrp", "hip", "ks", "vs", "kis", "hrs", "his")}

    for l in range(depth):
        g_ln = ln_g[l][:, None, :]
        b_ln = ln_b[l][:, None, :]
        up1, down1 = ffn1_up[l].astype(BF16), ffn1_down[l].astype(BF16)
        tf1 = _ff_tile(down1.shape[0])
        xp = _ffn(xp, up1, down1, g_ln[0], b_ln[0], alpha, tm_p, tf1)
        xs = _ffn(xs, up1, down1, g_ln[0], b_ln[0], alpha, tm_s, tf1)

        w_l = jnp.pad(w_in[l], ((0, 0), (0, pad_w))).astype(BF16)
        u_p, q_p, k_p, v_p, qi_p, kiwi_p = _inproj(xp, w_l, widths, tm_p)
        u_s, q_s, k_s, v_s, qi_s, kiwi_s = _inproj(xs, w_l, widths, tm_s)
        ki_p, wi_p = kiwi_p[:, :IDX_DIM], kiwi_p[:, IDX_DIM:IDX_DIM + IDX_HEADS]
        ki_s, wi_s = kiwi_s[:, :IDX_DIM], kiwi_s[:, IDX_DIM:IDX_DIM + IDX_HEADS]

        q_s = q_s.reshape(db, dseq, aw)
        k_s = k_s.reshape(db, dseq, aw)
        v_s = v_s.reshape(db, dseq, aw)
        ki_s = ki_s.reshape(db, dseq, IDX_DIM)
        qi_hm = _head_major(qi_s.reshape(db, dseq, IDX_HEADS * IDX_DIM), 1, dseq, IDX_HEADS, IDX_DIM)[:, 0]
        w_hm = _head_major(wi_s.reshape(db, dseq, IDX_HEADS), 1, dseq, IDX_HEADS, 1)[:, 0]
        bias = _sample_select(page_table, qi_hm, w_hm, cache_kidx_t, jnp.swapaxes(ki_s, 1, 2), l, topk_s, select_group)
        y_att_s = _sample_attention(page_table, q_s, bias, cache_k_t, cache_v_t, k_s, v_s, l)

        ab_re, ab_im, bb_re_t, bb_im_t = _s5_params(ssm_lambda_re[l], ssm_lambda_im[l], ssm_log_step[l],
                                                     ssm_b_re[l], ssm_b_im[l])
        wb = _re_im_stack(_block_diag_in(bb_re_t), _block_diag_in(bb_im_t))
        wc = _re_im_stack(_block_diag_out(ssm_c_re[l]), _block_diag_out(ssm_c_im[l]))
        ab_re = ab_re.reshape(1, n_state)
        ab_im = ab_im.reshape(1, n_state)
        d_row = ssm_d[l].reshape(1, sw)

        zeros_state = jnp.zeros((bsz, 1, n_state), F32)
        y_ssm_p, hr_p, hi_p = _s5_scan(u_p.reshape(bsz, seq, sw), zeros_state, zeros_state,
                                       ab_re, ab_im, wb, wc, d_row, s5_rows, 1)
        u_s = jnp.swapaxes(u_s.reshape(s_groups, s_per, dseq, sw), 1, 2)
        y_ssm_s, hr_s, hi_s = _s5_scan(u_s.reshape(s_groups, dseq * s_per, sw),
                                       state_ssm_re[l].reshape(s_groups, s_per, n_state),
                                       state_ssm_im[l].reshape(s_groups, s_per, n_state),
                                       ab_re, ab_im, wb, wc, d_row, dseq, s_per)
        y_ssm_s = jnp.swapaxes(y_ssm_s.reshape(s_groups, dseq, s_per, sw), 1, 2).reshape(ms, sw)

        q_t = jnp.swapaxes(q_p.reshape(bsz, nqb, q_block, aw), 2, 3)
        qi_t = jnp.transpose(qi_p.reshape(bsz, nqb, q_block, IDX_HEADS, IDX_DIM), (0, 1, 4, 3, 2))
        qi_t = qi_t.reshape(bsz, nqb, IDX_DIM, IDX_HEADS * q_block)
        w_t = jnp.swapaxes(wi_p.reshape(bsz, nqb, q_block, IDX_HEADS), 2, 3)
        w_t = w_t.reshape(bsz, nqb, 1, IDX_HEADS * q_block)
        v_t = jnp.swapaxes(v_p.reshape(bsz, nch, chunk, aw), 2, 3).astype(BF16)
        y_att_p = _prompt_attention(q_t, qi_t, w_t, ki_p.reshape(bsz, nch, chunk, IDX_DIM).astype(BF16),
                                    k_p.reshape(bsz, nch, chunk, aw).astype(BF16), v_t, topk_p, q_block, chunk)

        w_g = w_glu[l].astype(BF16)
        b_g = b_glu[l].reshape(1, sw)
        w_o = w_out[l].astype(BF16)
        xp = _mix(xp, y_ssm_p.reshape(mp, sw), y_att_p.reshape(mp, aw), w_g, b_g, w_o[:sw], w_o[sw:],
                  g_ln[1], b_ln[1], alpha, tm_p)
        xs = _mix(xs, y_ssm_s, y_att_s.reshape(ms, aw), w_g, b_g, w_o[:sw], w_o[sw:],
                  g_ln[1], b_ln[1], alpha, tm_s)
        up2, down2 = ffn2_up[l].astype(BF16), ffn2_down[l].astype(BF16)
        tf2 = _ff_tile(down2.shape[0])
        xp = _ffn(xp, up2, down2, g_ln[2], b_ln[2], alpha, tm_p, tf2)
        xs = _ffn(xs, up2, down2, g_ln[2], b_ln[2], alpha, tm_s, tf2)

        outs["kp"].append(k_p.reshape(bsz, seq, n_heads, HEAD_DIM))
        outs["vp"].append(v_p.reshape(bsz, seq, n_heads, HEAD_DIM))
        outs["kip"].append(ki_p.reshape(bsz, seq, IDX_DIM))
        outs["hrp"].append(hr_p.reshape(bsz, n_groups, SSM_STATE))
        outs["hip"].append(hi_p.reshape(bsz, n_groups, SSM_STATE))
        outs["ks"].append(k_s.reshape(db, dseq, n_heads, HEAD_DIM))
        outs["vs"].append(v_s.reshape(db, dseq, n_heads, HEAD_DIM))
        outs["kis"].append(ki_s)
        outs["hrs"].append(hr_s.reshape(db, n_groups, SSM_STATE))
        outs["his"].append(hi_s.reshape(db, n_groups, SSM_STATE))

    st = lambda name: jnp.stack(outs[name])
    return (xp.reshape(bsz, seq, d_model), xs.reshape(db, dseq, d_model),
            st("kp"), st("vp"), st("kip"), st("hrp"), st("hip"),
            st("ks"), st("vs"), st("kis"), st("hrs"), st("his"))
```

```python
import functools
import math

import jax
import jax.numpy as jnp
from jax import lax
from jax.experimental import pallas as pl
from jax.experimental.pallas import tpu as pltpu

F32 = jnp.float32
BF16 = jnp.bfloat16

SSM_GROUP = 16
SSM_STATE = 64
HEAD_DIM = 64
IDX_HEADS = 8
IDX_DIM = 32
TOPK_MAX = 256
LN_EPS = 1e-5

LANE = 128
SUBLANE = 8
VMEM_LIMIT_BYTES = 56 * 1024 * 1024

NEG_INF = float("-inf")
LOG2_E = 1.4426950408889634
_LOWEST_F32 = -3.4028234663852886e38
INT32_MIN = -2147483648
HIGH_HALF_MASK = -65536


def _cparams(*sem):
    return pltpu.CompilerParams(dimension_semantics=sem, vmem_limit_bytes=VMEM_LIMIT_BYTES)


def _layer_norm(y, g, b):
    mu = jnp.mean(y, axis=-1, keepdims=True)
    yc = y - mu
    var = jnp.mean(yc * yc, axis=-1, keepdims=True)
    return yc * lax.rsqrt(var + LN_EPS) * g + b


def _dot(a, b):
    return jnp.dot(a, b, preferred_element_type=F32)


def _dot_nt(a, b):
    return lax.dot_general(a, b, (((1,), (1,)), ((), ())), preferred_element_type=F32)


def _ffn_kernel(x_ref, wg_ref, wu_ref, wd_ref, g_ref, b_ref, o_ref, xb_ref, acc_ref, *, alpha):
    j = pl.program_id(1)

    @pl.when(j == 0)
    def _():
        xb_ref[...] = x_ref[...].astype(BF16)
        acc_ref[...] = jnp.zeros_like(acc_ref)

    xb = xb_ref[...]
    gate = _dot(xb, wg_ref[...])
    up = _dot(xb, wu_ref[...])
    act = (gate * jax.nn.sigmoid(gate)) * up
    acc_ref[...] += _dot(act.astype(BF16), wd_ref[...])

    @pl.when(j == pl.num_programs(1) - 1)
    def _():
        y = alpha * x_ref[...] + 0.5 * acc_ref[...]
        o_ref[...] = _layer_norm(y, g_ref[...], b_ref[...])


def _ffn(x, w_up, w_down, g, b, alpha, tm, tf):
    m, d = x.shape
    d_ff = w_down.shape[0]
    nf = d_ff // tf
    assert m % tm == 0 and d_ff % tf == 0
    return pl.pallas_call(
        functools.partial(_ffn_kernel, alpha=alpha),
        grid=(m // tm, nf),
        in_specs=[
            pl.BlockSpec((tm, d), lambda i, j: (i, 0)),
            pl.BlockSpec((d, tf), lambda i, j: (0, j)),
            pl.BlockSpec((d, tf), lambda i, j: (0, j + nf)),
            pl.BlockSpec((tf, d), lambda i, j: (j, 0)),
            pl.BlockSpec((1, d), lambda i, j: (0, 0)),
            pl.BlockSpec((1, d), lambda i, j: (0, 0)),
        ],
        out_specs=pl.BlockSpec((tm, d), lambda i, j: (i, 0)),
        out_shape=jax.ShapeDtypeStruct((m, d), F32),
        scratch_shapes=[pltpu.VMEM((tm, d), BF16), pltpu.VMEM((tm, d), F32)],
        compiler_params=_cparams("parallel", "arbitrary"),
        name="swiglu_ln",
    )(x, w_up, w_up, w_down, g, b)


def _inproj_kernel(x_ref, w_ref, *o_refs):
    y = _dot(x_ref[...].astype(BF16), w_ref[...])
    off = 0
    for o_ref in o_refs:
        n = o_ref.shape[1]
        o_ref[...] = y[:, off:off + n]
        off += n


def _inproj(x, w, widths, tm):
    m, d = x.shape
    n = w.shape[1]
    assert sum(widths) == n and m % tm == 0
    return pl.pallas_call(
        _inproj_kernel,
        grid=(m // tm,),
        in_specs=[pl.BlockSpec((tm, d), lambda i: (i, 0)),
                  pl.BlockSpec((d, n), lambda i: (0, 0))],
        out_specs=[pl.BlockSpec((tm, wd), lambda i: (i, 0)) for wd in widths],
        out_shape=[jax.ShapeDtypeStruct((m, wd), F32) for wd in widths],
        compiler_params=_cparams("parallel"),
        name="in_proj",
    )(x, w)


def _s5_param_kernel(lre_ref, lim_ref, ls_ref, bre_ref, bim_ref, abre_ref, abim_ref, bbre_ref, bbim_ref):
    lam_re = lre_ref[...]
    lam_im = lim_ref[...]
    step = jnp.exp(ls_ref[...])
    mag = jnp.exp(lam_re * step)
    ang = lam_im * step
    ab_re = mag * jnp.cos(ang)
    ab_im = mag * jnp.sin(ang)
    nr = ab_re - 1.0
    ni = ab_im
    den = lam_re * lam_re + lam_im * lam_im
    f_re = (nr * lam_re + ni * lam_im) / den
    f_im = (ni * lam_re - nr * lam_im) / den
    abre_ref[...] = ab_re
    abim_ref[...] = ab_im
    b_re = bre_ref[...]
    b_im = bim_ref[...]
    bbre_ref[...] = f_re[:, None, :] * b_re - f_im[:, None, :] * b_im
    bbim_ref[...] = f_re[:, None, :] * b_im + f_im[:, None, :] * b_re


def _s5_params(lam_re, lam_im, log_step, b_re, b_im):
    g, p = lam_re.shape
    h = b_re.shape[2]
    b_re_t = jnp.swapaxes(b_re, 1, 2)
    b_im_t = jnp.swapaxes(b_im, 1, 2)
    return pl.pallas_call(
        _s5_param_kernel,
        out_shape=[jax.ShapeDtypeStruct((g, p), F32), jax.ShapeDtypeStruct((g, p), F32),
                   jax.ShapeDtypeStruct((g, h, p), F32), jax.ShapeDtypeStruct((g, h, p), F32)],
        name="s5_params",
    )(lam_re, lam_im, log_step.reshape(g, 1), b_re_t, b_im_t)


def _block_diag_in(bb_t):
    g, h, p = bb_t.shape
    per = LANE // h
    x = bb_t.reshape(g // per, per, h, p)
    eye = jnp.eye(per, dtype=F32)
    return jnp.einsum("jahp,ab->jahbp", x, eye).reshape(g // per, per * h, per * p)


def _block_diag_out(c):
    g, h, p = c.shape
    per = LANE // h
    x = c.reshape(g // per, per, h, p)
    eye = jnp.eye(per, dtype=F32)
    return jnp.einsum("jahp,ab->japbh", x, eye).reshape(g // per, per * p, per * h)


def _re_im_stack(w_re, w_im):
    return jnp.stack([w_re, w_im]).astype(BF16)


def _s5_kernel(u_ref, h0re_ref, h0im_ref, abre_ref, abim_ref, wb_ref, wc_ref, d_ref,
               y_ref, hre_out, him_out, hre_s, him_s, stre_s, stim_s, *, n_steps, rows_per_step, row_group):
    c = pl.program_id(1)
    nb = wb_ref.shape[1]
    sw = wb_ref.shape[3]

    @pl.when(c == 0)
    def _():
        stre_s[...] = h0re_ref[0]
        stim_s[...] = h0im_ref[0]

    u = u_ref[0]
    u_b = u.astype(BF16)
    for j in range(nb):
        uj = u_b[:, j * LANE:(j + 1) * LANE]
        hre_s[:, j * sw:(j + 1) * sw] = _dot(uj, wb_ref[0, j])
        him_s[:, j * sw:(j + 1) * sw] = _dot(uj, wb_ref[1, j])

    a_re = abre_ref[...]
    a_im = abim_ref[...]

    def group_body(rg, carry):
        r0 = pl.multiple_of(rg * row_group, row_group)

        def step_body(t, h):
            h_re, h_im = h
            row = t * rows_per_step + r0
            b_re = hre_s[pl.ds(row, row_group), :]
            b_im = him_s[pl.ds(row, row_group), :]
            n_re = a_re * h_re - a_im * h_im + b_re
            n_im = a_re * h_im + a_im * h_re + b_im
            hre_s[pl.ds(row, row_group), :] = n_re
            him_s[pl.ds(row, row_group), :] = n_im
            return n_re, n_im

        h0 = (stre_s[pl.ds(r0, row_group), :], stim_s[pl.ds(r0, row_group), :])
        h_re, h_im = lax.fori_loop(0, n_steps, step_body, h0)
        stre_s[pl.ds(r0, row_group), :] = h_re
        stim_s[pl.ds(r0, row_group), :] = h_im
        return carry

    lax.fori_loop(0, rows_per_step // row_group, group_body, 0)

    for j in range(nb):
        yj = (_dot(hre_s[:, j * sw:(j + 1) * sw].astype(BF16), wc_ref[0, j])
              - _dot(him_s[:, j * sw:(j + 1) * sw].astype(BF16), wc_ref[1, j]))
        sl = slice(j * LANE, (j + 1) * LANE)
        y_ref[0, :, sl] = yj + d_ref[:, sl] * u[:, sl]

    @pl.when(c == pl.num_programs(1) - 1)
    def _():
        hre_out[0] = stre_s[...]
        him_out[0] = stim_s[...]


def _s5_scan(u, h0_re, h0_im, ab_re, ab_im, wb, wc, d, n_steps, rows_per_step):
    nbatch, rows, w = u.shape
    s = ab_re.shape[1]
    r = rows_per_step
    chunk_rows = n_steps * r
    assert rows % chunk_rows == 0
    row_group = min(r, SUBLANE)
    assert r % row_group == 0
    kern = functools.partial(_s5_kernel, n_steps=n_steps, rows_per_step=r, row_group=row_group)
    const = lambda shape: pl.BlockSpec(shape, lambda n, c: (0,) * len(shape))
    return pl.pallas_call(
        kern,
        grid=(nbatch, rows // chunk_rows),
        in_specs=[
            pl.BlockSpec((1, chunk_rows, w), lambda n, c: (n, c, 0)),
            pl.BlockSpec((1, r, s), lambda n, c: (n, 0, 0)),
            pl.BlockSpec((1, r, s), lambda n, c: (n, 0, 0)),
            const((1, s)), const((1, s)), const(wb.shape), const(wc.shape), const((1, w)),
        ],
        out_specs=[
            pl.BlockSpec((1, chunk_rows, w), lambda n, c: (n, c, 0)),
            pl.BlockSpec((1, r, s), lambda n, c: (n, 0, 0)),
            pl.BlockSpec((1, r, s), lambda n, c: (n, 0, 0)),
        ],
        out_shape=[jax.ShapeDtypeStruct((nbatch, rows, w), F32),
                   jax.ShapeDtypeStruct((nbatch, r, s), F32),
                   jax.ShapeDtypeStruct((nbatch, r, s), F32)],
        scratch_shapes=[pltpu.VMEM((chunk_rows, s), F32), pltpu.VMEM((chunk_rows, s), F32),
                        pltpu.VMEM((r, s), F32), pltpu.VMEM((r, s), F32)],
        compiler_params=_cparams("parallel", "arbitrary"),
        name="s5_scan",
    )(u, h0_re, h0_im, ab_re, ab_im, wb, wc, d)


def _key_to_f32(k):
    bits = jnp.where(k >= 0, k, k ^ jnp.int32(0x7FFFFFFF))
    return lax.bitcast_convert_type(bits, F32)


def _kth_threshold(count_ge, shape, topk, count_ge_coarse=None, counts_at_zero_and_lowest=None):
    kf = float(topk)
    if counts_at_zero_and_lowest is None:
        c_zero = count_ge(jnp.zeros(shape, F32))
        c_low = count_ge(jnp.full(shape, _LOWEST_F32, F32))
    else:
        c_zero, c_low = counts_at_zero_and_lowest
    pos = c_zero >= kf
    key = jnp.where(pos, 0, INT32_MIN).astype(jnp.int32)
    cnt_key = jnp.where(pos, c_zero, c_low)

    def descend(counter, top_bit, n_bits, carry):
        def body(i, carry):
            key, cnt_key = carry
            cand = key | (jnp.int32(1) << (top_bit - i))
            cnt = counter(_key_to_f32(cand))
            ok = cnt >= kf
            return jnp.where(ok, cand, key), jnp.where(ok, cnt, cnt_key)
        return lax.fori_loop(0, n_bits, body, carry)

    carry = descend(count_ge_coarse or count_ge, 30, 15, (key, cnt_key))
    key, cnt_key = descend(count_ge, 15, 16, carry)
    enough = c_low >= kf
    return jnp.where(enough, _key_to_f32(key), _LOWEST_F32), jnp.where(enough, cnt_key, c_low)


def _tie_index_bound(count_eq_le, count_gt, cnt_ge, shape, topk, n_keys):
    need = float(topk) - count_gt
    tied = cnt_ge > float(topk)
    lo = jnp.full(shape, -1, jnp.int32)
    hi = jnp.full(shape, n_keys - 1, jnp.int32)

    def body(_, carry):
        lo, hi = carry
        mid = lo + ((hi - lo) >> 1)
        ok = count_eq_le(mid) >= need
        return jnp.where(ok, lo, mid), jnp.where(ok, mid, hi)

    n_iter = max(1, math.ceil(math.log2(n_keys + 1)))
    lo, hi = lax.fori_loop(0, n_iter, body, (lo, hi))
    return jnp.where(tied, hi, n_keys)


def _prompt_attn_kernel(qt_ref, qit_ref, w_ref, ki_ref, k_ref, vt_ref, o_ref,
                        s_ref, sb_ref, m_ref, l_ref, acc_ref, *, topk, q_block, chunk, n_heads):
    qb = pl.program_id(1)
    a_chunk = k_ref.shape[2]
    n_keys = k_ref.shape[1] * a_chunk
    n_chunks = (qb * q_block + q_block + chunk - 1) // chunk
    k_row = lax.broadcasted_iota(jnp.int32, (chunk, q_block), 0)
    q_pos = qb * q_block + lax.broadcasted_iota(jnp.int32, (chunk, q_block), 1)

    qit = qit_ref[0, 0].astype(BF16)
    w_row = w_ref[0, 0]

    row = (1, q_block)
    fold = min(chunk, 64)
    assert n_keys // fold <= 256

    def fold_rows(hits):
        parts = [hits[i * fold:(i + 1) * fold] for i in range(chunk // fold)]
        while len(parts) > 1:
            parts = [a + b for a, b in zip(parts[0::2], parts[1::2])] + parts[len(parts) & ~1:]
        return parts[0]

    def truncate_to_bf16(x):
        bits = lax.bitcast_convert_type(x, jnp.int32) & HIGH_HALF_MASK
        return lax.bitcast_convert_type(bits, F32).astype(BF16)

    def score_body(c, carry):
        acc_zero, acc_low = carry
        st = _dot(ki_ref[0, c], qit)
        st = jnp.maximum(st, 0.0) * w_row
        sc = st[:, 0:q_block]
        for h in range(1, IDX_HEADS):
            sc = sc + st[:, h * q_block:(h + 1) * q_block]
        sc = jnp.where(c * chunk + k_row <= q_pos, sc, NEG_INF)
        s_ref[c] = sc
        sb_ref[c] = truncate_to_bf16(sc)
        return (acc_zero + fold_rows(jnp.where(sc >= 0.0, 1.0, 0.0)),
                acc_low + fold_rows(jnp.where(sc >= _LOWEST_F32, 1.0, 0.0)))

    zeros_fold = jnp.zeros((fold, q_block), F32)
    acc_zero, acc_low = lax.fori_loop(0, n_chunks, score_body, (zeros_fold, zeros_fold))
    fixed_counts = (jnp.sum(acc_zero, axis=0, keepdims=True), jnp.sum(acc_low, axis=0, keepdims=True))

    def count(pred):
        def body(c, acc):
            return acc + fold_rows(jnp.where(pred(c, s_ref[c]), 1.0, 0.0))
        return jnp.sum(lax.fori_loop(0, n_chunks, body, zeros_fold), axis=0, keepdims=True)

    def count_ge_coarse(t):
        t16 = truncate_to_bf16(t)
        one, zero = jnp.ones((), BF16), jnp.zeros((), BF16)

        def body(c, acc):
            return acc + fold_rows(jnp.where(sb_ref[c] >= t16, one, zero))
        acc = lax.fori_loop(0, n_chunks, body, jnp.zeros((fold, q_block), BF16))
        return jnp.sum(acc.astype(F32), axis=0, keepdims=True)

    thr, cnt_ge = _kth_threshold(lambda t: count(lambda c, s: s >= t), row, topk,
                                 count_ge_coarse=count_ge_coarse, counts_at_zero_and_lowest=fixed_counts)

    @pl.when(jnp.max(cnt_ge) > float(topk))
    def _():
        cnt_gt = count(lambda c, s: s > thr)
        bound = _tie_index_bound(
            lambda j: count(lambda c, s: (s == thr) & (c * chunk + k_row <= j)),
            cnt_gt, cnt_ge, row, topk, n_keys)

        def drop_body(c, carry):
            s = s_ref[c]
            s_ref[c] = jnp.where((s == thr) & (c * chunk + k_row > bound), NEG_INF, s)
            return carry

        lax.fori_loop(0, n_chunks, drop_body, 0)

    m_ref[...] = jnp.full(m_ref.shape, NEG_INF, F32)
    l_ref[...] = jnp.zeros(l_ref.shape, F32)
    acc_ref[...] = jnp.zeros(acc_ref.shape, F32)
    qt = qt_ref[0, 0] * (HEAD_DIM ** -0.5 * LOG2_E)
    head_of_dim = lax.broadcasted_iota(jnp.int32, (LANE, 2 * q_block), 0) // HEAD_DIM
    head_of_col = lax.broadcasted_iota(jnp.int32, (LANE, 2 * q_block), 1) // q_block
    q_pairs = []
    for j in range(n_heads // 2):
        qj = qt[j * LANE:(j + 1) * LANE, :]
        q_pairs.append(jnp.where(head_of_dim == head_of_col, jnp.concatenate([qj, qj], axis=1), 0.0).astype(BF16))

    def attn_body(c, carry):
        per = chunk // a_chunk
        off = pl.multiple_of((c % per) * a_chunk, a_chunk)
        sel = s_ref[c // per, pl.ds(off, a_chunk), :] >= thr
        logits = [_dot(k_ref[0, c, :, j * LANE:(j + 1) * LANE], q_pairs[j]) for j in range(n_heads // 2)]
        for h in range(n_heads):
            j, hh = divmod(h, 2)
            x = jnp.where(sel, logits[j][:, hh * q_block:(hh + 1) * q_block], NEG_INF)
            m_old = m_ref[h]
            m_new = jnp.maximum(m_old, jnp.max(x, axis=0, keepdims=True))
            m_safe = jnp.where(m_new == NEG_INF, 0.0, m_new)
            p = jnp.exp2(x - m_safe)
            scale = jnp.exp2(m_old - m_safe)
            l_ref[h] = scale * l_ref[h] + jnp.sum(p, axis=0, keepdims=True)
            pv = _dot(vt_ref[0, c, h * HEAD_DIM:(h + 1) * HEAD_DIM, :], p.astype(BF16))
            acc_ref[h] = scale * acc_ref[h] + pv
            m_ref[h] = m_new
        return carry

    lax.fori_loop(0, (qb * q_block + q_block + a_chunk - 1) // a_chunk, attn_body, 0)

    out_t = jnp.concatenate([acc_ref[h] / l_ref[h] for h in range(n_heads)], axis=0)
    o_ref[0] = out_t.T


def _prompt_attention(q_t, qi_t, w_row, ki, k, v_t, topk, q_block, chunk):
    b, nqb, aw, _ = q_t.shape
    nch = ki.shape[1]
    l = nch * chunk
    nach, a_chunk = k.shape[1], k.shape[2]
    assert nach * a_chunk == l and chunk % a_chunk == 0 and a_chunk % q_block == 0
    n_heads = aw // HEAD_DIM
    assert n_heads % 2 == 0 and nqb * q_block == l and chunk % q_block == 0 and 2 * HEAD_DIM == LANE
    kern = functools.partial(_prompt_attn_kernel, topk=topk, q_block=q_block, chunk=chunk, n_heads=n_heads)
    cols = IDX_HEADS * q_block
    return pl.pallas_call(
        kern,
        grid=(b, nqb),
        in_specs=[
            pl.BlockSpec((1, 1, aw, q_block), lambda i, j: (i, j, 0, 0)),
            pl.BlockSpec((1, 1, IDX_DIM, cols), lambda i, j: (i, j, 0, 0)),
            pl.BlockSpec((1, 1, 1, cols), lambda i, j: (i, j, 0, 0)),
            pl.BlockSpec((1, nch, chunk, IDX_DIM), lambda i, j: (i, 0, 0, 0)),
            pl.BlockSpec((1, nach, a_chunk, aw), lambda i, j: (i, 0, 0, 0)),
            pl.BlockSpec((1, nach, aw, a_chunk), lambda i, j: (i, 0, 0, 0)),
        ],
        out_specs=pl.BlockSpec((1, q_block, aw), lambda i, j: (i, j, 0)),
        out_shape=jax.ShapeDtypeStruct((b, l, aw), F32),
        scratch_shapes=[pltpu.VMEM((nch, chunk, q_block), F32),
                        pltpu.VMEM((nch, chunk, q_block), BF16),
                        pltpu.VMEM((n_heads, 1, q_block), F32),
                        pltpu.VMEM((n_heads, 1, q_block), F32),
                        pltpu.VMEM((n_heads, HEAD_DIM, q_block), F32)],
        compiler_params=_cparams("parallel", "arbitrary"),
        name="prompt_attention",
    )(q_t, qi_t, w_row, ki, k, v_t)


def _sample_select_kernel(pt_ref, qi_ref, w_ref, knew_ref, cache_ref, bias_ref, kbuf_ref, s_ref, sem,
                          *, topk, layer, group):
    step = pl.program_id(0)
    n_pages, page = kbuf_ref.shape[1] - 1, kbuf_ref.shape[3]
    n_q = knew_ref.shape[2]
    n_keys = (n_pages + 1) * page

    def page_copy(g, i):
        return pltpu.make_async_copy(cache_ref.at[layer, pt_ref[step * group + g, i]], kbuf_ref.at[g, i], sem.at[0])

    def start_body(g, carry):
        for i in range(n_pages):
            page_copy(g, i).start()
        return carry

    lax.fori_loop(0, group, start_body, 0)

    for g in range(group):
        kbuf_ref[g, n_pages] = jnp.zeros((IDX_DIM, page), F32)
        kbuf_ref[g, n_pages, :, 0:n_q] = knew_ref[g]

    def wait_body(g, carry):
        for i in range(n_pages):
            page_copy(g, i).wait()
        return carry

    lax.fori_loop(0, group, wait_body, 0)

    for g in range(group):
        qi = qi_ref[g].astype(BF16)
        w_col = w_ref[g]
        for i in range(n_pages + 1):
            st = _dot(qi, kbuf_ref[g, i].astype(BF16))
            st = jnp.maximum(st, 0.0) * w_col
            s_ref[g * n_q:(g + 1) * n_q, i * page:(i + 1) * page] = jnp.sum(st.reshape(IDX_HEADS, n_q, page), axis=0)

    rows = group * n_q
    q_of_row = lax.broadcasted_iota(jnp.int32, (rows, n_keys), 0) % n_q
    key_idx = lax.broadcasted_iota(jnp.int32, (rows, n_keys), 1)
    s_ref[...] = jnp.where(key_idx - n_pages * page <= q_of_row, s_ref[...], NEG_INF)

    col = (rows, 1)

    def count(pred):
        hits = jnp.where(pred(s_ref[...]), 1.0, 0.0)
        part = hits[:, 0:LANE]
        for t in range(1, n_keys // LANE):
            part = part + hits[:, t * LANE:(t + 1) * LANE]
        return jnp.sum(part, axis=1, keepdims=True)

    thr, cnt_ge = _kth_threshold(lambda t: count(lambda s: s >= t), col, topk)

    @pl.when(jnp.max(cnt_ge) > float(topk))
    def _():
        cnt_gt = count(lambda s: s > thr)
        bound = _tie_index_bound(lambda j: count(lambda s: (s == thr) & (key_idx <= j)),
                                 cnt_gt, cnt_ge, col, topk, n_keys)
        s = s_ref[...]
        s_ref[...] = jnp.where((s == thr) & (key_idx > bound), NEG_INF, s)

    bias_ref[...] = jnp.where(s_ref[...] >= thr, 0.0, NEG_INF).reshape(group, n_q, n_keys)


def _sample_select(page_table, qi_hm, w_col, cache_kidx_t, ki_new_t, layer, topk, group):
    db, n_pages = page_table.shape
    page = cache_kidx_t.shape[3]
    n_q = ki_new_t.shape[2]
    rows = IDX_HEADS * n_q
    n_keys = (n_pages + 1) * page
    assert db % group == 0 and n_q <= page
    kern = functools.partial(_sample_select_kernel, topk=topk, layer=layer, group=group)
    grid_spec = pltpu.PrefetchScalarGridSpec(
        num_scalar_prefetch=1,
        grid=(db // group,),
        in_specs=[
            pl.BlockSpec((group, rows, IDX_DIM), lambda s, pt: (s, 0, 0)),
            pl.BlockSpec((group, rows, 1), lambda s, pt: (s, 0, 0)),
            pl.BlockSpec((group, IDX_DIM, n_q), lambda s, pt: (s, 0, 0)),
            pl.BlockSpec(memory_space=pl.ANY),
        ],
        out_specs=pl.BlockSpec((group, n_q, n_keys), lambda s, pt: (s, 0, 0)),
        scratch_shapes=[pltpu.VMEM((group, n_pages + 1, IDX_DIM, page), F32),
                        pltpu.VMEM((group * n_q, n_keys), F32),
                        pltpu.SemaphoreType.DMA((1,))],
    )
    return pl.pallas_call(
        kern,
        grid_spec=grid_spec,
        out_shape=jax.ShapeDtypeStruct((db, n_q, n_keys), F32),
        compiler_params=_cparams("arbitrary"),
        name="sample_select",
    )(page_table, qi_hm, w_col, ki_new_t, cache_kidx_t)


def _sample_attn_kernel(pt_ref, q_ref, bias_ref, knew_ref, vnew_ref, kcache_ref, vcache_ref, o_ref,
                        kbuf_ref, vbuf_ref, lg_ref, kpad_ref, vpad_ref, sem, *, layer, n_heads):
    b = pl.program_id(0)
    n_seq = pl.num_programs(0)
    n_pages, page = kbuf_ref.shape[1], kbuf_ref.shape[4]
    n_q, aw = q_ref.shape[1], q_ref.shape[2]
    past = n_pages * page
    rows = n_heads * n_q
    slot = b % 2

    def page_copies(seq, i, to_slot):
        idx = pt_ref[seq, i]
        return (pltpu.make_async_copy(kcache_ref.at[layer, idx], kbuf_ref.at[to_slot, i], sem.at[to_slot, 0]),
                pltpu.make_async_copy(vcache_ref.at[layer, idx], vbuf_ref.at[to_slot, i], sem.at[to_slot, 1]))

    def start_fetch(seq, to_slot):
        for i in range(n_pages):
            for cp in page_copies(seq, i, to_slot):
                cp.start()

    @pl.when(b == 0)
    def _():
        start_fetch(0, 0)

    @pl.when(b + 1 < n_seq)
    def _():
        start_fetch(b + 1, 1 - slot)

    kpad_ref[...] = jnp.zeros(kpad_ref.shape, F32)
    vpad_ref[...] = jnp.zeros(vpad_ref.shape, F32)
    kpad_ref[0:n_q, :] = knew_ref[0]
    vpad_ref[0:n_q, :] = vnew_ref[0]

    q = q_ref[0] * (HEAD_DIM ** -0.5 * LOG2_E)
    head_of_row = lax.broadcasted_iota(jnp.int32, (rows, aw), 0) // n_q
    head_of_lane = lax.broadcasted_iota(jnp.int32, (rows, aw), 1) // HEAD_DIM
    own_head = head_of_row == head_of_lane
    q_diag = jnp.where(own_head, jnp.concatenate([q] * n_heads, axis=0), 0.0).astype(BF16)
    lg_ref[:, past:past + page] = _dot_nt(q_diag, kpad_ref[...].astype(BF16))

    for i in range(n_pages):
        for cp in page_copies(b, i, slot):
            cp.wait()

    for i in range(n_pages):
        k_t = kbuf_ref[slot, i].reshape(aw, page).astype(BF16)
        lg_ref[:, i * page:(i + 1) * page] = _dot(q_diag, k_t)

    logits = lg_ref[...] + jnp.concatenate([bias_ref[0]] * n_heads, axis=0)
    m = jnp.max(logits, axis=1, keepdims=True)
    pr = jnp.exp2(logits - m)
    denom = jnp.sum(pr, axis=1, keepdims=True)
    pr = pr.astype(BF16)

    acc = _dot(pr[:, past:past + page], vpad_ref[...].astype(BF16))
    for i in range(n_pages):
        v_t = vbuf_ref[slot, i].reshape(aw, page).astype(BF16)
        acc = acc + _dot_nt(pr[:, i * page:(i + 1) * page], v_t)
    out = jnp.where(own_head, acc / denom, 0.0)
    o_ref[0] = jnp.sum(out.reshape(n_heads, n_q, aw), axis=0)


def _sample_attention(page_table, q, bias, cache_k_t, cache_v_t, k_new, v_new, layer):
    db, n_pages = page_table.shape
    _, n_q, aw = q.shape
    n_heads = aw // HEAD_DIM
    page = cache_k_t.shape[4]
    n_keys = (n_pages + 1) * page
    rows = n_heads * n_q
    kern = functools.partial(_sample_attn_kernel, layer=layer, n_heads=n_heads)
    per_b = lambda shape: pl.BlockSpec(shape, lambda b, pt: (b,) + (0,) * (len(shape) - 1))
    grid_spec = pltpu.PrefetchScalarGridSpec(
        num_scalar_prefetch=1,
        grid=(db,),
        in_specs=[per_b((1, n_q, aw)), per_b((1, n_q, n_keys)), per_b((1, n_q, aw)), per_b((1, n_q, aw)),
                  pl.BlockSpec(memory_space=pl.ANY), pl.BlockSpec(memory_space=pl.ANY)],
        out_specs=per_b((1, n_q, aw)),
        scratch_shapes=[pltpu.VMEM((2, n_pages, n_heads, HEAD_DIM, page), F32),
                        pltpu.VMEM((2, n_pages, n_heads, HEAD_DIM, page), F32),
                        pltpu.VMEM((rows, n_keys), F32),
                        pltpu.VMEM((page, aw), F32), pltpu.VMEM((page, aw), F32),
                        pltpu.SemaphoreType.DMA((2, 2))],
    )
    return pl.pallas_call(
        kern,
        grid_spec=grid_spec,
        out_shape=jax.ShapeDtypeStruct((db, n_q, aw), F32),
        compiler_params=_cparams("arbitrary"),
        name="sample_attention",
    )(page_table, q, bias, k_new, v_new, cache_k_t, cache_v_t)


def _mix_kernel(x_ref, ys_ref, ya_ref, wglu_ref, bglu_ref, wos_ref, woa_ref, g_ref, b_ref, o_ref, *, alpha):
    y = ys_ref[...]
    gel = 0.5 * y * (1.0 + jnp.tanh(math.sqrt(2.0 / math.pi) * (y + 0.044715 * (y * y * y))))
    gate = jax.nn.sigmoid(_dot(gel.astype(BF16), wglu_ref[...]) + bglu_ref[...])
    y_ssm = gel * gate
    mix = _dot(y_ssm.astype(BF16), wos_ref[...]) + _dot(ya_ref[...].astype(BF16), woa_ref[...])
    o_ref[...] = _layer_norm(alpha * x_ref[...] + mix, g_ref[...], b_ref[...])


def _mix(x, y_ssm, y_att, w_glu, b_glu, w_out_ssm, w_out_att, g, b, alpha, tm):
    m, d = x.shape
    sw = y_ssm.shape[1]
    aw = y_att.shape[1]
    const = lambda shape: pl.BlockSpec(shape, lambda i: (0, 0))
    return pl.pallas_call(
        functools.partial(_mix_kernel, alpha=alpha),
        grid=(m // tm,),
        in_specs=[pl.BlockSpec((tm, d), lambda i: (i, 0)),
                  pl.BlockSpec((tm, sw), lambda i: (i, 0)),
                  pl.BlockSpec((tm, aw), lambda i: (i, 0)),
                  const((sw, sw)), const((1, sw)), const((sw, d)), const((aw, d)), const((1, d)), const((1, d))],
        out_specs=pl.BlockSpec((tm, d), lambda i: (i, 0)),
        out_shape=jax.ShapeDtypeStruct((m, d), F32),
        compiler_params=_cparams("parallel"),
        name="glu_outproj_ln",
    )(x, y_ssm, y_att, w_glu, b_glu, w_out_ssm, w_out_att, g, b)


def _row_tile(m, cap=512):
    t = cap
    while m % t:
        t //= 2
    return t


def _ff_tile(d_ff, cap=1536):
    best = LANE
    for t in range(LANE, cap + 1, LANE):
        if d_ff % t == 0:
            best = t
    return best


def _head_major(x, n_blocks, block, n_heads, dim):
    b = x.shape[0]
    x = x.reshape(b, n_blocks, block, n_heads, dim)
    return jnp.swapaxes(x, 2, 3).reshape(b, n_blocks, n_heads * block, dim)


def kernel(x_prompt, x_sample, cache_k, cache_v, cache_kidx, state_ssm_re, state_ssm_im, page_table,
           w_in, ssm_lambda_re, ssm_lambda_im, ssm_log_step, ssm_b_re, ssm_b_im, ssm_c_re, ssm_c_im,
           ssm_d, w_glu, b_glu, w_out, ffn1_up, ffn1_down, ffn2_up, ffn2_down, ln_g, ln_b):
    bsz, seq, d_model = x_prompt.shape
    db, dseq, _ = x_sample.shape
    depth = w_in.shape[0]
    sw = d_model // 2
    aw = d_model - sw
    n_groups = sw // SSM_GROUP
    n_state = n_groups * SSM_STATE
    n_heads = aw // HEAD_DIM
    page = cache_k.shape[2]
    n_pages = page_table.shape[1]
    past = n_pages * page
    alpha = (2.0 * depth) ** 0.25
    mp = bsz * seq
    ms = db * dseq
    tm_p = _row_tile(mp)
    tm_s = _row_tile(ms)
    q_block = min(128, seq)
    chunk = min(512, seq)
    a_chunk = min(512, seq)
    nqb = seq // q_block
    nch = seq // chunk
    topk_p = min(TOPK_MAX, seq // 4)
    topk_s = min(TOPK_MAX, (past + dseq) // 4)
    s5_rows = min(256, seq)
    s_groups = max(1, ms // 256)
    s_per = db // s_groups
    select_group = math.gcd(db, SUBLANE)
    cache_k_t = jnp.transpose(cache_k, (0, 1, 3, 4, 2))
    cache_v_t = jnp.transpose(cache_v, (0, 1, 3, 4, 2))
    cache_kidx_t = jnp.swapaxes(cache_kidx, 2, 3)
    pad_w = LANE - IDX_DIM - IDX_HEADS
    widths = (sw, aw, aw, aw, IDX_HEADS * IDX_DIM, LANE)

    xp = x_prompt.reshape(mp, d_model)
    xs = x_sample.reshape(ms, d_model)
    outs = {name: [] for name in ("kp", "vp", "kip", "hrp", "hip", "ks", "vs", "kis", "hrs", "his")}

    for l in range(depth):
        g_ln = ln_g[l][:, None, :]
        b_ln = ln_b[l][:, None, :]
        up1, down1 = ffn1_up[l].astype(BF16), ffn1_down[l].astype(BF16)
        tf1 = _ff_tile(down1.shape[0])
        xp = _ffn(xp, up1, down1, g_ln[0], b_ln[0], alpha, tm_p, tf1)
        xs = _ffn(xs, up1, down1, g_ln[0], b_ln[0], alpha, tm_s, tf1)

        w_l = jnp.pad(w_in[l], ((0, 0), (0, pad_w))).astype(BF16)
        u_p, q_p, k_p, v_p, qi_p, kiwi_p = _inproj(xp, w_l, widths, tm_p)
        u_s, q_s, k_s, v_s, qi_s, kiwi_s = _inproj(xs, w_l, widths, tm_s)
        ki_p, wi_p = kiwi_p[:, :IDX_DIM], kiwi_p[:, IDX_DIM:IDX_DIM + IDX_HEADS]
        ki_s, wi_s = kiwi_s[:, :IDX_DIM], kiwi_s[:, IDX_DIM:IDX_DIM + IDX_HEADS]

        q_s = q_s.reshape(db, dseq, aw)
        k_s = k_s.reshape(db, dseq, aw)
        v_s = v_s.reshape(db, dseq, aw)
        ki_s = ki_s.reshape(db, dseq, IDX_DIM)
        qi_hm = _head_major(qi_s.reshape(db, dseq, IDX_HEADS * IDX_DIM), 1, dseq, IDX_HEADS, IDX_DIM)[:, 0]
        w_hm = _head_major(wi_s.reshape(db, dseq, IDX_HEADS), 1, dseq, IDX_HEADS, 1)[:, 0]
        bias = _sample_select(page_table, qi_hm, w_hm, cache_kidx_t, jnp.swapaxes(ki_s, 1, 2), l, topk_s, select_group)
        y_att_s = _sample_attention(page_table, q_s, bias, cache_k_t, cache_v_t, k_s, v_s, l)

        ab_re, ab_im, bb_re_t, bb_im_t = _s5_params(ssm_lambda_re[l], ssm_lambda_im[l], ssm_log_step[l],
                                                     ssm_b_re[l], ssm_b_im[l])
        wb = _re_im_stack(_block_diag_in(bb_re_t), _block_diag_in(bb_im_t))
        wc = _re_im_stack(_block_diag_out(ssm_c_re[l]), _block_diag_out(ssm_c_im[l]))
        ab_re = ab_re.reshape(1, n_state)
        ab_im = ab_im.reshape(1, n_state)
        d_row = ssm_d[l].reshape(1, sw)

        zeros_state = jnp.zeros((bsz, 1, n_state), F32)
        y_ssm_p, hr_p, hi_p = _s5_scan(u_p.reshape(bsz, seq, sw), zeros_state, zeros_state,
                                       ab_re, ab_im, wb, wc, d_row, s5_rows, 1)
        u_s = jnp.swapaxes(u_s.reshape(s_groups, s_per, dseq, sw), 1, 2)
        y_ssm_s, hr_s, hi_s = _s5_scan(u_s.reshape(s_groups, dseq * s_per, sw),
                                       state_ssm_re[l].reshape(s_groups, s_per, n_state),
                                       state_ssm_im[l].reshape(s_groups, s_per, n_state),
                                       ab_re, ab_im, wb, wc, d_row, dseq, s_per)
        y_ssm_s = jnp.swapaxes(y_ssm_s.reshape(s_groups, dseq, s_per, sw), 1, 2).reshape(ms, sw)

        q_t = jnp.swapaxes(q_p.reshape(bsz, nqb, q_block, aw), 2, 3)
        qi_t = jnp.transpose(qi_p.reshape(bsz, nqb, q_block, IDX_HEADS, IDX_DIM), (0, 1, 4, 3, 2))
        qi_t = qi_t.reshape(bsz, nqb, IDX_DIM, IDX_HEADS * q_block)
        w_t = jnp.swapaxes(wi_p.reshape(bsz, nqb, q_block, IDX_HEADS), 2, 3)
        w_t = w_t.reshape(bsz, nqb, 1, IDX_HEADS * q_block)
        v_t = jnp.swapaxes(v_p.reshape(bsz, seq // a_chunk, a_chunk, aw), 2, 3).astype(BF16)
        y_att_p = _prompt_attention(q_t, qi_t, w_t, ki_p.reshape(bsz, nch, chunk, IDX_DIM).astype(BF16),
                                    k_p.reshape(bsz, seq // a_chunk, a_chunk, aw).astype(BF16), v_t,
                                    topk_p, q_block, chunk)

        w_g = w_glu[l].astype(BF16)
        b_g = b_glu[l].reshape(1, sw)
        w_o = w_out[l].astype(BF16)
        xp = _mix(xp, y_ssm_p.reshape(mp, sw), y_att_p.reshape(mp, aw), w_g, b_g, w_o[:sw], w_o[sw:],
                  g_ln[1], b_ln[1], alpha, tm_p)
        xs = _mix(xs, y_ssm_s, y_att_s.reshape(ms, aw), w_g, b_g, w_o[:sw], w_o[sw:],
                  g_ln[1], b_ln[1], alpha, tm_s)
        up2, down2 = ffn2_up[l].astype(BF16), ffn2_down[l].astype(BF16)
        tf2 = _ff_tile(down2.shape[0])
        xp = _ffn(xp, up2, down2, g_ln[2], b_ln[2], alpha, tm_p, tf2)
        xs = _ffn(xs, up2, down2, g_ln[2], b_ln[2], alpha, tm_s, tf2)

        outs["kp"].append(k_p.reshape(bsz, seq, n_heads, HEAD_DIM))
        outs["vp"].append(v_p.reshape(bsz, seq, n_heads, HEAD_DIM))
        outs["kip"].append(ki_p.reshape(bsz, seq, IDX_DIM))
        outs["hrp"].append(hr_p.reshape(bsz, n_groups, SSM_STATE))
        outs["hip"].append(hi_p.reshape(bsz, n_groups, SSM_STATE))
        outs["ks"].append(k_s.reshape(db, dseq, n_heads, HEAD_DIM))
        outs["vs"].append(v_s.reshape(db, dseq, n_heads, HEAD_DIM))
        outs["kis"].append(ki_s)
        outs["hrs"].append(hr_s.reshape(db, n_groups, SSM_STATE))
        outs["his"].append(hi_s.reshape(db, n_groups, SSM_STATE))

    st = lambda name: jnp.stack(outs[name])
    return (xp.reshape(bsz, seq, d_model), xs.reshape(db, dseq, d_model),
            st("kp"), st("vp"), st("kip"), st("hrp"), st("hip"),
            st("ks"), st("vs"), st("kis"), st("hrs"), st("his"))
```

```python
import functools
import math

import jax
import jax.numpy as jnp
from jax import lax
from jax.experimental import pallas as pl
from jax.experimental.pallas import tpu as pltpu

F32 = jnp.float32
BF16 = jnp.bfloat16

SSM_GROUP = 16
SSM_STATE = 64
HEAD_DIM = 64
IDX_HEADS = 8
IDX_DIM = 32
TOPK_MAX = 256
LN_EPS = 1e-5

LANE = 128
SUBLANE = 8
VMEM_LIMIT_BYTES = 56 * 1024 * 1024

NEG_INF = float("-inf")
LOG2_E = 1.4426950408889634
_LOWEST_F32 = -3.4028234663852886e38
KEY_LOWEST = -2139095040
KEY_POS_INF = 0x7F800000


def _cparams(*sem):
    return pltpu.CompilerParams(dimension_semantics=sem, vmem_limit_bytes=VMEM_LIMIT_BYTES)


def _layer_norm(y, g, b):
    mu = jnp.mean(y, axis=-1, keepdims=True)
    yc = y - mu
    var = jnp.mean(yc * yc, axis=-1, keepdims=True)
    return yc * lax.rsqrt(var + LN_EPS) * g + b


def _dot(a, b):
    return jnp.dot(a, b, preferred_element_type=F32)


def _dot_nt(a, b):
    return lax.dot_general(a, b, (((1,), (1,)), ((), ())), preferred_element_type=F32)


def _ffn_kernel(x_ref, wg_ref, wu_ref, wd_ref, g_ref, b_ref, o_ref, xb_ref, acc_ref, *, alpha):
    j = pl.program_id(1)

    @pl.when(j == 0)
    def _():
        xb_ref[...] = x_ref[...].astype(BF16)
        acc_ref[...] = jnp.zeros_like(acc_ref)

    xb = xb_ref[...]
    gate = _dot(xb, wg_ref[...])
    up = _dot(xb, wu_ref[...])
    act = (gate * jax.nn.sigmoid(gate)) * up
    acc_ref[...] += _dot(act.astype(BF16), wd_ref[...])

    @pl.when(j == pl.num_programs(1) - 1)
    def _():
        y = alpha * x_ref[...] + 0.5 * acc_ref[...]
        o_ref[...] = _layer_norm(y, g_ref[...], b_ref[...])


def _ffn(x, w_up, w_down, g, b, alpha, tm, tf):
    m, d = x.shape
    d_ff = w_down.shape[0]
    nf = d_ff // tf
    assert m % tm == 0 and d_ff % tf == 0
    return pl.pallas_call(
        functools.partial(_ffn_kernel, alpha=alpha),
        grid=(m // tm, nf),
        in_specs=[
            pl.BlockSpec((tm, d), lambda i, j: (i, 0)),
            pl.BlockSpec((d, tf), lambda i, j: (0, j)),
            pl.BlockSpec((d, tf), lambda i, j: (0, j + nf)),
            pl.BlockSpec((tf, d), lambda i, j: (j, 0)),
            pl.BlockSpec((1, d), lambda i, j: (0, 0)),
            pl.BlockSpec((1, d), lambda i, j: (0, 0)),
        ],
        out_specs=pl.BlockSpec((tm, d), lambda i, j: (i, 0)),
        out_shape=jax.ShapeDtypeStruct((m, d), F32),
        scratch_shapes=[pltpu.VMEM((tm, d), BF16), pltpu.VMEM((tm, d), F32)],
        compiler_params=_cparams("parallel", "arbitrary"),
        name="swiglu_ln",
    )(x, w_up, w_up, w_down, g, b)


def _inproj_kernel(x_ref, w_ref, *o_refs):
    y = _dot(x_ref[...].astype(BF16), w_ref[...])
    off = 0
    for o_ref in o_refs:
        n = o_ref.shape[1]
        o_ref[...] = y[:, off:off + n]
        off += n


def _inproj(x, w, widths, tm):
    m, d = x.shape
    n = w.shape[1]
    assert sum(widths) == n and m % tm == 0
    return pl.pallas_call(
        _inproj_kernel,
        grid=(m // tm,),
        in_specs=[pl.BlockSpec((tm, d), lambda i: (i, 0)),
                  pl.BlockSpec((d, n), lambda i: (0, 0))],
        out_specs=[pl.BlockSpec((tm, wd), lambda i: (i, 0)) for wd in widths],
        out_shape=[jax.ShapeDtypeStruct((m, wd), F32) for wd in widths],
        compiler_params=_cparams("parallel"),
        name="in_proj",
    )(x, w)


def _inproj_prompt_kernel(x_ref, wn_ref, wt_ref, u_ref, k_ref, v_ref, kiwi_ref, kb_ref, kib_ref,
                          qt_ref, vtb_ref, qit_ref, wt_out_ref, *, sw, aw):
    xb = x_ref[...].astype(BF16)
    y = _dot(xb, wn_ref[...])
    u_ref[...] = y[:, 0:sw]
    k = y[:, sw:sw + aw]
    k_ref[...] = k
    kb_ref[...] = k.astype(BF16)
    v_ref[...] = y[:, sw + aw:sw + 2 * aw]
    kiwi = y[:, sw + 2 * aw:sw + 2 * aw + LANE]
    kiwi_ref[...] = kiwi
    kib_ref[...] = kiwi[:, 0:IDX_DIM].astype(BF16)
    yt = _dot_nt(wt_ref[...], xb)
    n_qi = IDX_HEADS * IDX_DIM
    qt_ref[0, 0] = yt[0:aw]
    vtb_ref[0, 0] = yt[aw:2 * aw].astype(BF16)
    qit_ref[0, 0] = yt[2 * aw:2 * aw + n_qi]
    wt_out_ref[0, 0] = yt[2 * aw + n_qi:2 * aw + n_qi + IDX_HEADS]


def _inproj_prompt(x, w_nat, w_t, bsz, seq, sw, aw, tm):
    m, d = x.shape
    nt = seq // tm
    n_qi = IDX_HEADS * IDX_DIM
    assert m == bsz * seq and seq % tm == 0
    row = lambda width: pl.BlockSpec((tm, width), lambda i: (i, 0))
    tiled = lambda rows: pl.BlockSpec((1, 1, rows, tm), lambda i: (i // nt, i % nt, 0, 0))
    return pl.pallas_call(
        functools.partial(_inproj_prompt_kernel, sw=sw, aw=aw),
        grid=(m // tm,),
        in_specs=[row(d), pl.BlockSpec(w_nat.shape, lambda i: (0, 0)), pl.BlockSpec(w_t.shape, lambda i: (0, 0))],
        out_specs=[row(sw), row(aw), row(aw), row(LANE), row(aw), row(IDX_DIM),
                   tiled(aw), tiled(aw), tiled(n_qi), tiled(IDX_HEADS)],
        out_shape=[jax.ShapeDtypeStruct((m, sw), F32), jax.ShapeDtypeStruct((m, aw), F32),
                   jax.ShapeDtypeStruct((m, aw), F32), jax.ShapeDtypeStruct((m, LANE), F32),
                   jax.ShapeDtypeStruct((m, aw), BF16), jax.ShapeDtypeStruct((m, IDX_DIM), BF16),
                   jax.ShapeDtypeStruct((bsz, nt, aw, tm), F32), jax.ShapeDtypeStruct((bsz, nt, aw, tm), BF16),
                   jax.ShapeDtypeStruct((bsz, nt, n_qi, tm), F32), jax.ShapeDtypeStruct((bsz, nt, IDX_HEADS, tm), F32)],
        compiler_params=_cparams("parallel"),
        name="in_proj_prompt",
    )(x, w_nat, w_t)


def _s5_param_kernel(lre_ref, lim_ref, ls_ref, bre_ref, bim_ref, abre_ref, abim_ref, bbre_ref, bbim_ref):
    lam_re = lre_ref[...]
    lam_im = lim_ref[...]
    step = jnp.exp(ls_ref[...])
    mag = jnp.exp(lam_re * step)
    ang = lam_im * step
    ab_re = mag * jnp.cos(ang)
    ab_im = mag * jnp.sin(ang)
    nr = ab_re - 1.0
    ni = ab_im
    den = lam_re * lam_re + lam_im * lam_im
    f_re = (nr * lam_re + ni * lam_im) / den
    f_im = (ni * lam_re - nr * lam_im) / den
    abre_ref[...] = ab_re
    abim_ref[...] = ab_im
    b_re = bre_ref[...]
    b_im = bim_ref[...]
    bbre_ref[...] = f_re[:, None, :] * b_re - f_im[:, None, :] * b_im
    bbim_ref[...] = f_re[:, None, :] * b_im + f_im[:, None, :] * b_re


def _s5_params(lam_re, lam_im, log_step, b_re, b_im):
    g, p = lam_re.shape
    h = b_re.shape[2]
    b_re_t = jnp.swapaxes(b_re, 1, 2)
    b_im_t = jnp.swapaxes(b_im, 1, 2)
    return pl.pallas_call(
        _s5_param_kernel,
        out_shape=[jax.ShapeDtypeStruct((g, p), F32), jax.ShapeDtypeStruct((g, p), F32),
                   jax.ShapeDtypeStruct((g, h, p), F32), jax.ShapeDtypeStruct((g, h, p), F32)],
        name="s5_params",
    )(lam_re, lam_im, log_step.reshape(g, 1), b_re_t, b_im_t)


def _block_diag_in(bb_t):
    g, h, p = bb_t.shape
    per = LANE // h
    x = bb_t.reshape(g // per, per, h, p)
    eye = jnp.eye(per, dtype=F32)
    return jnp.einsum("jahp,ab->jahbp", x, eye).reshape(g // per, per * h, per * p)


def _block_diag_out(c):
    g, h, p = c.shape
    per = LANE // h
    x = c.reshape(g // per, per, h, p)
    eye = jnp.eye(per, dtype=F32)
    return jnp.einsum("jahp,ab->japbh", x, eye).reshape(g // per, per * p, per * h)


def _re_im_stack(w_re, w_im):
    return jnp.stack([w_re, w_im]).astype(BF16)


def _s5_kernel(u_ref, h0re_ref, h0im_ref, abre_ref, abim_ref, wb_ref, wc_ref, d_ref,
               y_ref, hre_out, him_out, hre_s, him_s, stre_s, stim_s, *, n_steps, rows_per_step, row_group):
    c = pl.program_id(1)
    nb = wb_ref.shape[1]
    sw = wb_ref.shape[3]

    @pl.when(c == 0)
    def _():
        stre_s[...] = h0re_ref[0]
        stim_s[...] = h0im_ref[0]

    u = u_ref[0]
    u_b = u.astype(BF16)
    for j in range(nb):
        uj = u_b[:, j * LANE:(j + 1) * LANE]
        hre_s[:, j * sw:(j + 1) * sw] = _dot(uj, wb_ref[0, j])
        him_s[:, j * sw:(j + 1) * sw] = _dot(uj, wb_ref[1, j])

    a_re = abre_ref[...]
    a_im = abim_ref[...]

    def group_body(rg, carry):
        r0 = pl.multiple_of(rg * row_group, row_group)

        def step_body(t, h):
            h_re, h_im = h
            row = t * rows_per_step + r0
            b_re = hre_s[pl.ds(row, row_group), :]
            b_im = him_s[pl.ds(row, row_group), :]
            n_re = a_re * h_re - a_im * h_im + b_re
            n_im = a_re * h_im + a_im * h_re + b_im
            hre_s[pl.ds(row, row_group), :] = n_re
            him_s[pl.ds(row, row_group), :] = n_im
            return n_re, n_im

        h0 = (stre_s[pl.ds(r0, row_group), :], stim_s[pl.ds(r0, row_group), :])
        h_re, h_im = lax.fori_loop(0, n_steps, step_body, h0)
        stre_s[pl.ds(r0, row_group), :] = h_re
        stim_s[pl.ds(r0, row_group), :] = h_im
        return carry

    lax.fori_loop(0, rows_per_step // row_group, group_body, 0)

    for j in range(nb):
        yj = (_dot(hre_s[:, j * sw:(j + 1) * sw].astype(BF16), wc_ref[0, j])
              - _dot(him_s[:, j * sw:(j + 1) * sw].astype(BF16), wc_ref[1, j]))
        sl = slice(j * LANE, (j + 1) * LANE)
        y_ref[0, :, sl] = yj + d_ref[:, sl] * u[:, sl]

    @pl.when(c == pl.num_programs(1) - 1)
    def _():
        hre_out[0] = stre_s[...]
        him_out[0] = stim_s[...]


def _s5_scan(u, h0_re, h0_im, ab_re, ab_im, wb, wc, d, n_steps, rows_per_step):
    nbatch, rows, w = u.shape
    s = ab_re.shape[1]
    r = rows_per_step
    chunk_rows = n_steps * r
    assert rows % chunk_rows == 0
    row_group = min(r, SUBLANE)
    assert r % row_group == 0
    kern = functools.partial(_s5_kernel, n_steps=n_steps, rows_per_step=r, row_group=row_group)
    const = lambda shape: pl.BlockSpec(shape, lambda n, c: (0,) * len(shape))
    return pl.pallas_call(
        kern,
        grid=(nbatch, rows // chunk_rows),
        in_specs=[
            pl.BlockSpec((1, chunk_rows, w), lambda n, c: (n, c, 0)),
            pl.BlockSpec((1, r, s), lambda n, c: (n, 0, 0)),
            pl.BlockSpec((1, r, s), lambda n, c: (n, 0, 0)),
            const((1, s)), const((1, s)), const(wb.shape), const(wc.shape), const((1, w)),
        ],
        out_specs=[
            pl.BlockSpec((1, chunk_rows, w), lambda n, c: (n, c, 0)),
            pl.BlockSpec((1, r, s), lambda n, c: (n, 0, 0)),
            pl.BlockSpec((1, r, s), lambda n, c: (n, 0, 0)),
        ],
        out_shape=[jax.ShapeDtypeStruct((nbatch, rows, w), F32),
                   jax.ShapeDtypeStruct((nbatch, r, s), F32),
                   jax.ShapeDtypeStruct((nbatch, r, s), F32)],
        scratch_shapes=[pltpu.VMEM((chunk_rows, s), F32), pltpu.VMEM((chunk_rows, s), F32),
                        pltpu.VMEM((r, s), F32), pltpu.VMEM((r, s), F32)],
        compiler_params=_cparams("parallel", "arbitrary"),
        name="s5_scan",
    )(u, h0_re, h0_im, ab_re, ab_im, wb, wc, d)


def _key_to_f32(k):
    bits = jnp.where(k >= 0, k, k ^ jnp.int32(0x7FFFFFFF))
    return lax.bitcast_convert_type(bits, F32)


def _kth_threshold(count_ge, shape, topk):
    kf = float(topk)
    c_zero = count_ge(jnp.zeros(shape, F32))
    c_low = count_ge(jnp.full(shape, _LOWEST_F32, F32))
    pos = c_zero >= kf
    lo = jnp.where(pos, 0, KEY_LOWEST).astype(jnp.int32)
    hi = jnp.where(pos, KEY_POS_INF + 1, 0).astype(jnp.int32)
    cnt_lo = jnp.where(pos, c_zero, c_low)

    def body(_, carry):
        lo, hi, cnt_lo = carry
        mid = lo + ((hi - lo) >> 1)
        cnt = count_ge(_key_to_f32(mid))
        ok = cnt >= kf
        return jnp.where(ok, mid, lo), jnp.where(ok, hi, mid), jnp.where(ok, cnt, cnt_lo)

    lo, hi, cnt_lo = lax.fori_loop(0, 31, body, (lo, hi, cnt_lo))
    return _key_to_f32(lo), cnt_lo


def _tie_index_bound(count_eq_le, count_gt, cnt_ge, shape, topk, n_keys):
    need = float(topk) - count_gt
    tied = cnt_ge > float(topk)
    lo = jnp.full(shape, -1, jnp.int32)
    hi = jnp.full(shape, n_keys - 1, jnp.int32)

    def body(_, carry):
        lo, hi = carry
        mid = lo + ((hi - lo) >> 1)
        ok = count_eq_le(mid) >= need
        return jnp.where(ok, lo, mid), jnp.where(ok, mid, hi)

    n_iter = max(1, math.ceil(math.log2(n_keys + 1)))
    lo, hi = lax.fori_loop(0, n_iter, body, (lo, hi))
    return jnp.where(tied, hi, n_keys)


def _prompt_attn_kernel(qt_ref, qit_ref, w_ref, ki_ref, k_ref, vt_ref, o_ref,
                        s_ref, m_ref, l_ref, acc_ref, *, topk, q_block, chunk, n_heads):
    qb = pl.program_id(1)
    n_keys = k_ref.shape[1] * chunk
    n_chunks = (qb * q_block + q_block + chunk - 1) // chunk
    k_row = lax.broadcasted_iota(jnp.int32, (chunk, q_block), 0)
    q_pos = qb * q_block + lax.broadcasted_iota(jnp.int32, (chunk, q_block), 1)

    qi_rows = qit_ref[0, 0].astype(BF16)
    qit = jnp.concatenate([qi_rows[h * IDX_DIM:(h + 1) * IDX_DIM] for h in range(IDX_HEADS)], axis=1)
    w_rows = w_ref[0, 0]
    w_row = jnp.concatenate([w_rows[h:h + 1] for h in range(IDX_HEADS)], axis=1)

    def score_body(c, carry):
        st = _dot(ki_ref[0, c], qit)
        st = jnp.maximum(st, 0.0) * w_row
        sc = st[:, 0:q_block]
        for h in range(1, IDX_HEADS):
            sc = sc + st[:, h * q_block:(h + 1) * q_block]
        s_ref[c] = jnp.where(c * chunk + k_row <= q_pos, sc, NEG_INF)
        return carry

    lax.fori_loop(0, n_chunks, score_body, 0)

    row = (1, q_block)
    fold = min(chunk, 64)

    def count(pred):
        def body(c, acc):
            hits = jnp.where(pred(c, s_ref[c]), 1.0, 0.0)
            return acc + jnp.sum(hits.reshape(chunk // fold, fold, q_block), axis=0)
        acc = lax.fori_loop(0, n_chunks, body, jnp.zeros((fold, q_block), F32))
        return jnp.sum(acc, axis=0, keepdims=True)

    thr, cnt_ge = _kth_threshold(lambda t: count(lambda c, s: s >= t), row, topk)

    @pl.when(jnp.max(cnt_ge) > float(topk))
    def _():
        cnt_gt = count(lambda c, s: s > thr)
        bound = _tie_index_bound(
            lambda j: count(lambda c, s: (s == thr) & (c * chunk + k_row <= j)),
            cnt_gt, cnt_ge, row, topk, n_keys)

        def drop_body(c, carry):
            s = s_ref[c]
            s_ref[c] = jnp.where((s == thr) & (c * chunk + k_row > bound), NEG_INF, s)
            return carry

        lax.fori_loop(0, n_chunks, drop_body, 0)

    m_ref[...] = jnp.full(m_ref.shape, NEG_INF, F32)
    l_ref[...] = jnp.zeros(l_ref.shape, F32)
    acc_ref[...] = jnp.zeros(acc_ref.shape, F32)
    qt = qt_ref[0, 0] * (HEAD_DIM ** -0.5 * LOG2_E)
    head_of_dim = lax.broadcasted_iota(jnp.int32, (LANE, 2 * q_block), 0) // HEAD_DIM
    head_of_col = lax.broadcasted_iota(jnp.int32, (LANE, 2 * q_block), 1) // q_block
    q_pairs = []
    for j in range(n_heads // 2):
        qj = qt[j * LANE:(j + 1) * LANE, :]
        q_pairs.append(jnp.where(head_of_dim == head_of_col, jnp.concatenate([qj, qj], axis=1), 0.0).astype(BF16))

    def attn_body(c, carry):
        sel = s_ref[c] >= thr
        logits = [_dot(k_ref[0, c, :, j * LANE:(j + 1) * LANE], q_pairs[j]) for j in range(n_heads // 2)]
        for h in range(n_heads):
            j, hh = divmod(h, 2)
            x = jnp.where(sel, logits[j][:, hh * q_block:(hh + 1) * q_block], NEG_INF)
            m_old = m_ref[h]
            m_new = jnp.maximum(m_old, jnp.max(x, axis=0, keepdims=True))
            m_safe = jnp.where(m_new == NEG_INF, 0.0, m_new)
            p = jnp.exp2(x - m_safe)
            scale = jnp.exp2(m_old - m_safe)
            l_ref[h] = scale * l_ref[h] + jnp.sum(p, axis=0, keepdims=True)
            pv = _dot(vt_ref[0, c, h * HEAD_DIM:(h + 1) * HEAD_DIM, :], p.astype(BF16))
            acc_ref[h] = scale * acc_ref[h] + pv
            m_ref[h] = m_new
        return carry

    lax.fori_loop(0, n_chunks, attn_body, 0)

    out_t = jnp.concatenate([acc_ref[h] / l_ref[h] for h in range(n_heads)], axis=0)
    o_ref[0] = out_t.T


def _prompt_attention(q_t, qi_t, w_t, ki, k, v_t, topk, q_block, chunk):
    b, ntq, aw, tq = q_t.shape
    nch = k.shape[1]
    l = nch * chunk
    nqb = l // q_block
    per_tile = tq // q_block
    n_heads = aw // HEAD_DIM
    assert n_heads % 2 == 0 and ntq * tq == l and tq % q_block == 0 and chunk % q_block == 0
    assert 2 * HEAD_DIM == LANE
    kern = functools.partial(_prompt_attn_kernel, topk=topk, q_block=q_block, chunk=chunk, n_heads=n_heads)
    q_side = lambda rows: pl.BlockSpec((1, 1, rows, q_block), lambda i, j: (i, j // per_tile, 0, j % per_tile))
    return pl.pallas_call(
        kern,
        grid=(b, nqb),
        in_specs=[
            q_side(aw), q_side(IDX_HEADS * IDX_DIM), q_side(IDX_HEADS),
            pl.BlockSpec((1, nch, chunk, IDX_DIM), lambda i, j: (i, 0, 0, 0)),
            pl.BlockSpec((1, nch, chunk, aw), lambda i, j: (i, 0, 0, 0)),
            pl.BlockSpec((1, nch, aw, chunk), lambda i, j: (i, 0, 0, 0)),
        ],
        out_specs=pl.BlockSpec((1, q_block, aw), lambda i, j: (i, j, 0)),
        out_shape=jax.ShapeDtypeStruct((b, l, aw), F32),
        scratch_shapes=[pltpu.VMEM((nch, chunk, q_block), F32),
                        pltpu.VMEM((n_heads, 1, q_block), F32),
                        pltpu.VMEM((n_heads, 1, q_block), F32),
                        pltpu.VMEM((n_heads, HEAD_DIM, q_block), F32)],
        compiler_params=_cparams("parallel", "arbitrary"),
        name="prompt_attention",
    )(q_t, qi_t, w_t, ki, k, v_t)


def _sample_select_kernel(pt_ref, qi_ref, w_ref, knew_ref, cache_ref, bias_ref, kbuf_ref, s_ref, sem,
                          *, topk, layer, group):
    step = pl.program_id(0)
    n_pages, page = kbuf_ref.shape[1] - 1, kbuf_ref.shape[3]
    n_q = knew_ref.shape[2]
    n_keys = (n_pages + 1) * page

    def page_copy(g, i):
        return pltpu.make_async_copy(cache_ref.at[layer, pt_ref[step * group + g, i]], kbuf_ref.at[g, i], sem.at[0])

    def start_body(g, carry):
        for i in range(n_pages):
            page_copy(g, i).start()
        return carry

    lax.fori_loop(0, group, start_body, 0)

    for g in range(group):
        kbuf_ref[g, n_pages] = jnp.zeros((IDX_DIM, page), F32)
        kbuf_ref[g, n_pages, :, 0:n_q] = knew_ref[g]

    def wait_body(g, carry):
        for i in range(n_pages):
            page_copy(g, i).wait()
        return carry

    lax.fori_loop(0, group, wait_body, 0)

    for g in range(group):
        qi = qi_ref[g].astype(BF16)
        w_col = w_ref[g]
        for i in range(n_pages + 1):
            st = _dot(qi, kbuf_ref[g, i].astype(BF16))
            st = jnp.maximum(st, 0.0) * w_col
            s_ref[g * n_q:(g + 1) * n_q, i * page:(i + 1) * page] = jnp.sum(st.reshape(IDX_HEADS, n_q, page), axis=0)

    rows = group * n_q
    q_of_row = lax.broadcasted_iota(jnp.int32, (rows, n_keys), 0) % n_q
    key_idx = lax.broadcasted_iota(jnp.int32, (rows, n_keys), 1)
    s_ref[...] = jnp.where(key_idx - n_pages * page <= q_of_row, s_ref[...], NEG_INF)

    col = (rows, 1)

    def count(pred):
        hits = jnp.where(pred(s_ref[...]), 1.0, 0.0)
        part = hits[:, 0:LANE]
        for t in range(1, n_keys // LANE):
            part = part + hits[:, t * LANE:(t + 1) * LANE]
        return jnp.sum(part, axis=1, keepdims=True)

    thr, cnt_ge = _kth_threshold(lambda t: count(lambda s: s >= t), col, topk)

    @pl.when(jnp.max(cnt_ge) > float(topk))
    def _():
        cnt_gt = count(lambda s: s > thr)
        bound = _tie_index_bound(lambda j: count(lambda s: (s == thr) & (key_idx <= j)),
                                 cnt_gt, cnt_ge, col, topk, n_keys)
        s = s_ref[...]
        s_ref[...] = jnp.where((s == thr) & (key_idx > bound), NEG_INF, s)

    bias_ref[...] = jnp.where(s_ref[...] >= thr, 0.0, NEG_INF).reshape(group, n_q, n_keys)


def _sample_select(page_table, qi_hm, w_col, cache_kidx_t, ki_new_t, layer, topk, group):
    db, n_pages = page_table.shape
    page = cache_kidx_t.shape[3]
    n_q = ki_new_t.shape[2]
    rows = IDX_HEADS * n_q
    n_keys = (n_pages + 1) * page
    assert db % group == 0 and n_q <= page
    kern = functools.partial(_sample_select_kernel, topk=topk, layer=layer, group=group)
    grid_spec = pltpu.PrefetchScalarGridSpec(
        num_scalar_prefetch=1,
        grid=(db // group,),
        in_specs=[
            pl.BlockSpec((group, rows, IDX_DIM), lambda s, pt: (s, 0, 0)),
            pl.BlockSpec((group, rows, 1), lambda s, pt: (s, 0, 0)),
            pl.BlockSpec((group, IDX_DIM, n_q), lambda s, pt: (s, 0, 0)),
            pl.BlockSpec(memory_space=pl.ANY),
        ],
        out_specs=pl.BlockSpec((group, n_q, n_keys), lambda s, pt: (s, 0, 0)),
        scratch_shapes=[pltpu.VMEM((group, n_pages + 1, IDX_DIM, page), F32),
                        pltpu.VMEM((group * n_q, n_keys), F32),
                        pltpu.SemaphoreType.DMA((1,))],
    )
    return pl.pallas_call(
        kern,
        grid_spec=grid_spec,
        out_shape=jax.ShapeDtypeStruct((db, n_q, n_keys), F32),
        compiler_params=_cparams("arbitrary"),
        name="sample_select",
    )(page_table, qi_hm, w_col, ki_new_t, cache_kidx_t)


def _sample_attn_kernel(pt_ref, q_ref, bias_ref, knew_ref, vnew_ref, kcache_ref, vcache_ref, o_ref,
                        kbuf_ref, vbuf_ref, lg_ref, kpad_ref, vpad_ref, sem, *, layer, n_heads):
    b = pl.program_id(0)
    n_seq = pl.num_programs(0)
    n_pages, page = kbuf_ref.shape[1], kbuf_ref.shape[4]
    n_q, aw = q_ref.shape[1], q_ref.shape[2]
    past = n_pages * page
    rows = n_heads * n_q
    slot = b % 2

    def page_copies(seq, i, to_slot):
        idx = pt_ref[seq, i]
        return (pltpu.make_async_copy(kcache_ref.at[layer, idx], kbuf_ref.at[to_slot, i], sem.at[to_slot, 0]),
                pltpu.make_async_copy(vcache_ref.at[layer, idx], vbuf_ref.at[to_slot, i], sem.at[to_slot, 1]))

    def start_fetch(seq, to_slot):
        for i in range(n_pages):
            for cp in page_copies(seq, i, to_slot):
                cp.start()

    @pl.when(b == 0)
    def _():
        start_fetch(0, 0)

    @pl.when(b + 1 < n_seq)
    def _():
        start_fetch(b + 1, 1 - slot)

    kpad_ref[...] = jnp.zeros(kpad_ref.shape, F32)
    vpad_ref[...] = jnp.zeros(vpad_ref.shape, F32)
    kpad_ref[0:n_q, :] = knew_ref[0]
    vpad_ref[0:n_q, :] = vnew_ref[0]

    q = q_ref[0] * (HEAD_DIM ** -0.5 * LOG2_E)
    head_of_row = lax.broadcasted_iota(jnp.int32, (rows, aw), 0) // n_q
    head_of_lane = lax.broadcasted_iota(jnp.int32, (rows, aw), 1) // HEAD_DIM
    own_head = head_of_row == head_of_lane
    q_diag = jnp.where(own_head, jnp.concatenate([q] * n_heads, axis=0), 0.0).astype(BF16)
    lg_ref[:, past:past + page] = _dot_nt(q_diag, kpad_ref[...].astype(BF16))

    for i in range(n_pages):
        for cp in page_copies(b, i, slot):
            cp.wait()

    for i in range(n_pages):
        k_t = kbuf_ref[slot, i].reshape(aw, page).astype(BF16)
        lg_ref[:, i * page:(i + 1) * page] = _dot(q_diag, k_t)

    logits = lg_ref[...] + jnp.concatenate([bias_ref[0]] * n_heads, axis=0)
    m = jnp.max(logits, axis=1, keepdims=True)
    pr = jnp.exp2(logits - m)
    denom = jnp.sum(pr, axis=1, keepdims=True)
    pr = pr.astype(BF16)

    acc = _dot(pr[:, past:past + page], vpad_ref[...].astype(BF16))
    for i in range(n_pages):
        v_t = vbuf_ref[slot, i].reshape(aw, page).astype(BF16)
        acc = acc + _dot_nt(pr[:, i * page:(i + 1) * page], v_t)
    out = jnp.where(own_head, acc / denom, 0.0)
    o_ref[0] = jnp.sum(out.reshape(n_heads, n_q, aw), axis=0)


def _sample_attention(page_table, q, bias, cache_k_t, cache_v_t, k_new, v_new, layer):
    db, n_pages = page_table.shape
    _, n_q, aw = q.shape
    n_heads = aw // HEAD_DIM
    page = cache_k_t.shape[4]
    n_keys = (n_pages + 1) * page
    rows = n_heads * n_q
    kern = functools.partial(_sample_attn_kernel, layer=layer, n_heads=n_heads)
    per_b = lambda shape: pl.BlockSpec(shape, lambda b, pt: (b,) + (0,) * (len(shape) - 1))
    grid_spec = pltpu.PrefetchScalarGridSpec(
        num_scalar_prefetch=1,
        grid=(db,),
        in_specs=[per_b((1, n_q, aw)), per_b((1, n_q, n_keys)), per_b((1, n_q, aw)), per_b((1, n_q, aw)),
                  pl.BlockSpec(memory_space=pl.ANY), pl.BlockSpec(memory_space=pl.ANY)],
        out_specs=per_b((1, n_q, aw)),
        scratch_shapes=[pltpu.VMEM((2, n_pages, n_heads, HEAD_DIM, page), F32),
                        pltpu.VMEM((2, n_pages, n_heads, HEAD_DIM, page), F32),
                        pltpu.VMEM((rows, n_keys), F32),
                        pltpu.VMEM((page, aw), F32), pltpu.VMEM((page, aw), F32),
                        pltpu.SemaphoreType.DMA((2, 2))],
    )
    return pl.pallas_call(
        kern,
        grid_spec=grid_spec,
        out_shape=jax.ShapeDtypeStruct((db, n_q, aw), F32),
        compiler_params=_cparams("arbitrary"),
        name="sample_attention",
    )(page_table, q, bias, k_new, v_new, cache_k_t, cache_v_t)


def _mix_kernel(x_ref, ys_ref, ya_ref, wglu_ref, bglu_ref, wos_ref, woa_ref, g_ref, b_ref, o_ref, *, alpha):
    y = ys_ref[...]
    gel = 0.5 * y * (1.0 + jnp.tanh(math.sqrt(2.0 / math.pi) * (y + 0.044715 * (y * y * y))))
    gate = jax.nn.sigmoid(_dot(gel.astype(BF16), wglu_ref[...]) + bglu_ref[...])
    y_ssm = gel * gate
    mix = _dot(y_ssm.astype(BF16), wos_ref[...]) + _dot(ya_ref[...].astype(BF16), woa_ref[...])
    o_ref[...] = _layer_norm(alpha * x_ref[...] + mix, g_ref[...], b_ref[...])


def _mix(x, y_ssm, y_att, w_glu, b_glu, w_out_ssm, w_out_att, g, b, alpha, tm):
    m, d = x.shape
    sw = y_ssm.shape[1]
    aw = y_att.shape[1]
    const = lambda shape: pl.BlockSpec(shape, lambda i: (0, 0))
    return pl.pallas_call(
        functools.partial(_mix_kernel, alpha=alpha),
        grid=(m // tm,),
        in_specs=[pl.BlockSpec((tm, d), lambda i: (i, 0)),
                  pl.BlockSpec((tm, sw), lambda i: (i, 0)),
                  pl.BlockSpec((tm, aw), lambda i: (i, 0)),
                  const((sw, sw)), const((1, sw)), const((sw, d)), const((aw, d)), const((1, d)), const((1, d))],
        out_specs=pl.BlockSpec((tm, d), lambda i: (i, 0)),
        out_shape=jax.ShapeDtypeStruct((m, d), F32),
        compiler_params=_cparams("parallel"),
        name="glu_outproj_ln",
    )(x, y_ssm, y_att, w_glu, b_glu, w_out_ssm, w_out_att, g, b)


def _row_tile(m, cap=512):
    t = cap
    while m % t:
        t //= 2
    return t


def _ff_tile(d_ff, cap=1536):
    best = LANE
    for t in range(LANE, cap + 1, LANE):
        if d_ff % t == 0:
            best = t
    return best


def _head_major(x, n_blocks, block, n_heads, dim):
    b = x.shape[0]
    x = x.reshape(b, n_blocks, block, n_heads, dim)
    return jnp.swapaxes(x, 2, 3).reshape(b, n_blocks, n_heads * block, dim)


def kernel(x_prompt, x_sample, cache_k, cache_v, cache_kidx, state_ssm_re, state_ssm_im, page_table,
           w_in, ssm_lambda_re, ssm_lambda_im, ssm_log_step, ssm_b_re, ssm_b_im, ssm_c_re, ssm_c_im,
           ssm_d, w_glu, b_glu, w_out, ffn1_up, ffn1_down, ffn2_up, ffn2_down, ln_g, ln_b):
    bsz, seq, d_model = x_prompt.shape
    db, dseq, _ = x_sample.shape
    depth = w_in.shape[0]
    sw = d_model // 2
    aw = d_model - sw
    n_groups = sw // SSM_GROUP
    n_state = n_groups * SSM_STATE
    n_heads = aw // HEAD_DIM
    page = cache_k.shape[2]
    n_pages = page_table.shape[1]
    past = n_pages * page
    alpha = (2.0 * depth) ** 0.25
    mp = bsz * seq
    ms = db * dseq
    tm_p = _row_tile(mp)
    tm_s = _row_tile(ms)
    q_block = min(128, seq)
    chunk = tm_p
    nch = seq // chunk
    topk_p = min(TOPK_MAX, seq // 4)
    topk_s = min(TOPK_MAX, (past + dseq) // 4)
    s5_rows = min(256, seq)
    s_groups = max(1, ms // 256)
    s_per = db // s_groups
    select_group = math.gcd(db, SUBLANE)
    cache_k_t = jnp.transpose(cache_k, (0, 1, 3, 4, 2))
    cache_v_t = jnp.transpose(cache_v, (0, 1, 3, 4, 2))
    cache_kidx_t = jnp.swapaxes(cache_kidx, 2, 3)
    pad_w = LANE - IDX_DIM - IDX_HEADS
    widths = (sw, aw, aw, aw, IDX_HEADS * IDX_DIM, LANE)

    xp = x_prompt.reshape(mp, d_model)
    xs = x_sample.reshape(ms, d_model)
    outs = {name: [] for name in ("kp", "vp", "kip", "hrp", "hip", "ks", "vs", "kis", "hrs", "his")}

    for l in range(depth):
        g_ln = ln_g[l][:, None, :]
        b_ln = ln_b[l][:, None, :]
        up1, down1 = ffn1_up[l].astype(BF16), ffn1_down[l].astype(BF16)
        tf1 = _ff_tile(down1.shape[0])
        xp = _ffn(xp, up1, down1, g_ln[0], b_ln[0], alpha, tm_p, tf1)
        xs = _ffn(xs, up1, down1, g_ln[0], b_ln[0], alpha, tm_s, tf1)

        w_l = jnp.pad(w_in[l], ((0, 0), (0, pad_w))).astype(BF16)
        o_q, o_k, o_v, o_qi = sw, sw + aw, sw + 2 * aw, sw + 3 * aw
        o_ki = o_qi + IDX_HEADS * IDX_DIM
        w_nat = jnp.concatenate([w_l[:, 0:o_q], w_l[:, o_k:o_qi], w_l[:, o_ki:]], axis=1)
        w_t = jnp.concatenate([w_l[:, o_q:o_k], w_l[:, o_v:o_ki], w_l[:, o_ki + IDX_DIM:o_ki + IDX_DIM + 16]], axis=1).T
        (u_p, k_p, v_p, kiwi_p, k_pb, ki_pb, q_t, v_tb, qi_t, w_t_p) = _inproj_prompt(
            xp, w_nat, w_t, bsz, seq, sw, aw, tm_p)
        u_s, q_s, k_s, v_s, qi_s, kiwi_s = _inproj(xs, w_l, widths, tm_s)
        ki_p = kiwi_p[:, :IDX_DIM]
        ki_s, wi_s = kiwi_s[:, :IDX_DIM], kiwi_s[:, IDX_DIM:IDX_DIM + IDX_HEADS]

        q_s = q_s.reshape(db, dseq, aw)
        k_s = k_s.reshape(db, dseq, aw)
        v_s = v_s.reshape(db, dseq, aw)
        ki_s = ki_s.reshape(db, dseq, IDX_DIM)
        qi_hm = _head_major(qi_s.reshape(db, dseq, IDX_HEADS * IDX_DIM), 1, dseq, IDX_HEADS, IDX_DIM)[:, 0]
        w_hm = _head_major(wi_s.reshape(db, dseq, IDX_HEADS), 1, dseq, IDX_HEADS, 1)[:, 0]
        bias = _sample_select(page_table, qi_hm, w_hm, cache_kidx_t, jnp.swapaxes(ki_s, 1, 2), l, topk_s, select_group)
        y_att_s = _sample_attention(page_table, q_s, bias, cache_k_t, cache_v_t, k_s, v_s, l)

        ab_re, ab_im, bb_re_t, bb_im_t = _s5_params(ssm_lambda_re[l], ssm_lambda_im[l], ssm_log_step[l],
                                                     ssm_b_re[l], ssm_b_im[l])
        wb = _re_im_stack(_block_diag_in(bb_re_t), _block_diag_in(bb_im_t))
        wc = _re_im_stack(_block_diag_out(ssm_c_re[l]), _block_diag_out(ssm_c_im[l]))
        ab_re = ab_re.reshape(1, n_state)
        ab_im = ab_im.reshape(1, n_state)
        d_row = ssm_d[l].reshape(1, sw)

        zeros_state = jnp.zeros((bsz, 1, n_state), F32)
        y_ssm_p, hr_p, hi_p = _s5_scan(u_p.reshape(bsz, seq, sw), zeros_state, zeros_state,
                                       ab_re, ab_im, wb, wc, d_row, s5_rows, 1)
        u_s = jnp.swapaxes(u_s.reshape(s_groups, s_per, dseq, sw), 1, 2)
        y_ssm_s, hr_s, hi_s = _s5_scan(u_s.reshape(s_groups, dseq * s_per, sw),
                                       state_ssm_re[l].reshape(s_groups, s_per, n_state),
                                       state_ssm_im[l].reshape(s_groups, s_per, n_state),
                                       ab_re, ab_im, wb, wc, d_row, dseq, s_per)
        y_ssm_s = jnp.swapaxes(y_ssm_s.reshape(s_groups, dseq, s_per, sw), 1, 2).reshape(ms, sw)

        y_att_p = _prompt_attention(q_t, qi_t, w_t_p, ki_pb.reshape(bsz, nch, chunk, IDX_DIM),
                                    k_pb.reshape(bsz, nch, chunk, aw), v_tb, topk_p, q_block, chunk)

        w_g = w_glu[l].astype(BF16)
        b_g = b_glu[l].reshape(1, sw)
        w_o = w_out[l].astype(BF16)
        xp = _mix(xp, y_ssm_p.reshape(mp, sw), y_att_p.reshape(mp, aw), w_g, b_g, w_o[:sw], w_o[sw:],
                  g_ln[1], b_ln[1], alpha, tm_p)
        xs = _mix(xs, y_ssm_s, y_att_s.reshape(ms, aw), w_g, b_g, w_o[:sw], w_o[sw:],
                  g_ln[1], b_ln[1], alpha, tm_s)
        up2, down2 = ffn2_up[l].astype(BF16), ffn2_down[l].astype(BF16)
        tf2 = _ff_tile(down2.shape[0])
        xp = _ffn(xp, up2, down2, g_ln[2], b_ln[2], alpha, tm_p, tf2)
        xs = _ffn(xs, up2, down2, g_ln[2], b_ln[2], alpha, tm_s, tf2)

        outs["kp"].append(k_p.reshape(bsz, seq, n_heads, HEAD_DIM))
        outs["vp"].append(v_p.reshape(bsz, seq, n_heads, HEAD_DIM))
        outs["kip"].append(ki_p.reshape(bsz, seq, IDX_DIM))
        outs["hrp"].append(hr_p.reshape(bsz, n_groups, SSM_STATE))
        outs["hip"].append(hi_p.reshape(bsz, n_groups, SSM_STATE))
        outs["ks"].append(k_s.reshape(db, dseq, n_heads, HEAD_DIM))
        outs["vs"].append(v_s.reshape(db, dseq, n_heads, HEAD_DIM))
        outs["kis"].append(ki_s)
        outs["hrs"].append(hr_s.reshape(db, n_groups, SSM_STATE))
        outs["his"].append(hi_s.reshape(db, n_groups, SSM_STATE))

    st = lambda name: jnp.stack(outs[name])
    return (xp.reshape(bsz, seq, d_model), xs.reshape(db, dseq, d_model),
            st("kp"), st("vp"), st("kip"), st("hrp"), st("hip"),
            st("ks"), st("vs"), st("kis"), st("hrs"), st("his"))
```

```python
import functools
import math

import jax
import jax.numpy as jnp
from jax import lax
from jax.experimental import pallas as pl
from jax.experimental.pallas import tpu as pltpu

F32 = jnp.float32
BF16 = jnp.bfloat16

SSM_GROUP = 16
SSM_STATE = 64
HEAD_DIM = 64
IDX_HEADS = 8
IDX_DIM = 32
TOPK_MAX = 256
LN_EPS = 1e-5

LANE = 128
SUBLANE = 8
VMEM_LIMIT_BYTES = 56 * 1024 * 1024

NEG_INF = float("-inf")
LOG2_E = 1.4426950408889634
_LOWEST_F32 = -3.4028234663852886e38
KEY_LOWEST = -2139095040
KEY_POS_INF = 0x7F800000


def _cparams(*sem):
    return pltpu.CompilerParams(dimension_semantics=sem, vmem_limit_bytes=VMEM_LIMIT_BYTES)


def _layer_norm(y, g, b):
    mu = jnp.mean(y, axis=-1, keepdims=True)
    yc = y - mu
    var = jnp.mean(yc * yc, axis=-1, keepdims=True)
    return yc * lax.rsqrt(var + LN_EPS) * g + b


def _dot(a, b):
    return jnp.dot(a, b, preferred_element_type=F32)


def _dot_nt(a, b):
    return lax.dot_general(a, b, (((1,), (1,)), ((), ())), preferred_element_type=F32)


def _ffn_kernel(x_ref, wg_ref, wu_ref, wd_ref, g_ref, b_ref, o_ref, xb_ref, acc_ref, *, alpha):
    j = pl.program_id(1)

    @pl.when(j == 0)
    def _():
        xb_ref[...] = x_ref[...].astype(BF16)
        acc_ref[...] = jnp.zeros_like(acc_ref)

    xb = xb_ref[...]
    gate = _dot(xb, wg_ref[...])
    up = _dot(xb, wu_ref[...])
    act = (gate * jax.nn.sigmoid(gate)) * up
    acc_ref[...] += _dot(act.astype(BF16), wd_ref[...])

    @pl.when(j == pl.num_programs(1) - 1)
    def _():
        y = alpha * x_ref[...] + 0.5 * acc_ref[...]
        o_ref[...] = _layer_norm(y, g_ref[...], b_ref[...])


def _ffn(x, w_up, w_down, g, b, alpha, tm, tf):
    m, d = x.shape
    d_ff = w_down.shape[0]
    nf = d_ff // tf
    assert m % tm == 0 and d_ff % tf == 0
    return pl.pallas_call(
        functools.partial(_ffn_kernel, alpha=alpha),
        grid=(m // tm, nf),
        in_specs=[
            pl.BlockSpec((tm, d), lambda i, j: (i, 0)),
            pl.BlockSpec((d, tf), lambda i, j: (0, j)),
            pl.BlockSpec((d, tf), lambda i, j: (0, j + nf)),
            pl.BlockSpec((tf, d), lambda i, j: (j, 0)),
            pl.BlockSpec((1, d), lambda i, j: (0, 0)),
            pl.BlockSpec((1, d), lambda i, j: (0, 0)),
        ],
        out_specs=pl.BlockSpec((tm, d), lambda i, j: (i, 0)),
        out_shape=jax.ShapeDtypeStruct((m, d), F32),
        scratch_shapes=[pltpu.VMEM((tm, d), BF16), pltpu.VMEM((tm, d), F32)],
        compiler_params=_cparams("parallel", "arbitrary"),
        name="swiglu_ln",
    )(x, w_up, w_up, w_down, g, b)


def _inproj_kernel(x_ref, w_ref, *o_refs):
    y = _dot(x_ref[...].astype(BF16), w_ref[...])
    off = 0
    for o_ref in o_refs:
        n = o_ref.shape[1]
        o_ref[...] = y[:, off:off + n]
        off += n


def _inproj(x, w, widths, tm):
    m, d = x.shape
    n = w.shape[1]
    assert sum(widths) == n and m % tm == 0
    return pl.pallas_call(
        _inproj_kernel,
        grid=(m // tm,),
        in_specs=[pl.BlockSpec((tm, d), lambda i: (i, 0)),
                  pl.BlockSpec((d, n), lambda i: (0, 0))],
        out_specs=[pl.BlockSpec((tm, wd), lambda i: (i, 0)) for wd in widths],
        out_shape=[jax.ShapeDtypeStruct((m, wd), F32) for wd in widths],
        compiler_params=_cparams("parallel"),
        name="in_proj",
    )(x, w)


def _inproj_prompt_kernel(x_ref, wn_ref, wt_ref, u_ref, kb_ref, kib_ref,
                          qt_ref, vtb_ref, qit_ref, wt_out_ref, kt_ref, vt_ref, kit_ref, *, sw, aw):
    xb = x_ref[...].astype(BF16)
    y = _dot(xb, wn_ref[...])
    u_ref[...] = y[:, 0:sw]
    kb_ref[...] = y[:, sw:sw + aw].astype(BF16)
    kib_ref[...] = y[:, sw + aw:sw + aw + IDX_DIM].astype(BF16)
    yt = _dot_nt(wt_ref[...], xb)
    n_qi = IDX_HEADS * IDX_DIM
    o_k, o_v, o_qi, o_ki, o_wi = aw, 2 * aw, 3 * aw, 3 * aw + n_qi, 3 * aw + n_qi + IDX_DIM
    qt_ref[0, 0] = yt[0:o_k]
    kt_ref[0] = yt[o_k:o_v]
    v_t = yt[o_v:o_qi]
    vt_ref[0] = v_t
    vtb_ref[0, 0] = v_t.astype(BF16)
    qit_ref[0, 0] = yt[o_qi:o_ki]
    kit_ref[0] = yt[o_ki:o_wi]
    wt_out_ref[0, 0] = yt[o_wi:o_wi + IDX_HEADS]


def _inproj_prompt(x, w_nat, w_t, bsz, seq, sw, aw, tm):
    m, d = x.shape
    nt = seq // tm
    n_qi = IDX_HEADS * IDX_DIM
    assert m == bsz * seq and seq % tm == 0
    row = lambda width: pl.BlockSpec((tm, width), lambda i: (i, 0))
    tiled = lambda rows: pl.BlockSpec((1, 1, rows, tm), lambda i: (i // nt, i % nt, 0, 0))
    whole = lambda rows: pl.BlockSpec((1, rows, tm), lambda i: (i // nt, 0, i % nt))
    return pl.pallas_call(
        functools.partial(_inproj_prompt_kernel, sw=sw, aw=aw),
        grid=(m // tm,),
        in_specs=[row(d), pl.BlockSpec(w_nat.shape, lambda i: (0, 0)), pl.BlockSpec(w_t.shape, lambda i: (0, 0))],
        out_specs=[row(sw), row(aw), row(IDX_DIM),
                   tiled(aw), tiled(aw), tiled(n_qi), tiled(IDX_HEADS),
                   whole(aw), whole(aw), whole(IDX_DIM)],
        out_shape=[jax.ShapeDtypeStruct((m, sw), F32),
                   jax.ShapeDtypeStruct((m, aw), BF16), jax.ShapeDtypeStruct((m, IDX_DIM), BF16),
                   jax.ShapeDtypeStruct((bsz, nt, aw, tm), F32), jax.ShapeDtypeStruct((bsz, nt, aw, tm), BF16),
                   jax.ShapeDtypeStruct((bsz, nt, n_qi, tm), F32), jax.ShapeDtypeStruct((bsz, nt, IDX_HEADS, tm), F32),
                   jax.ShapeDtypeStruct((bsz, aw, seq), F32), jax.ShapeDtypeStruct((bsz, aw, seq), F32),
                   jax.ShapeDtypeStruct((bsz, IDX_DIM, seq), F32)],
        compiler_params=_cparams("parallel"),
        name="in_proj_prompt",
    )(x, w_nat, w_t)


def _s5_param_kernel(lre_ref, lim_ref, ls_ref, bre_ref, bim_ref, abre_ref, abim_ref, bbre_ref, bbim_ref):
    lam_re = lre_ref[...]
    lam_im = lim_ref[...]
    step = jnp.exp(ls_ref[...])
    mag = jnp.exp(lam_re * step)
    ang = lam_im * step
    ab_re = mag * jnp.cos(ang)
    ab_im = mag * jnp.sin(ang)
    nr = ab_re - 1.0
    ni = ab_im
    den = lam_re * lam_re + lam_im * lam_im
    f_re = (nr * lam_re + ni * lam_im) / den
    f_im = (ni * lam_re - nr * lam_im) / den
    abre_ref[...] = ab_re
    abim_ref[...] = ab_im
    b_re = bre_ref[...]
    b_im = bim_ref[...]
    bbre_ref[...] = f_re[:, None, :] * b_re - f_im[:, None, :] * b_im
    bbim_ref[...] = f_re[:, None, :] * b_im + f_im[:, None, :] * b_re


def _s5_params(lam_re, lam_im, log_step, b_re, b_im):
    g, p = lam_re.shape
    h = b_re.shape[2]
    b_re_t = jnp.swapaxes(b_re, 1, 2)
    b_im_t = jnp.swapaxes(b_im, 1, 2)
    return pl.pallas_call(
        _s5_param_kernel,
        out_shape=[jax.ShapeDtypeStruct((g, p), F32), jax.ShapeDtypeStruct((g, p), F32),
                   jax.ShapeDtypeStruct((g, h, p), F32), jax.ShapeDtypeStruct((g, h, p), F32)],
        name="s5_params",
    )(lam_re, lam_im, log_step.reshape(g, 1), b_re_t, b_im_t)


def _block_diag_in(bb_t):
    g, h, p = bb_t.shape
    per = LANE // h
    x = bb_t.reshape(g // per, per, h, p)
    eye = jnp.eye(per, dtype=F32)
    return jnp.einsum("jahp,ab->jahbp", x, eye).reshape(g // per, per * h, per * p)


def _block_diag_out(c):
    g, h, p = c.shape
    per = LANE // h
    x = c.reshape(g // per, per, h, p)
    eye = jnp.eye(per, dtype=F32)
    return jnp.einsum("jahp,ab->japbh", x, eye).reshape(g // per, per * p, per * h)


def _re_im_stack(w_re, w_im):
    return jnp.stack([w_re, w_im]).astype(BF16)


def _s5_kernel(u_ref, h0re_ref, h0im_ref, abre_ref, abim_ref, wb_ref, wc_ref, d_ref,
               y_ref, hre_out, him_out, hre_s, him_s, stre_s, stim_s, *, n_steps, rows_per_step, row_group):
    c = pl.program_id(1)
    nb = wb_ref.shape[1]
    sw = wb_ref.shape[3]

    @pl.when(c == 0)
    def _():
        stre_s[...] = h0re_ref[0]
        stim_s[...] = h0im_ref[0]

    u = u_ref[0]
    u_b = u.astype(BF16)
    for j in range(nb):
        uj = u_b[:, j * LANE:(j + 1) * LANE]
        hre_s[:, j * sw:(j + 1) * sw] = _dot(uj, wb_ref[0, j])
        him_s[:, j * sw:(j + 1) * sw] = _dot(uj, wb_ref[1, j])

    a_re = abre_ref[...]
    a_im = abim_ref[...]

    def group_body(rg, carry):
        r0 = pl.multiple_of(rg * row_group, row_group)

        def step_body(t, h):
            h_re, h_im = h
            row = t * rows_per_step + r0
            b_re = hre_s[pl.ds(row, row_group), :]
            b_im = him_s[pl.ds(row, row_group), :]
            n_re = a_re * h_re - a_im * h_im + b_re
            n_im = a_re * h_im + a_im * h_re + b_im
            hre_s[pl.ds(row, row_group), :] = n_re
            him_s[pl.ds(row, row_group), :] = n_im
            return n_re, n_im

        h0 = (stre_s[pl.ds(r0, row_group), :], stim_s[pl.ds(r0, row_group), :])
        h_re, h_im = lax.fori_loop(0, n_steps, step_body, h0)
        stre_s[pl.ds(r0, row_group), :] = h_re
        stim_s[pl.ds(r0, row_group), :] = h_im
        return carry

    lax.fori_loop(0, rows_per_step // row_group, group_body, 0)

    for j in range(nb):
        yj = (_dot(hre_s[:, j * sw:(j + 1) * sw].astype(BF16), wc_ref[0, j])
              - _dot(him_s[:, j * sw:(j + 1) * sw].astype(BF16), wc_ref[1, j]))
        sl = slice(j * LANE, (j + 1) * LANE)
        y_ref[0, :, sl] = yj + d_ref[:, sl] * u[:, sl]

    @pl.when(c == pl.num_programs(1) - 1)
    def _():
        hre_out[0] = stre_s[...]
        him_out[0] = stim_s[...]


def _s5_scan(u, h0_re, h0_im, ab_re, ab_im, wb, wc, d, n_steps, rows_per_step):
    nbatch, rows, w = u.shape
    s = ab_re.shape[1]
    r = rows_per_step
    chunk_rows = n_steps * r
    assert rows % chunk_rows == 0
    row_group = min(r, SUBLANE)
    assert r % row_group == 0
    kern = functools.partial(_s5_kernel, n_steps=n_steps, rows_per_step=r, row_group=row_group)
    const = lambda shape: pl.BlockSpec(shape, lambda n, c: (0,) * len(shape))
    return pl.pallas_call(
        kern,
        grid=(nbatch, rows // chunk_rows),
        in_specs=[
            pl.BlockSpec((1, chunk_rows, w), lambda n, c: (n, c, 0)),
            pl.BlockSpec((1, r, s), lambda n, c: (n, 0, 0)),
            pl.BlockSpec((1, r, s), lambda n, c: (n, 0, 0)),
            const((1, s)), const((1, s)), const(wb.shape), const(wc.shape), const((1, w)),
        ],
        out_specs=[
            pl.BlockSpec((1, chunk_rows, w), lambda n, c: (n, c, 0)),
            pl.BlockSpec((1, r, s), lambda n, c: (n, 0, 0)),
            pl.BlockSpec((1, r, s), lambda n, c: (n, 0, 0)),
        ],
        out_shape=[jax.ShapeDtypeStruct((nbatch, rows, w), F32),
                   jax.ShapeDtypeStruct((nbatch, r, s), F32),
                   jax.ShapeDtypeStruct((nbatch, r, s), F32)],
        scratch_shapes=[pltpu.VMEM((chunk_rows, s), F32), pltpu.VMEM((chunk_rows, s), F32),
                        pltpu.VMEM((r, s), F32), pltpu.VMEM((r, s), F32)],
        compiler_params=_cparams("parallel", "arbitrary"),
        name="s5_scan",
    )(u, h0_re, h0_im, ab_re, ab_im, wb, wc, d)


def _key_to_f32(k):
    bits = jnp.where(k >= 0, k, k ^ jnp.int32(0x7FFFFFFF))
    return lax.bitcast_convert_type(bits, F32)


def _kth_threshold(count_ge, shape, topk):
    kf = float(topk)
    c_zero = count_ge(jnp.zeros(shape, F32))
    c_low = count_ge(jnp.full(shape, _LOWEST_F32, F32))
    pos = c_zero >= kf
    lo = jnp.where(pos, 0, KEY_LOWEST).astype(jnp.int32)
    hi = jnp.where(pos, KEY_POS_INF + 1, 0).astype(jnp.int32)
    cnt_lo = jnp.where(pos, c_zero, c_low)

    def body(_, carry):
        lo, hi, cnt_lo = carry
        mid = lo + ((hi - lo) >> 1)
        cnt = count_ge(_key_to_f32(mid))
        ok = cnt >= kf
        return jnp.where(ok, mid, lo), jnp.where(ok, hi, mid), jnp.where(ok, cnt, cnt_lo)

    lo, hi, cnt_lo = lax.fori_loop(0, 31, body, (lo, hi, cnt_lo))
    return _key_to_f32(lo), cnt_lo


def _tie_index_bound(count_eq_le, count_gt, cnt_ge, shape, topk, n_keys):
    need = float(topk) - count_gt
    tied = cnt_ge > float(topk)
    lo = jnp.full(shape, -1, jnp.int32)
    hi = jnp.full(shape, n_keys - 1, jnp.int32)

    def body(_, carry):
        lo, hi = carry
        mid = lo + ((hi - lo) >> 1)
        ok = count_eq_le(mid) >= need
        return jnp.where(ok, lo, mid), jnp.where(ok, mid, hi)

    n_iter = max(1, math.ceil(math.log2(n_keys + 1)))
    lo, hi = lax.fori_loop(0, n_iter, body, (lo, hi))
    return jnp.where(tied, hi, n_keys)


def _prompt_attn_kernel(qt_ref, qit_ref, w_ref, ki_ref, k_ref, vt_ref, o_ref,
                        s_ref, m_ref, l_ref, acc_ref, *, topk, q_block, chunk, n_heads):
    qb = pl.program_id(1)
    n_keys = k_ref.shape[1] * chunk
    n_chunks = (qb * q_block + q_block + chunk - 1) // chunk
    k_row = lax.broadcasted_iota(jnp.int32, (chunk, q_block), 0)
    q_pos = qb * q_block + lax.broadcasted_iota(jnp.int32, (chunk, q_block), 1)

    qi_rows = qit_ref[0, 0].astype(BF16)
    qit = jnp.concatenate([qi_rows[h * IDX_DIM:(h + 1) * IDX_DIM] for h in range(IDX_HEADS)], axis=1)
    w_rows = w_ref[0, 0]
    w_row = jnp.concatenate([w_rows[h:h + 1] for h in range(IDX_HEADS)], axis=1)

    def score_body(c, carry):
        st = _dot(ki_ref[0, c], qit)
        st = jnp.maximum(st, 0.0) * w_row
        sc = st[:, 0:q_block]
        for h in range(1, IDX_HEADS):
            sc = sc + st[:, h * q_block:(h + 1) * q_block]
        s_ref[c] = jnp.where(c * chunk + k_row <= q_pos, sc, NEG_INF)
        return carry

    lax.fori_loop(0, n_chunks, score_body, 0)

    row = (1, q_block)
    fold = min(chunk, 64)

    def count(pred):
        def body(c, acc):
            hits = jnp.where(pred(c, s_ref[c]), 1.0, 0.0)
            return acc + jnp.sum(hits.reshape(chunk // fold, fold, q_block), axis=0)
        acc = lax.fori_loop(0, n_chunks, body, jnp.zeros((fold, q_block), F32))
        return jnp.sum(acc, axis=0, keepdims=True)

    thr, cnt_ge = _kth_threshold(lambda t: count(lambda c, s: s >= t), row, topk)

    @pl.when(jnp.max(cnt_ge) > float(topk))
    def _():
        cnt_gt = count(lambda c, s: s > thr)
        bound = _tie_index_bound(
            lambda j: count(lambda c, s: (s == thr) & (c * chunk + k_row <= j)),
            cnt_gt, cnt_ge, row, topk, n_keys)

        def drop_body(c, carry):
            s = s_ref[c]
            s_ref[c] = jnp.where((s == thr) & (c * chunk + k_row > bound), NEG_INF, s)
            return carry

        lax.fori_loop(0, n_chunks, drop_body, 0)

    m_ref[...] = jnp.full(m_ref.shape, NEG_INF, F32)
    l_ref[...] = jnp.zeros(l_ref.shape, F32)
    acc_ref[...] = jnp.zeros(acc_ref.shape, F32)
    qt = qt_ref[0, 0] * (HEAD_DIM ** -0.5 * LOG2_E)
    head_of_dim = lax.broadcasted_iota(jnp.int32, (LANE, 2 * q_block), 0) // HEAD_DIM
    head_of_col = lax.broadcasted_iota(jnp.int32, (LANE, 2 * q_block), 1) // q_block
    q_pairs = []
    for j in range(n_heads // 2):
        qj = qt[j * LANE:(j + 1) * LANE, :]
        q_pairs.append(jnp.where(head_of_dim == head_of_col, jnp.concatenate([qj, qj], axis=1), 0.0).astype(BF16))

    def attn_body(c, carry):
        sel = s_ref[c] >= thr
        logits = [_dot(k_ref[0, c, :, j * LANE:(j + 1) * LANE], q_pairs[j]) for j in range(n_heads // 2)]
        for h in range(n_heads):
            j, hh = divmod(h, 2)
            x = jnp.where(sel, logits[j][:, hh * q_block:(hh + 1) * q_block], NEG_INF)
            m_old = m_ref[h]
            m_new = jnp.maximum(m_old, jnp.max(x, axis=0, keepdims=True))
            m_safe = jnp.where(m_new == NEG_INF, 0.0, m_new)
            p = jnp.exp2(x - m_safe)
            scale = jnp.exp2(m_old - m_safe)
            l_ref[h] = scale * l_ref[h] + jnp.sum(p, axis=0, keepdims=True)
            pv = _dot(vt_ref[0, c, h * HEAD_DIM:(h + 1) * HEAD_DIM, :], p.astype(BF16))
            acc_ref[h] = scale * acc_ref[h] + pv
            m_ref[h] = m_new
        return carry

    lax.fori_loop(0, n_chunks, attn_body, 0)

    out_t = jnp.concatenate([acc_ref[h] / l_ref[h] for h in range(n_heads)], axis=0)
    o_ref[0] = out_t.T


def _prompt_attention(q_t, qi_t, w_t, ki, k, v_t, topk, q_block, chunk):
    b, ntq, aw, tq = q_t.shape
    nch = k.shape[1]
    l = nch * chunk
    nqb = l // q_block
    per_tile = tq // q_block
    n_heads = aw // HEAD_DIM
    assert n_heads % 2 == 0 and ntq * tq == l and tq % q_block == 0 and chunk % q_block == 0
    assert 2 * HEAD_DIM == LANE
    kern = functools.partial(_prompt_attn_kernel, topk=topk, q_block=q_block, chunk=chunk, n_heads=n_heads)
    q_side = lambda rows: pl.BlockSpec((1, 1, rows, q_block), lambda i, j: (i, j // per_tile, 0, j % per_tile))
    return pl.pallas_call(
        kern,
        grid=(b, nqb),
        in_specs=[
            q_side(aw), q_side(IDX_HEADS * IDX_DIM), q_side(IDX_HEADS),
            pl.BlockSpec((1, nch, chunk, IDX_DIM), lambda i, j: (i, 0, 0, 0)),
            pl.BlockSpec((1, nch, chunk, aw), lambda i, j: (i, 0, 0, 0)),
            pl.BlockSpec((1, nch, aw, chunk), lambda i, j: (i, 0, 0, 0)),
        ],
        out_specs=pl.BlockSpec((1, q_block, aw), lambda i, j: (i, j, 0)),
        out_shape=jax.ShapeDtypeStruct((b, l, aw), F32),
        scratch_shapes=[pltpu.VMEM((nch, chunk, q_block), F32),
                        pltpu.VMEM((n_heads, 1, q_block), F32),
                        pltpu.VMEM((n_heads, 1, q_block), F32),
                        pltpu.VMEM((n_heads, HEAD_DIM, q_block), F32)],
        compiler_params=_cparams("parallel", "arbitrary"),
        name="prompt_attention",
    )(q_t, qi_t, w_t, ki, k, v_t)


def _sample_select_kernel(pt_ref, qi_ref, w_ref, knew_ref, cache_ref, bias_ref, kbuf_ref, s_ref, sem,
                          *, topk, layer, group):
    step = pl.program_id(0)
    n_pages, page = kbuf_ref.shape[1] - 1, kbuf_ref.shape[3]
    n_q = knew_ref.shape[2]
    n_keys = (n_pages + 1) * page

    def page_copy(g, i):
        return pltpu.make_async_copy(cache_ref.at[layer, pt_ref[step * group + g, i]], kbuf_ref.at[g, i], sem.at[0])

    def start_body(g, carry):
        for i in range(n_pages):
            page_copy(g, i).start()
        return carry

    lax.fori_loop(0, group, start_body, 0)

    for g in range(group):
        kbuf_ref[g, n_pages] = jnp.zeros((IDX_DIM, page), F32)
        kbuf_ref[g, n_pages, :, 0:n_q] = knew_ref[g]

    def wait_body(g, carry):
        for i in range(n_pages):
            page_copy(g, i).wait()
        return carry

    lax.fori_loop(0, group, wait_body, 0)

    for g in range(group):
        qi = qi_ref[g].astype(BF16)
        w_col = w_ref[g]
        for i in range(n_pages + 1):
            st = _dot(qi, kbuf_ref[g, i].astype(BF16))
            st = jnp.maximum(st, 0.0) * w_col
            s_ref[g * n_q:(g + 1) * n_q, i * page:(i + 1) * page] = jnp.sum(st.reshape(IDX_HEADS, n_q, page), axis=0)

    rows = group * n_q
    q_of_row = lax.broadcasted_iota(jnp.int32, (rows, n_keys), 0) % n_q
    key_idx = lax.broadcasted_iota(jnp.int32, (rows, n_keys), 1)
    s_ref[...] = jnp.where(key_idx - n_pages * page <= q_of_row, s_ref[...], NEG_INF)

    col = (rows, 1)

    def count(pred):
        hits = jnp.where(pred(s_ref[...]), 1.0, 0.0)
        part = hits[:, 0:LANE]
        for t in range(1, n_keys // LANE):
            part = part + hits[:, t * LANE:(t + 1) * LANE]
        return jnp.sum(part, axis=1, keepdims=True)

    thr, cnt_ge = _kth_threshold(lambda t: count(lambda s: s >= t), col, topk)

    @pl.when(jnp.max(cnt_ge) > float(topk))
    def _():
        cnt_gt = count(lambda s: s > thr)
        bound = _tie_index_bound(lambda j: count(lambda s: (s == thr) & (key_idx <= j)),
                                 cnt_gt, cnt_ge, col, topk, n_keys)
        s = s_ref[...]
        s_ref[...] = jnp.where((s == thr) & (key_idx > bound), NEG_INF, s)

    bias_ref[...] = jnp.where(s_ref[...] >= thr, 0.0, NEG_INF).reshape(group, n_q, n_keys)


def _sample_select(page_table, qi_hm, w_col, cache_kidx_t, ki_new_t, layer, topk, group):
    db, n_pages = page_table.shape
    page = cache_kidx_t.shape[3]
    n_q = ki_new_t.shape[2]
    rows = IDX_HEADS * n_q
    n_keys = (n_pages + 1) * page
    assert db % group == 0 and n_q <= page
    kern = functools.partial(_sample_select_kernel, topk=topk, layer=layer, group=group)
    grid_spec = pltpu.PrefetchScalarGridSpec(
        num_scalar_prefetch=1,
        grid=(db // group,),
        in_specs=[
            pl.BlockSpec((group, rows, IDX_DIM), lambda s, pt: (s, 0, 0)),
            pl.BlockSpec((group, rows, 1), lambda s, pt: (s, 0, 0)),
            pl.BlockSpec((group, IDX_DIM, n_q), lambda s, pt: (s, 0, 0)),
            pl.BlockSpec(memory_space=pl.ANY),
        ],
        out_specs=pl.BlockSpec((group, n_q, n_keys), lambda s, pt: (s, 0, 0)),
        scratch_shapes=[pltpu.VMEM((group, n_pages + 1, IDX_DIM, page), F32),
                        pltpu.VMEM((group * n_q, n_keys), F32),
                        pltpu.SemaphoreType.DMA((1,))],
    )
    return pl.pallas_call(
        kern,
        grid_spec=grid_spec,
        out_shape=jax.ShapeDtypeStruct((db, n_q, n_keys), F32),
        compiler_params=_cparams("arbitrary"),
        name="sample_select",
    )(page_table, qi_hm, w_col, ki_new_t, cache_kidx_t)


def _sample_attn_kernel(pt_ref, q_ref, bias_ref, knew_ref, vnew_ref, kcache_ref, vcache_ref, o_ref,
                        kbuf_ref, vbuf_ref, lg_ref, kpad_ref, vpad_ref, sem, *, layer, n_heads):
    b = pl.program_id(0)
    n_seq = pl.num_programs(0)
    n_pages, page = kbuf_ref.shape[1], kbuf_ref.shape[4]
    n_q, aw = q_ref.shape[1], q_ref.shape[2]
    past = n_pages * page
    rows = n_heads * n_q
    slot = b % 2

    def page_copies(seq, i, to_slot):
        idx = pt_ref[seq, i]
        return (pltpu.make_async_copy(kcache_ref.at[layer, idx], kbuf_ref.at[to_slot, i], sem.at[to_slot, 0]),
                pltpu.make_async_copy(vcache_ref.at[layer, idx], vbuf_ref.at[to_slot, i], sem.at[to_slot, 1]))

    def start_fetch(seq, to_slot):
        for i in range(n_pages):
            for cp in page_copies(seq, i, to_slot):
                cp.start()

    @pl.when(b == 0)
    def _():
        start_fetch(0, 0)

    @pl.when(b + 1 < n_seq)
    def _():
        start_fetch(b + 1, 1 - slot)

    kpad_ref[...] = jnp.zeros(kpad_ref.shape, F32)
    vpad_ref[...] = jnp.zeros(vpad_ref.shape, F32)
    kpad_ref[0:n_q, :] = knew_ref[0]
    vpad_ref[0:n_q, :] = vnew_ref[0]

    q = q_ref[0] * (HEAD_DIM ** -0.5 * LOG2_E)
    head_of_row = lax.broadcasted_iota(jnp.int32, (rows, aw), 0) // n_q
    head_of_lane = lax.broadcasted_iota(jnp.int32, (rows, aw), 1) // HEAD_DIM
    own_head = head_of_row == head_of_lane
    q_diag = jnp.where(own_head, jnp.concatenate([q] * n_heads, axis=0), 0.0).astype(BF16)
    lg_ref[:, past:past + page] = _dot_nt(q_diag, kpad_ref[...].astype(BF16))

    for i in range(n_pages):
        for cp in page_copies(b, i, slot):
            cp.wait()

    for i in range(n_pages):
        k_t = kbuf_ref[slot, i].reshape(aw, page).astype(BF16)
        lg_ref[:, i * page:(i + 1) * page] = _dot(q_diag, k_t)

    logits = lg_ref[...] + jnp.concatenate([bias_ref[0]] * n_heads, axis=0)
    m = jnp.max(logits, axis=1, keepdims=True)
    pr = jnp.exp2(logits - m)
    denom = jnp.sum(pr, axis=1, keepdims=True)
    pr = pr.astype(BF16)

    acc = _dot(pr[:, past:past + page], vpad_ref[...].astype(BF16))
    for i in range(n_pages):
        v_t = vbuf_ref[slot, i].reshape(aw, page).astype(BF16)
        acc = acc + _dot_nt(pr[:, i * page:(i + 1) * page], v_t)
    out = jnp.where(own_head, acc / denom, 0.0)
    o_ref[0] = jnp.sum(out.reshape(n_heads, n_q, aw), axis=0)


def _sample_attention(page_table, q, bias, cache_k_t, cache_v_t, k_new, v_new, layer):
    db, n_pages = page_table.shape
    _, n_q, aw = q.shape
    n_heads = aw // HEAD_DIM
    page = cache_k_t.shape[4]
    n_keys = (n_pages + 1) * page
    rows = n_heads * n_q
    kern = functools.partial(_sample_attn_kernel, layer=layer, n_heads=n_heads)
    per_b = lambda shape: pl.BlockSpec(shape, lambda b, pt: (b,) + (0,) * (len(shape) - 1))
    grid_spec = pltpu.PrefetchScalarGridSpec(
        num_scalar_prefetch=1,
        grid=(db,),
        in_specs=[per_b((1, n_q, aw)), per_b((1, n_q, n_keys)), per_b((1, n_q, aw)), per_b((1, n_q, aw)),
                  pl.BlockSpec(memory_space=pl.ANY), pl.BlockSpec(memory_space=pl.ANY)],
        out_specs=per_b((1, n_q, aw)),
        scratch_shapes=[pltpu.VMEM((2, n_pages, n_heads, HEAD_DIM, page), F32),
                        pltpu.VMEM((2, n_pages, n_heads, HEAD_DIM, page), F32),
                        pltpu.VMEM((rows, n_keys), F32),
                        pltpu.VMEM((page, aw), F32), pltpu.VMEM((page, aw), F32),
                        pltpu.SemaphoreType.DMA((2, 2))],
    )
    return pl.pallas_call(
        kern,
        grid_spec=grid_spec,
        out_shape=jax.ShapeDtypeStruct((db, n_q, aw), F32),
        compiler_params=_cparams("arbitrary"),
        name="sample_attention",
    )(page_table, q, bias, k_new, v_new, cache_k_t, cache_v_t)


def _mix_kernel(x_ref, ys_ref, ya_ref, wglu_ref, bglu_ref, wos_ref, woa_ref, g_ref, b_ref, o_ref, *, alpha):
    y = ys_ref[...]
    gel = 0.5 * y * (1.0 + jnp.tanh(math.sqrt(2.0 / math.pi) * (y + 0.044715 * (y * y * y))))
    gate = jax.nn.sigmoid(_dot(gel.astype(BF16), wglu_ref[...]) + bglu_ref[...])
    y_ssm = gel * gate
    mix = _dot(y_ssm.astype(BF16), wos_ref[...]) + _dot(ya_ref[...].astype(BF16), woa_ref[...])
    o_ref[...] = _layer_norm(alpha * x_ref[...] + mix, g_ref[...], b_ref[...])


def _mix(x, y_ssm, y_att, w_glu, b_glu, w_out_ssm, w_out_att, g, b, alpha, tm):
    m, d = x.shape
    sw = y_ssm.shape[1]
    aw = y_att.shape[1]
    const = lambda shape: pl.BlockSpec(shape, lambda i: (0, 0))
    return pl.pallas_call(
        functools.partial(_mix_kernel, alpha=alpha),
        grid=(m // tm,),
        in_specs=[pl.BlockSpec((tm, d), lambda i: (i, 0)),
                  pl.BlockSpec((tm, sw), lambda i: (i, 0)),
                  pl.BlockSpec((tm, aw), lambda i: (i, 0)),
                  const((sw, sw)), const((1, sw)), const((sw, d)), const((aw, d)), const((1, d)), const((1, d))],
        out_specs=pl.BlockSpec((tm, d), lambda i: (i, 0)),
        out_shape=jax.ShapeDtypeStruct((m, d), F32),
        compiler_params=_cparams("parallel"),
        name="glu_outproj_ln",
    )(x, y_ssm, y_att, w_glu, b_glu, w_out_ssm, w_out_att, g, b)


def _row_tile(m, cap=512):
    t = cap
    while m % t:
        t //= 2
    return t


def _ff_tile(d_ff, cap=1536):
    best = LANE
    for t in range(LANE, cap + 1, LANE):
        if d_ff % t == 0:
            best = t
    return best


def _head_major(x, n_blocks, block, n_heads, dim):
    b = x.shape[0]
    x = x.reshape(b, n_blocks, block, n_heads, dim)
    return jnp.swapaxes(x, 2, 3).reshape(b, n_blocks, n_heads * block, dim)


def kernel(x_prompt, x_sample, cache_k, cache_v, cache_kidx, state_ssm_re, state_ssm_im, page_table,
           w_in, ssm_lambda_re, ssm_lambda_im, ssm_log_step, ssm_b_re, ssm_b_im, ssm_c_re, ssm_c_im,
           ssm_d, w_glu, b_glu, w_out, ffn1_up, ffn1_down, ffn2_up, ffn2_down, ln_g, ln_b):
    bsz, seq, d_model = x_prompt.shape
    db, dseq, _ = x_sample.shape
    depth = w_in.shape[0]
    sw = d_model // 2
    aw = d_model - sw
    n_groups = sw // SSM_GROUP
    n_state = n_groups * SSM_STATE
    n_heads = aw // HEAD_DIM
    page = cache_k.shape[2]
    n_pages = page_table.shape[1]
    past = n_pages * page
    alpha = (2.0 * depth) ** 0.25
    mp = bsz * seq
    ms = db * dseq
    tm_p = _row_tile(mp)
    tm_s = _row_tile(ms)
    q_block = min(128, seq)
    chunk = tm_p
    nch = seq // chunk
    topk_p = min(TOPK_MAX, seq // 4)
    topk_s = min(TOPK_MAX, (past + dseq) // 4)
    s5_rows = min(256, seq)
    s_groups = max(1, ms // 256)
    s_per = db // s_groups
    select_group = math.gcd(db, SUBLANE)
    cache_k_t = jnp.transpose(cache_k, (0, 1, 3, 4, 2))
    cache_v_t = jnp.transpose(cache_v, (0, 1, 3, 4, 2))
    cache_kidx_t = jnp.swapaxes(cache_kidx, 2, 3)
    pad_w = LANE - IDX_DIM - IDX_HEADS
    widths = (sw, aw, aw, aw, IDX_HEADS * IDX_DIM, LANE)

    xp = x_prompt.reshape(mp, d_model)
    xs = x_sample.reshape(ms, d_model)
    outs = {name: [] for name in ("kp", "vp", "kip", "hrp", "hip", "ks", "vs", "kis", "hrs", "his")}

    for l in range(depth):
        g_ln = ln_g[l][:, None, :]
        b_ln = ln_b[l][:, None, :]
        up1, down1 = ffn1_up[l].astype(BF16), ffn1_down[l].astype(BF16)
        tf1 = _ff_tile(down1.shape[0])
        xp = _ffn(xp, up1, down1, g_ln[0], b_ln[0], alpha, tm_p, tf1)
        xs = _ffn(xs, up1, down1, g_ln[0], b_ln[0], alpha, tm_s, tf1)

        w_l = jnp.pad(w_in[l], ((0, 0), (0, pad_w))).astype(BF16)
        o_q, o_k, o_v, o_qi = sw, sw + aw, sw + 2 * aw, sw + 3 * aw
        o_ki = o_qi + IDX_HEADS * IDX_DIM
        w_nat = jnp.concatenate([w_l[:, 0:o_q], w_l[:, o_k:o_v], w_l[:, o_ki:]], axis=1)
        w_t = w_l[:, o_q:o_ki + IDX_DIM + 16].T
        (u_p, k_pb, ki_pb, q_t, v_tb, qi_t, w_t_p, k_pt, v_pt, ki_pt) = _inproj_prompt(
            xp, w_nat, w_t, bsz, seq, sw, aw, tm_p)
        u_s, q_s, k_s, v_s, qi_s, kiwi_s = _inproj(xs, w_l, widths, tm_s)
        ki_s, wi_s = kiwi_s[:, :IDX_DIM], kiwi_s[:, IDX_DIM:IDX_DIM + IDX_HEADS]

        q_s = q_s.reshape(db, dseq, aw)
        k_s = k_s.reshape(db, dseq, aw)
        v_s = v_s.reshape(db, dseq, aw)
        ki_s = ki_s.reshape(db, dseq, IDX_DIM)
        qi_hm = _head_major(qi_s.reshape(db, dseq, IDX_HEADS * IDX_DIM), 1, dseq, IDX_HEADS, IDX_DIM)[:, 0]
        w_hm = _head_major(wi_s.reshape(db, dseq, IDX_HEADS), 1, dseq, IDX_HEADS, 1)[:, 0]
        bias = _sample_select(page_table, qi_hm, w_hm, cache_kidx_t, jnp.swapaxes(ki_s, 1, 2), l, topk_s, select_group)
        y_att_s = _sample_attention(page_table, q_s, bias, cache_k_t, cache_v_t, k_s, v_s, l)

        ab_re, ab_im, bb_re_t, bb_im_t = _s5_params(ssm_lambda_re[l], ssm_lambda_im[l], ssm_log_step[l],
                                                     ssm_b_re[l], ssm_b_im[l])
        wb = _re_im_stack(_block_diag_in(bb_re_t), _block_diag_in(bb_im_t))
        wc = _re_im_stack(_block_diag_out(ssm_c_re[l]), _block_diag_out(ssm_c_im[l]))
        ab_re = ab_re.reshape(1, n_state)
        ab_im = ab_im.reshape(1, n_state)
        d_row = ssm_d[l].reshape(1, sw)

        zeros_state = jnp.zeros((bsz, 1, n_state), F32)
        y_ssm_p, hr_p, hi_p = _s5_scan(u_p.reshape(bsz, seq, sw), zeros_state, zeros_state,
                                       ab_re, ab_im, wb, wc, d_row, s5_rows, 1)
        u_s = jnp.swapaxes(u_s.reshape(s_groups, s_per, dseq, sw), 1, 2)
        y_ssm_s, hr_s, hi_s = _s5_scan(u_s.reshape(s_groups, dseq * s_per, sw),
                                       state_ssm_re[l].reshape(s_groups, s_per, n_state),
                                       state_ssm_im[l].reshape(s_groups, s_per, n_state),
                                       ab_re, ab_im, wb, wc, d_row, dseq, s_per)
        y_ssm_s = jnp.swapaxes(y_ssm_s.reshape(s_groups, dseq, s_per, sw), 1, 2).reshape(ms, sw)

        y_att_p = _prompt_attention(q_t, qi_t, w_t_p, ki_pb.reshape(bsz, nch, chunk, IDX_DIM),
                                    k_pb.reshape(bsz, nch, chunk, aw), v_tb, topk_p, q_block, chunk)

        w_g = w_glu[l].astype(BF16)
        b_g = b_glu[l].reshape(1, sw)
        w_o = w_out[l].astype(BF16)
        xp = _mix(xp, y_ssm_p.reshape(mp, sw), y_att_p.reshape(mp, aw), w_g, b_g, w_o[:sw], w_o[sw:],
                  g_ln[1], b_ln[1], alpha, tm_p)
        xs = _mix(xs, y_ssm_s, y_att_s.reshape(ms, aw), w_g, b_g, w_o[:sw], w_o[sw:],
                  g_ln[1], b_ln[1], alpha, tm_s)
        up2, down2 = ffn2_up[l].astype(BF16), ffn2_down[l].astype(BF16)
        tf2 = _ff_tile(down2.shape[0])
        xp = _ffn(xp, up2, down2, g_ln[2], b_ln[2], alpha, tm_p, tf2)
        xs = _ffn(xs, up2, down2, g_ln[2], b_ln[2], alpha, tm_s, tf2)

        outs["kp"].append(jnp.transpose(k_pt.reshape(bsz, n_heads, HEAD_DIM, seq), (0, 3, 1, 2)))
        outs["vp"].append(jnp.transpose(v_pt.reshape(bsz, n_heads, HEAD_DIM, seq), (0, 3, 1, 2)))
        outs["kip"].append(jnp.swapaxes(ki_pt, 1, 2))
        outs["hrp"].append(hr_p.reshape(bsz, n_groups, SSM_STATE))
        outs["hip"].append(hi_p.reshape(bsz, n_groups, SSM_STATE))
        outs["ks"].append(k_s.reshape(db, dseq, n_heads, HEAD_DIM))
        outs["vs"].append(v_s.reshape(db, dseq, n_heads, HEAD_DIM))
        outs["kis"].append(ki_s)
        outs["hrs"].append(hr_s.reshape(db, n_groups, SSM_STATE))
        outs["his"].append(hi_s.reshape(db, n_groups, SSM_STATE))

    st = lambda name: jnp.stack(outs[name])
    return (xp.reshape(bsz, seq, d_model), xs.reshape(db, dseq, d_model),
            st("kp"), st("vp"), st("kip"), st("hrp"), st("hip"),
            st("ks"), st("vs"), st("kis"), st("hrs"), st("his"))
```

```python
import functools
import math

import jax
import jax.numpy as jnp
from jax import lax
from jax.experimental import pallas as pl
from jax.experimental.pallas import tpu as pltpu

F32 = jnp.float32
BF16 = jnp.bfloat16

SSM_GROUP = 16
SSM_STATE = 64
HEAD_DIM = 64
IDX_HEADS = 8
IDX_DIM = 32
TOPK_MAX = 256
LN_EPS = 1e-5

LANE = 128
SUBLANE = 8
VMEM_LIMIT_BYTES = 56 * 1024 * 1024

NEG_INF = float("-inf")
LOG2_E = 1.4426950408889634
_LOWEST_F32 = -3.4028234663852886e38
KEY_LOWEST = -2139095040
KEY_POS_INF = 0x7F800000


def _cparams(*sem):
    return pltpu.CompilerParams(dimension_semantics=sem, vmem_limit_bytes=VMEM_LIMIT_BYTES)


def _layer_norm(y, g, b):
    mu = jnp.mean(y, axis=-1, keepdims=True)
    yc = y - mu
    var = jnp.mean(yc * yc, axis=-1, keepdims=True)
    return yc * lax.rsqrt(var + LN_EPS) * g + b


def _dot(a, b):
    return jnp.dot(a, b, preferred_element_type=F32)


def _dot_nt(a, b):
    return lax.dot_general(a, b, (((1,), (1,)), ((), ())), preferred_element_type=F32)


def _ffn_kernel(x_ref, wg_ref, wu_ref, wd_ref, g_ref, b_ref, o_ref, xb_ref, acc_ref, *, alpha):
    j = pl.program_id(1)

    @pl.when(j == 0)
    def _():
        xb_ref[...] = x_ref[...].astype(BF16)
        acc_ref[...] = jnp.zeros_like(acc_ref)

    xb = xb_ref[...]
    gate = _dot(xb, wg_ref[...])
    up = _dot(xb, wu_ref[...])
    act = (gate * jax.nn.sigmoid(gate)) * up
    acc_ref[...] += _dot(act.astype(BF16), wd_ref[...])

    @pl.when(j == pl.num_programs(1) - 1)
    def _():
        y = alpha * x_ref[...] + 0.5 * acc_ref[...]
        o_ref[...] = _layer_norm(y, g_ref[...], b_ref[...])


def _ffn(x, w_up, w_down, g, b, alpha, tm, tf):
    m, d = x.shape
    d_ff = w_down.shape[0]
    nf = d_ff // tf
    assert m % tm == 0 and d_ff % tf == 0
    return pl.pallas_call(
        functools.partial(_ffn_kernel, alpha=alpha),
        grid=(m // tm, nf),
        in_specs=[
            pl.BlockSpec((tm, d), lambda i, j: (i, 0)),
            pl.BlockSpec((d, tf), lambda i, j: (0, j)),
            pl.BlockSpec((d, tf), lambda i, j: (0, j + nf)),
            pl.BlockSpec((tf, d), lambda i, j: (j, 0)),
            pl.BlockSpec((1, d), lambda i, j: (0, 0)),
            pl.BlockSpec((1, d), lambda i, j: (0, 0)),
        ],
        out_specs=pl.BlockSpec((tm, d), lambda i, j: (i, 0)),
        out_shape=jax.ShapeDtypeStruct((m, d), F32),
        scratch_shapes=[pltpu.VMEM((tm, d), BF16), pltpu.VMEM((tm, d), F32)],
        compiler_params=_cparams("parallel", "arbitrary"),
        name="swiglu_ln",
    )(x, w_up, w_up, w_down, g, b)


def _inproj_kernel(x_ref, w_ref, *o_refs):
    y = _dot(x_ref[...].astype(BF16), w_ref[...])
    off = 0
    for o_ref in o_refs:
        n = o_ref.shape[1]
        o_ref[...] = y[:, off:off + n]
        off += n


def _inproj(x, w, widths, tm):
    m, d = x.shape
    n = w.shape[1]
    assert sum(widths) == n and m % tm == 0
    return pl.pallas_call(
        _inproj_kernel,
        grid=(m // tm,),
        in_specs=[pl.BlockSpec((tm, d), lambda i: (i, 0)),
                  pl.BlockSpec((d, n), lambda i: (0, 0))],
        out_specs=[pl.BlockSpec((tm, wd), lambda i: (i, 0)) for wd in widths],
        out_shape=[jax.ShapeDtypeStruct((m, wd), F32) for wd in widths],
        compiler_params=_cparams("parallel"),
        name="in_proj",
    )(x, w)


def _inproj_prompt_kernel(x_ref, wn_ref, wt_ref, u_ref, kb_ref, kib_ref,
                          qt_ref, vtb_ref, qit_ref, wt_out_ref, kt_ref, vt_ref, kit_ref, *, sw, aw):
    xb = x_ref[...].astype(BF16)
    y = _dot(xb, wn_ref[...])
    u_ref[...] = y[:, 0:sw]
    kb_ref[...] = y[:, sw:sw + aw].astype(BF16)
    kib_ref[...] = y[:, sw + aw:sw + aw + IDX_DIM].astype(BF16)
    yt = _dot_nt(wt_ref[...], xb)
    n_qi = IDX_HEADS * IDX_DIM
    o_k, o_v, o_qi, o_ki, o_wi = aw, 2 * aw, 3 * aw, 3 * aw + n_qi, 3 * aw + n_qi + IDX_DIM
    qt_ref[0, 0] = yt[0:o_k]
    kt_ref[0] = yt[o_k:o_v]
    v_t = yt[o_v:o_qi]
    vt_ref[0] = v_t
    vtb_ref[0, 0] = v_t.astype(BF16)
    qit_ref[0, 0] = yt[o_qi:o_ki]
    kit_ref[0] = yt[o_ki:o_wi]
    wt_out_ref[0, 0] = yt[o_wi:o_wi + IDX_HEADS]


def _inproj_prompt(x, w_nat, w_t, bsz, seq, sw, aw, tm):
    m, d = x.shape
    nt = seq // tm
    n_qi = IDX_HEADS * IDX_DIM
    assert m == bsz * seq and seq % tm == 0
    row = lambda width: pl.BlockSpec((tm, width), lambda i: (i, 0))
    tiled = lambda rows: pl.BlockSpec((1, 1, rows, tm), lambda i: (i // nt, i % nt, 0, 0))
    whole = lambda rows: pl.BlockSpec((1, rows, tm), lambda i: (i // nt, 0, i % nt))
    return pl.pallas_call(
        functools.partial(_inproj_prompt_kernel, sw=sw, aw=aw),
        grid=(m // tm,),
        in_specs=[row(d), pl.BlockSpec(w_nat.shape, lambda i: (0, 0)), pl.BlockSpec(w_t.shape, lambda i: (0, 0))],
        out_specs=[row(sw), row(aw), row(IDX_DIM),
                   tiled(aw), tiled(aw), tiled(n_qi), tiled(IDX_HEADS),
                   whole(aw), whole(aw), whole(IDX_DIM)],
        out_shape=[jax.ShapeDtypeStruct((m, sw), F32),
                   jax.ShapeDtypeStruct((m, aw), BF16), jax.ShapeDtypeStruct((m, IDX_DIM), BF16),
                   jax.ShapeDtypeStruct((bsz, nt, aw, tm), F32), jax.ShapeDtypeStruct((bsz, nt, aw, tm), BF16),
                   jax.ShapeDtypeStruct((bsz, nt, n_qi, tm), F32), jax.ShapeDtypeStruct((bsz, nt, IDX_HEADS, tm), F32),
                   jax.ShapeDtypeStruct((bsz, aw, seq), F32), jax.ShapeDtypeStruct((bsz, aw, seq), F32),
                   jax.ShapeDtypeStruct((bsz, IDX_DIM, seq), F32)],
        compiler_params=_cparams("parallel"),
        name="in_proj_prompt",
    )(x, w_nat, w_t)


def _s5_param_kernel(lre_ref, lim_ref, ls_ref, bre_ref, bim_ref, abre_ref, abim_ref, bbre_ref, bbim_ref):
    lam_re = lre_ref[...]
    lam_im = lim_ref[...]
    step = jnp.exp(ls_ref[...])
    mag = jnp.exp(lam_re * step)
    ang = lam_im * step
    ab_re = mag * jnp.cos(ang)
    ab_im = mag * jnp.sin(ang)
    nr = ab_re - 1.0
    ni = ab_im
    den = lam_re * lam_re + lam_im * lam_im
    f_re = (nr * lam_re + ni * lam_im) / den
    f_im = (ni * lam_re - nr * lam_im) / den
    abre_ref[...] = ab_re
    abim_ref[...] = ab_im
    b_re = bre_ref[...]
    b_im = bim_ref[...]
    bbre_ref[...] = f_re[:, None, :] * b_re - f_im[:, None, :] * b_im
    bbim_ref[...] = f_re[:, None, :] * b_im + f_im[:, None, :] * b_re


def _s5_params(lam_re, lam_im, log_step, b_re, b_im):
    g, p = lam_re.shape
    h = b_re.shape[2]
    b_re_t = jnp.swapaxes(b_re, 1, 2)
    b_im_t = jnp.swapaxes(b_im, 1, 2)
    return pl.pallas_call(
        _s5_param_kernel,
        out_shape=[jax.ShapeDtypeStruct((g, p), F32), jax.ShapeDtypeStruct((g, p), F32),
                   jax.ShapeDtypeStruct((g, h, p), F32), jax.ShapeDtypeStruct((g, h, p), F32)],
        name="s5_params",
    )(lam_re, lam_im, log_step.reshape(g, 1), b_re_t, b_im_t)


def _block_diag_in(bb_t):
    g, h, p = bb_t.shape
    per = LANE // h
    x = bb_t.reshape(g // per, per, h, p)
    eye = jnp.eye(per, dtype=F32)
    return jnp.einsum("jahp,ab->jahbp", x, eye).reshape(g // per, per * h, per * p)


def _block_diag_out(c):
    g, h, p = c.shape
    per = LANE // h
    x = c.reshape(g // per, per, h, p)
    eye = jnp.eye(per, dtype=F32)
    return jnp.einsum("jahp,ab->japbh", x, eye).reshape(g // per, per * p, per * h)


def _re_im_stack(w_re, w_im):
    return jnp.stack([w_re, w_im]).astype(BF16)


def _s5_kernel(u_ref, h0re_ref, h0im_ref, abre_ref, abim_ref, wb_ref, wc_ref, d_ref,
               y_ref, hre_out, him_out, hre_s, him_s, stre_s, stim_s, *, n_steps, rows_per_step, row_group):
    c = pl.program_id(1)
    nb = wb_ref.shape[1]
    sw = wb_ref.shape[3]

    @pl.when(c == 0)
    def _():
        stre_s[...] = h0re_ref[0]
        stim_s[...] = h0im_ref[0]

    u = u_ref[0]
    u_b = u.astype(BF16)
    for j in range(nb):
        uj = u_b[:, j * LANE:(j + 1) * LANE]
        hre_s[:, j * sw:(j + 1) * sw] = _dot(uj, wb_ref[0, j])
        him_s[:, j * sw:(j + 1) * sw] = _dot(uj, wb_ref[1, j])

    a_re = abre_ref[...]
    a_im = abim_ref[...]

    def group_body(rg, carry):
        r0 = pl.multiple_of(rg * row_group, row_group)

        def step_body(t, h):
            h_re, h_im = h
            row = t * rows_per_step + r0
            b_re = hre_s[pl.ds(row, row_group), :]
            b_im = him_s[pl.ds(row, row_group), :]
            n_re = a_re * h_re - a_im * h_im + b_re
            n_im = a_re * h_im + a_im * h_re + b_im
            hre_s[pl.ds(row, row_group), :] = n_re
            him_s[pl.ds(row, row_group), :] = n_im
            return n_re, n_im

        h0 = (stre_s[pl.ds(r0, row_group), :], stim_s[pl.ds(r0, row_group), :])
        h_re, h_im = lax.fori_loop(0, n_steps, step_body, h0)
        stre_s[pl.ds(r0, row_group), :] = h_re
        stim_s[pl.ds(r0, row_group), :] = h_im
        return carry

    lax.fori_loop(0, rows_per_step // row_group, group_body, 0)

    for j in range(nb):
        yj = (_dot(hre_s[:, j * sw:(j + 1) * sw].astype(BF16), wc_ref[0, j])
              - _dot(him_s[:, j * sw:(j + 1) * sw].astype(BF16), wc_ref[1, j]))
        sl = slice(j * LANE, (j + 1) * LANE)
        y_ref[0, :, sl] = yj + d_ref[:, sl] * u[:, sl]

    @pl.when(c == pl.num_programs(1) - 1)
    def _():
        hre_out[0] = stre_s[...]
        him_out[0] = stim_s[...]


def _s5_scan(u, h0_re, h0_im, ab_re, ab_im, wb, wc, d, n_steps, rows_per_step):
    nbatch, rows, w = u.shape
    s = ab_re.shape[1]
    r = rows_per_step
    chunk_rows = n_steps * r
    assert rows % chunk_rows == 0
    row_group = min(r, SUBLANE)
    assert r % row_group == 0
    kern = functools.partial(_s5_kernel, n_steps=n_steps, rows_per_step=r, row_group=row_group)
    const = lambda shape: pl.BlockSpec(shape, lambda n, c: (0,) * len(shape))
    return pl.pallas_call(
        kern,
        grid=(nbatch, rows // chunk_rows),
        in_specs=[
            pl.BlockSpec((1, chunk_rows, w), lambda n, c: (n, c, 0)),
            pl.BlockSpec((1, r, s), lambda n, c: (n, 0, 0)),
            pl.BlockSpec((1, r, s), lambda n, c: (n, 0, 0)),
            const((1, s)), const((1, s)), const(wb.shape), const(wc.shape), const((1, w)),
        ],
        out_specs=[
            pl.BlockSpec((1, chunk_rows, w), lambda n, c: (n, c, 0)),
            pl.BlockSpec((1, r, s), lambda n, c: (n, 0, 0)),
            pl.BlockSpec((1, r, s), lambda n, c: (n, 0, 0)),
        ],
        out_shape=[jax.ShapeDtypeStruct((nbatch, rows, w), F32),
                   jax.ShapeDtypeStruct((nbatch, r, s), F32),
                   jax.ShapeDtypeStruct((nbatch, r, s), F32)],
        scratch_shapes=[pltpu.VMEM((chunk_rows, s), F32), pltpu.VMEM((chunk_rows, s), F32),
                        pltpu.VMEM((r, s), F32), pltpu.VMEM((r, s), F32)],
        compiler_params=_cparams("parallel", "arbitrary"),
        name="s5_scan",
    )(u, h0_re, h0_im, ab_re, ab_im, wb, wc, d)


def _key_to_f32(k):
    bits = jnp.where(k >= 0, k, k ^ jnp.int32(0x7FFFFFFF))
    return lax.bitcast_convert_type(bits, F32)


def _kth_threshold(count_ge, shape, topk):
    kf = float(topk)
    c_zero = count_ge(jnp.zeros(shape, F32))
    c_low = count_ge(jnp.full(shape, _LOWEST_F32, F32))
    pos = c_zero >= kf
    lo = jnp.where(pos, 0, KEY_LOWEST).astype(jnp.int32)
    hi = jnp.where(pos, KEY_POS_INF + 1, 0).astype(jnp.int32)
    cnt_lo = jnp.where(pos, c_zero, c_low)

    def body(_, carry):
        lo, hi, cnt_lo = carry
        mid = lo + ((hi - lo) >> 1)
        cnt = count_ge(_key_to_f32(mid))
        ok = cnt >= kf
        return jnp.where(ok, mid, lo), jnp.where(ok, hi, mid), jnp.where(ok, cnt, cnt_lo)

    lo, hi, cnt_lo = lax.fori_loop(0, 31, body, (lo, hi, cnt_lo))
    return _key_to_f32(lo), cnt_lo


def _tie_index_bound(count_eq_le, count_gt, cnt_ge, shape, topk, n_keys):
    need = float(topk) - count_gt
    tied = cnt_ge > float(topk)
    lo = jnp.full(shape, -1, jnp.int32)
    hi = jnp.full(shape, n_keys - 1, jnp.int32)

    def body(_, carry):
        lo, hi = carry
        mid = lo + ((hi - lo) >> 1)
        ok = count_eq_le(mid) >= need
        return jnp.where(ok, lo, mid), jnp.where(ok, mid, hi)

    n_iter = max(1, math.ceil(math.log2(n_keys + 1)))
    lo, hi = lax.fori_loop(0, n_iter, body, (lo, hi))
    return jnp.where(tied, hi, n_keys)


def _prompt_attn_kernel(qt_ref, qit_ref, w_ref, ki_ref, k_ref, vt_ref, o_ref,
                        s_ref, m_ref, l_ref, acc_ref, *, topk, q_block, chunk, n_heads):
    qb = pl.program_id(1)
    n_keys = k_ref.shape[1] * chunk
    n_chunks = (qb * q_block + q_block + chunk - 1) // chunk
    k_row = lax.broadcasted_iota(jnp.int32, (chunk, q_block), 0)
    q_pos = qb * q_block + lax.broadcasted_iota(jnp.int32, (chunk, q_block), 1)

    qi_rows = qit_ref[0, 0].astype(BF16)
    qit = jnp.concatenate([qi_rows[h * IDX_DIM:(h + 1) * IDX_DIM] for h in range(IDX_HEADS)], axis=1)
    w_rows = w_ref[0, 0]
    w_row = jnp.concatenate([w_rows[h:h + 1] for h in range(IDX_HEADS)], axis=1)

    def score_body(c, carry):
        st = _dot(ki_ref[0, c], qit)
        st = jnp.maximum(st, 0.0) * w_row
        sc = st[:, 0:q_block]
        for h in range(1, IDX_HEADS):
            sc = sc + st[:, h * q_block:(h + 1) * q_block]
        s_ref[c] = jnp.where(c * chunk + k_row <= q_pos, sc, NEG_INF)
        return carry

    lax.fori_loop(0, n_chunks, score_body, 0)

    row = (1, q_block)
    fold = min(chunk, 64)

    @pl.when(n_chunks % 2 == 1)
    def _():
        s_ref[n_chunks] = jnp.full((chunk, q_block), NEG_INF, F32)

    def count(pred):
        def hits(c):
            h = jnp.where(pred(c, s_ref[c]), 1.0, 0.0)
            return jnp.sum(h.reshape(chunk // fold, fold, q_block), axis=0)

        def body(i, acc):
            return acc + hits(2 * i) + hits(2 * i + 1)
        acc = lax.fori_loop(0, (n_chunks + 1) // 2, body, jnp.zeros((fold, q_block), F32))
        return jnp.sum(acc, axis=0, keepdims=True)

    thr, cnt_ge = _kth_threshold(lambda t: count(lambda c, s: s >= t), row, topk)

    @pl.when(jnp.max(cnt_ge) > float(topk))
    def _():
        cnt_gt = count(lambda c, s: s > thr)
        bound = _tie_index_bound(
            lambda j: count(lambda c, s: (s == thr) & (c * chunk + k_row <= j)),
            cnt_gt, cnt_ge, row, topk, n_keys)

        def drop_body(c, carry):
            s = s_ref[c]
            s_ref[c] = jnp.where((s == thr) & (c * chunk + k_row > bound), NEG_INF, s)
            return carry

        lax.fori_loop(0, n_chunks, drop_body, 0)

    m_ref[...] = jnp.full(m_ref.shape, NEG_INF, F32)
    l_ref[...] = jnp.zeros(l_ref.shape, F32)
    acc_ref[...] = jnp.zeros(acc_ref.shape, F32)
    qt = qt_ref[0, 0] * (HEAD_DIM ** -0.5 * LOG2_E)
    head_of_dim = lax.broadcasted_iota(jnp.int32, (LANE, 2 * q_block), 0) // HEAD_DIM
    head_of_col = lax.broadcasted_iota(jnp.int32, (LANE, 2 * q_block), 1) // q_block
    q_pairs = []
    for j in range(n_heads // 2):
        qj = qt[j * LANE:(j + 1) * LANE, :]
        q_pairs.append(jnp.where(head_of_dim == head_of_col, jnp.concatenate([qj, qj], axis=1), 0.0).astype(BF16))

    def attn_body(c, carry):
        sel = s_ref[c] >= thr
        logits = [_dot(k_ref[0, c, :, j * LANE:(j + 1) * LANE], q_pairs[j]) for j in range(n_heads // 2)]
        for h in range(n_heads):
            j, hh = divmod(h, 2)
            x = jnp.where(sel, logits[j][:, hh * q_block:(hh + 1) * q_block], NEG_INF)
            m_old = m_ref[h]
            m_new = jnp.maximum(m_old, jnp.max(x, axis=0, keepdims=True))
            m_safe = jnp.where(m_new == NEG_INF, 0.0, m_new)
            p = jnp.exp2(x - m_safe)
            scale = jnp.exp2(m_old - m_safe)
            l_ref[h] = scale * l_ref[h] + jnp.sum(p, axis=0, keepdims=True)
            pv = _dot(vt_ref[0, c, h * HEAD_DIM:(h + 1) * HEAD_DIM, :], p.astype(BF16))
            acc_ref[h] = scale * acc_ref[h] + pv
            m_ref[h] = m_new
        return carry

    lax.fori_loop(0, n_chunks, attn_body, 0)

    out_t = jnp.concatenate([acc_ref[h] / l_ref[h] for h in range(n_heads)], axis=0)
    o_ref[0] = out_t.T


def _prompt_attention(q_t, qi_t, w_t, ki, k, v_t, topk, q_block, chunk):
    b, ntq, aw, tq = q_t.shape
    nch = k.shape[1]
    l = nch * chunk
    nqb = l // q_block
    per_tile = tq // q_block
    n_heads = aw // HEAD_DIM
    assert n_heads % 2 == 0 and ntq * tq == l and tq % q_block == 0 and chunk % q_block == 0
    assert 2 * HEAD_DIM == LANE
    kern = functools.partial(_prompt_attn_kernel, topk=topk, q_block=q_block, chunk=chunk, n_heads=n_heads)
    q_side = lambda rows: pl.BlockSpec((1, 1, rows, q_block), lambda i, j: (i, j // per_tile, 0, j % per_tile))
    return pl.pallas_call(
        kern,
        grid=(b, nqb),
        in_specs=[
            q_side(aw), q_side(IDX_HEADS * IDX_DIM), q_side(IDX_HEADS),
            pl.BlockSpec((1, nch, chunk, IDX_DIM), lambda i, j: (i, 0, 0, 0)),
            pl.BlockSpec((1, nch, chunk, aw), lambda i, j: (i, 0, 0, 0)),
            pl.BlockSpec((1, nch, aw, chunk), lambda i, j: (i, 0, 0, 0)),
        ],
        out_specs=pl.BlockSpec((1, q_block, aw), lambda i, j: (i, j, 0)),
        out_shape=jax.ShapeDtypeStruct((b, l, aw), F32),
        scratch_shapes=[pltpu.VMEM((nch + nch % 2, chunk, q_block), F32),
                        pltpu.VMEM((n_heads, 1, q_block), F32),
                        pltpu.VMEM((n_heads, 1, q_block), F32),
                        pltpu.VMEM((n_heads, HEAD_DIM, q_block), F32)],
        compiler_params=_cparams("parallel", "arbitrary"),
        name="prompt_attention",
    )(q_t, qi_t, w_t, ki, k, v_t)


def _sample_select_kernel(pt_ref, qi_ref, w_ref, knew_ref, cache_ref, bias_ref, kbuf_ref, s_ref, sem,
                          *, topk, layer, group):
    step = pl.program_id(0)
    n_pages, page = kbuf_ref.shape[1] - 1, kbuf_ref.shape[3]
    n_q = knew_ref.shape[2]
    n_keys = (n_pages + 1) * page

    def page_copy(g, i):
        return pltpu.make_async_copy(cache_ref.at[layer, pt_ref[step * group + g, i]], kbuf_ref.at[g, i], sem.at[0])

    def start_body(g, carry):
        for i in range(n_pages):
            page_copy(g, i).start()
        return carry

    lax.fori_loop(0, group, start_body, 0)

    for g in range(group):
        kbuf_ref[g, n_pages] = jnp.zeros((IDX_DIM, page), F32)
        kbuf_ref[g, n_pages, :, 0:n_q] = knew_ref[g]

    def wait_body(g, carry):
        for i in range(n_pages):
            page_copy(g, i).wait()
        return carry

    lax.fori_loop(0, group, wait_body, 0)

    for g in range(group):
        qi = qi_ref[g].astype(BF16)
        w_col = w_ref[g]
        for i in range(n_pages + 1):
            st = _dot(qi, kbuf_ref[g, i].astype(BF16))
            st = jnp.maximum(st, 0.0) * w_col
            s_ref[g * n_q:(g + 1) * n_q, i * page:(i + 1) * page] = jnp.sum(st.reshape(IDX_HEADS, n_q, page), axis=0)

    rows = group * n_q
    q_of_row = lax.broadcasted_iota(jnp.int32, (rows, n_keys), 0) % n_q
    key_idx = lax.broadcasted_iota(jnp.int32, (rows, n_keys), 1)
    s_ref[...] = jnp.where(key_idx - n_pages * page <= q_of_row, s_ref[...], NEG_INF)

    col = (rows, 1)

    def count(pred):
        hits = jnp.where(pred(s_ref[...]), 1.0, 0.0)
        part = hits[:, 0:LANE]
        for t in range(1, n_keys // LANE):
            part = part + hits[:, t * LANE:(t + 1) * LANE]
        return jnp.sum(part, axis=1, keepdims=True)

    thr, cnt_ge = _kth_threshold(lambda t: count(lambda s: s >= t), col, topk)

    @pl.when(jnp.max(cnt_ge) > float(topk))
    def _():
        cnt_gt = count(lambda s: s > thr)
        bound = _tie_index_bound(lambda j: count(lambda s: (s == thr) & (key_idx <= j)),
                                 cnt_gt, cnt_ge, col, topk, n_keys)
        s = s_ref[...]
        s_ref[...] = jnp.where((s == thr) & (key_idx > bound), NEG_INF, s)

    bias_ref[...] = jnp.where(s_ref[...] >= thr, 0.0, NEG_INF).reshape(group, n_q, n_keys)


def _sample_select(page_table, qi_hm, w_col, cache_kidx_t, ki_new_t, layer, topk, group):
    db, n_pages = page_table.shape
    page = cache_kidx_t.shape[3]
    n_q = ki_new_t.shape[2]
    rows = IDX_HEADS * n_q
    n_keys = (n_pages + 1) * page
    assert db % group == 0 and n_q <= page
    kern = functools.partial(_sample_select_kernel, topk=topk, layer=layer, group=group)
    grid_spec = pltpu.PrefetchScalarGridSpec(
        num_scalar_prefetch=1,
        grid=(db // group,),
        in_specs=[
            pl.BlockSpec((group, rows, IDX_DIM), lambda s, pt: (s, 0, 0)),
            pl.BlockSpec((group, rows, 1), lambda s, pt: (s, 0, 0)),
            pl.BlockSpec((group, IDX_DIM, n_q), lambda s, pt: (s, 0, 0)),
            pl.BlockSpec(memory_space=pl.ANY),
        ],
        out_specs=pl.BlockSpec((group, n_q, n_keys), lambda s, pt: (s, 0, 0)),
        scratch_shapes=[pltpu.VMEM((group, n_pages + 1, IDX_DIM, page), F32),
                        pltpu.VMEM((group * n_q, n_keys), F32),
                        pltpu.SemaphoreType.DMA((1,))],
    )
    return pl.pallas_call(
        kern,
        grid_spec=grid_spec,
        out_shape=jax.ShapeDtypeStruct((db, n_q, n_keys), F32),
        compiler_params=_cparams("arbitrary"),
        name="sample_select",
    )(page_table, qi_hm, w_col, ki_new_t, cache_kidx_t)


def _sample_attn_kernel(pt_ref, q_ref, bias_ref, knew_ref, vnew_ref, kcache_ref, vcache_ref, o_ref,
                        kbuf_ref, vbuf_ref, lg_ref, kpad_ref, vpad_ref, sem, *, layer, n_heads):
    b = pl.program_id(0)
    n_seq = pl.num_programs(0)
    n_pages, page = kbuf_ref.shape[1], kbuf_ref.shape[4]
    n_q, aw = q_ref.shape[1], q_ref.shape[2]
    past = n_pages * page
    rows = n_heads * n_q
    slot = b % 2

    def page_copies(seq, i, to_slot):
        idx = pt_ref[seq, i]
        return (pltpu.make_async_copy(kcache_ref.at[layer, idx], kbuf_ref.at[to_slot, i], sem.at[to_slot, 0]),
                pltpu.make_async_copy(vcache_ref.at[layer, idx], vbuf_ref.at[to_slot, i], sem.at[to_slot, 1]))

    def start_fetch(seq, to_slot):
        for i in range(n_pages):
            for cp in page_copies(seq, i, to_slot):
                cp.start()

    @pl.when(b == 0)
    def _():
        start_fetch(0, 0)

    @pl.when(b + 1 < n_seq)
    def _():
        start_fetch(b + 1, 1 - slot)

    kpad_ref[...] = jnp.zeros(kpad_ref.shape, F32)
    vpad_ref[...] = jnp.zeros(vpad_ref.shape, F32)
    kpad_ref[0:n_q, :] = knew_ref[0]
    vpad_ref[0:n_q, :] = vnew_ref[0]

    q = q_ref[0] * (HEAD_DIM ** -0.5 * LOG2_E)
    head_of_row = lax.broadcasted_iota(jnp.int32, (rows, aw), 0) // n_q
    head_of_lane = lax.broadcasted_iota(jnp.int32, (rows, aw), 1) // HEAD_DIM
    own_head = head_of_row == head_of_lane
    q_diag = jnp.where(own_head, jnp.concatenate([q] * n_heads, axis=0), 0.0).astype(BF16)
    lg_ref[:, past:past + page] = _dot_nt(q_diag, kpad_ref[...].astype(BF16))

    for i in range(n_pages):
        for cp in page_copies(b, i, slot):
            cp.wait()

    for i in range(n_pages):
        k_t = kbuf_ref[slot, i].reshape(aw, page).astype(BF16)
        lg_ref[:, i * page:(i + 1) * page] = _dot(q_diag, k_t)

    logits = lg_ref[...] + jnp.concatenate([bias_ref[0]] * n_heads, axis=0)
    m = jnp.max(logits, axis=1, keepdims=True)
    pr = jnp.exp2(logits - m)
    denom = jnp.sum(pr, axis=1, keepdims=True)
    pr = pr.astype(BF16)

    acc = _dot(pr[:, past:past + page], vpad_ref[...].astype(BF16))
    for i in range(n_pages):
        v_t = vbuf_ref[slot, i].reshape(aw, page).astype(BF16)
        acc = acc + _dot_nt(pr[:, i * page:(i + 1) * page], v_t)
    out = jnp.where(own_head, acc / denom, 0.0)
    o_ref[0] = jnp.sum(out.reshape(n_heads, n_q, aw), axis=0)


def _sample_attention(page_table, q, bias, cache_k_t, cache_v_t, k_new, v_new, layer):
    db, n_pages = page_table.shape
    _, n_q, aw = q.shape
    n_heads = aw // HEAD_DIM
    page = cache_k_t.shape[4]
    n_keys = (n_pages + 1) * page
    rows = n_heads * n_q
    kern = functools.partial(_sample_attn_kernel, layer=layer, n_heads=n_heads)
    per_b = lambda shape: pl.BlockSpec(shape, lambda b, pt: (b,) + (0,) * (len(shape) - 1))
    grid_spec = pltpu.PrefetchScalarGridSpec(
        num_scalar_prefetch=1,
        grid=(db,),
        in_specs=[per_b((1, n_q, aw)), per_b((1, n_q, n_keys)), per_b((1, n_q, aw)), per_b((1, n_q, aw)),
                  pl.BlockSpec(memory_space=pl.ANY), pl.BlockSpec(memory_space=pl.ANY)],
        out_specs=per_b((1, n_q, aw)),
        scratch_shapes=[pltpu.VMEM((2, n_pages, n_heads, HEAD_DIM, page), F32),
                        pltpu.VMEM((2, n_pages, n_heads, HEAD_DIM, page), F32),
                        pltpu.VMEM((rows, n_keys), F32),
                        pltpu.VMEM((page, aw), F32), pltpu.VMEM((page, aw), F32),
                        pltpu.SemaphoreType.DMA((2, 2))],
    )
    return pl.pallas_call(
        kern,
        grid_spec=grid_spec,
        out_shape=jax.ShapeDtypeStruct((db, n_q, aw), F32),
        compiler_params=_cparams("arbitrary"),
        name="sample_attention",
    )(page_table, q, bias, k_new, v_new, cache_k_t, cache_v_t)


def _mix_kernel(x_ref, ys_ref, ya_ref, wglu_ref, bglu_ref, wos_ref, woa_ref, g_ref, b_ref, o_ref, *, alpha):
    y = ys_ref[...]
    gel = 0.5 * y * (1.0 + jnp.tanh(math.sqrt(2.0 / math.pi) * (y + 0.044715 * (y * y * y))))
    gate = jax.nn.sigmoid(_dot(gel.astype(BF16), wglu_ref[...]) + bglu_ref[...])
    y_ssm = gel * gate
    mix = _dot(y_ssm.astype(BF16), wos_ref[...]) + _dot(ya_ref[...].astype(BF16), woa_ref[...])
    o_ref[...] = _layer_norm(alpha * x_ref[...] + mix, g_ref[...], b_ref[...])


def _mix(x, y_ssm, y_att, w_glu, b_glu, w_out_ssm, w_out_att, g, b, alpha, tm):
    m, d = x.shape
    sw = y_ssm.shape[1]
    aw = y_att.shape[1]
    const = lambda shape: pl.BlockSpec(shape, lambda i: (0, 0))
    return pl.pallas_call(
        functools.partial(_mix_kernel, alpha=alpha),
        grid=(m // tm,),
        in_specs=[pl.BlockSpec((tm, d), lambda i: (i, 0)),
                  pl.BlockSpec((tm, sw), lambda i: (i, 0)),
                  pl.BlockSpec((tm, aw), lambda i: (i, 0)),
                  const((sw, sw)), const((1, sw)), const((sw, d)), const((aw, d)), const((1, d)), const((1, d))],
        out_specs=pl.BlockSpec((tm, d), lambda i: (i, 0)),
        out_shape=jax.ShapeDtypeStruct((m, d), F32),
        compiler_params=_cparams("parallel"),
        name="glu_outproj_ln",
    )(x, y_ssm, y_att, w_glu, b_glu, w_out_ssm, w_out_att, g, b)


def _row_tile(m, cap=512):
    t = cap
    while m % t:
        t //= 2
    return t


def _ff_tile(d_ff, cap=1536):
    best = LANE
    for t in range(LANE, cap + 1, LANE):
        if d_ff % t == 0:
            best = t
    return best


def _head_major(x, n_blocks, block, n_heads, dim):
    b = x.shape[0]
    x = x.reshape(b, n_blocks, block, n_heads, dim)
    return jnp.swapaxes(x, 2, 3).reshape(b, n_blocks, n_heads * block, dim)


def kernel(x_prompt, x_sample, cache_k, cache_v, cache_kidx, state_ssm_re, state_ssm_im, page_table,
           w_in, ssm_lambda_re, ssm_lambda_im, ssm_log_step, ssm_b_re, ssm_b_im, ssm_c_re, ssm_c_im,
           ssm_d, w_glu, b_glu, w_out, ffn1_up, ffn1_down, ffn2_up, ffn2_down, ln_g, ln_b):
    bsz, seq, d_model = x_prompt.shape
    db, dseq, _ = x_sample.shape
    depth = w_in.shape[0]
    sw = d_model // 2
    aw = d_model - sw
    n_groups = sw // SSM_GROUP
    n_state = n_groups * SSM_STATE
    n_heads = aw // HEAD_DIM
    page = cache_k.shape[2]
    n_pages = page_table.shape[1]
    past = n_pages * page
    alpha = (2.0 * depth) ** 0.25
    mp = bsz * seq
    ms = db * dseq
    tm_p = _row_tile(mp)
    tm_s = _row_tile(ms)
    q_block = min(128, seq)
    chunk = tm_p
    nch = seq // chunk
    topk_p = min(TOPK_MAX, seq // 4)
    topk_s = min(TOPK_MAX, (past + dseq) // 4)
    s5_rows = min(256, seq)
    s_groups = max(1, ms // 256)
    s_per = db // s_groups
    select_group = math.gcd(db, SUBLANE)
    cache_k_t = jnp.transpose(cache_k, (0, 1, 3, 4, 2))
    cache_v_t = jnp.transpose(cache_v, (0, 1, 3, 4, 2))
    cache_kidx_t = jnp.swapaxes(cache_kidx, 2, 3)
    pad_w = LANE - IDX_DIM - IDX_HEADS
    widths = (sw, aw, aw, aw, IDX_HEADS * IDX_DIM, LANE)

    xp = x_prompt.reshape(mp, d_model)
    xs = x_sample.reshape(ms, d_model)
    outs = {name: [] for name in ("kp", "vp", "kip", "hrp", "hip", "ks", "vs", "kis", "hrs", "his")}

    for l in range(depth):
        g_ln = ln_g[l][:, None, :]
        b_ln = ln_b[l][:, None, :]
        up1, down1 = ffn1_up[l].astype(BF16), ffn1_down[l].astype(BF16)
        tf1 = _ff_tile(down1.shape[0])
        xp = _ffn(xp, up1, down1, g_ln[0], b_ln[0], alpha, tm_p, tf1)
        xs = _ffn(xs, up1, down1, g_ln[0], b_ln[0], alpha, tm_s, tf1)

        w_l = jnp.pad(w_in[l], ((0, 0), (0, pad_w))).astype(BF16)
        o_q, o_k, o_v, o_qi = sw, sw + aw, sw + 2 * aw, sw + 3 * aw
        o_ki = o_qi + IDX_HEADS * IDX_DIM
        w_nat = jnp.concatenate([w_l[:, 0:o_q], w_l[:, o_k:o_v], w_l[:, o_ki:]], axis=1)
        w_t = w_l[:, o_q:o_ki + IDX_DIM + 16].T
        (u_p, k_pb, ki_pb, q_t, v_tb, qi_t, w_t_p, k_pt, v_pt, ki_pt) = _inproj_prompt(
            xp, w_nat, w_t, bsz, seq, sw, aw, tm_p)
        u_s, q_s, k_s, v_s, qi_s, kiwi_s = _inproj(xs, w_l, widths, tm_s)
        ki_s, wi_s = kiwi_s[:, :IDX_DIM], kiwi_s[:, IDX_DIM:IDX_DIM + IDX_HEADS]

        q_s = q_s.reshape(db, dseq, aw)
        k_s = k_s.reshape(db, dseq, aw)
        v_s = v_s.reshape(db, dseq, aw)
        ki_s = ki_s.reshape(db, dseq, IDX_DIM)
        qi_hm = _head_major(qi_s.reshape(db, dseq, IDX_HEADS * IDX_DIM), 1, dseq, IDX_HEADS, IDX_DIM)[:, 0]
        w_hm = _head_major(wi_s.reshape(db, dseq, IDX_HEADS), 1, dseq, IDX_HEADS, 1)[:, 0]
        bias = _sample_select(page_table, qi_hm, w_hm, cache_kidx_t, jnp.swapaxes(ki_s, 1, 2), l, topk_s, select_group)
        y_att_s = _sample_attention(page_table, q_s, bias, cache_k_t, cache_v_t, k_s, v_s, l)

        ab_re, ab_im, bb_re_t, bb_im_t = _s5_params(ssm_lambda_re[l], ssm_lambda_im[l], ssm_log_step[l],
                                                     ssm_b_re[l], ssm_b_im[l])
        wb = _re_im_stack(_block_diag_in(bb_re_t), _block_diag_in(bb_im_t))
        wc = _re_im_stack(_block_diag_out(ssm_c_re[l]), _block_diag_out(ssm_c_im[l]))
        ab_re = ab_re.reshape(1, n_state)
        ab_im = ab_im.reshape(1, n_state)
        d_row = ssm_d[l].reshape(1, sw)

        zeros_state = jnp.zeros((bsz, 1, n_state), F32)
        y_ssm_p, hr_p, hi_p = _s5_scan(u_p.reshape(bsz, seq, sw), zeros_state, zeros_state,
                                       ab_re, ab_im, wb, wc, d_row, s5_rows, 1)
        u_s = jnp.swapaxes(u_s.reshape(s_groups, s_per, dseq, sw), 1, 2)
        y_ssm_s, hr_s, hi_s = _s5_scan(u_s.reshape(s_groups, dseq * s_per, sw),
                                       state_ssm_re[l].reshape(s_groups, s_per, n_state),
                                       state_ssm_im[l].reshape(s_groups, s_per, n_state),
                                       ab_re, ab_im, wb, wc, d_row, dseq, s_per)
        y_ssm_s = jnp.swapaxes(y_ssm_s.reshape(s_groups, dseq, s_per, sw), 1, 2).reshape(ms, sw)

        y_att_p = _prompt_attention(q_t, qi_t, w_t_p, ki_pb.reshape(bsz, nch, chunk, IDX_DIM),
                                    k_pb.reshape(bsz, nch, chunk, aw), v_tb, topk_p, q_block, chunk)

        w_g = w_glu[l].astype(BF16)
        b_g = b_glu[l].reshape(1, sw)
        w_o = w_out[l].astype(BF16)
        xp = _mix(xp, y_ssm_p.reshape(mp, sw), y_att_p.reshape(mp, aw), w_g, b_g, w_o[:sw], w_o[sw:],
                  g_ln[1], b_ln[1], alpha, tm_p)
        xs = _mix(xs, y_ssm_s, y_att_s.reshape(ms, aw), w_g, b_g, w_o[:sw], w_o[sw:],
                  g_ln[1], b_ln[1], alpha, tm_s)
        up2, down2 = ffn2_up[l].astype(BF16), ffn2_down[l].astype(BF16)
        tf2 = _ff_tile(down2.shape[0])
        xp = _ffn(xp, up2, down2, g_ln[2], b_ln[2], alpha, tm_p, tf2)
        xs = _ffn(xs, up2, down2, g_ln[2], b_ln[2], alpha, tm_s, tf2)

        outs["kp"].append(jnp.transpose(k_pt.reshape(bsz, n_heads, HEAD_DIM, seq), (0, 3, 1, 2)))
        outs["vp"].append(jnp.transpose(v_pt.reshape(bsz, n_heads, HEAD_DIM, seq), (0, 3, 1, 2)))
        outs["kip"].append(jnp.swapaxes(ki_pt, 1, 2))
        outs["hrp"].append(hr_p.reshape(bsz, n_groups, SSM_STATE))
        outs["hip"].append(hi_p.reshape(bsz, n_groups, SSM_STATE))
        outs["ks"].append(k_s.reshape(db, dseq, n_heads, HEAD_DIM))
        outs["vs"].append(v_s.reshape(db, dseq, n_heads, HEAD_DIM))
        outs["kis"].append(ki_s)
        outs["hrs"].append(hr_s.reshape(db, n_groups, SSM_STATE))
        outs["his"].append(hi_s.reshape(db, n_groups, SSM_STATE))

    st = lambda name: jnp.stack(outs[name])
    return (xp.reshape(bsz, seq, d_model), xs.reshape(db, dseq, d_model),
            st("kp"), st("vp"), st("kip"), st("hrp"), st("hip"),
            st("ks"), st("vs"), st("kis"), st("hrs"), st("his"))
```

```python
import functools
import math

import jax
import jax.numpy as jnp
from jax import lax
from jax.experimental import pallas as pl
from jax.experimental.pallas import tpu as pltpu

F32 = jnp.float32
BF16 = jnp.bfloat16

SSM_GROUP = 16
SSM_STATE = 64
HEAD_DIM = 64
IDX_HEADS = 8
IDX_DIM = 32
TOPK_MAX = 256
LN_EPS = 1e-5

LANE = 128
SUBLANE = 8
VMEM_LIMIT_BYTES = 56 * 1024 * 1024
COUNT_UNROLL = 4

NEG_INF = float("-inf")
LOG2_E = 1.4426950408889634
_LOWEST_F32 = -3.4028234663852886e38
KEY_LOWEST = -2139095040
KEY_POS_INF = 0x7F800000


def _cparams(*sem):
    return pltpu.CompilerParams(dimension_semantics=sem, vmem_limit_bytes=VMEM_LIMIT_BYTES)


def _layer_norm(y, g, b):
    mu = jnp.mean(y, axis=-1, keepdims=True)
    yc = y - mu
    var = jnp.mean(yc * yc, axis=-1, keepdims=True)
    return yc * lax.rsqrt(var + LN_EPS) * g + b


def _dot(a, b):
    return jnp.dot(a, b, preferred_element_type=F32)


def _dot_nt(a, b):
    return lax.dot_general(a, b, (((1,), (1,)), ((), ())), preferred_element_type=F32)


def _ffn_kernel(x_ref, wg_ref, wu_ref, wd_ref, g_ref, b_ref, o_ref, xb_ref, acc_ref, *, alpha):
    j = pl.program_id(1)

    @pl.when(j == 0)
    def _():
        xb_ref[...] = x_ref[...].astype(BF16)
        acc_ref[...] = jnp.zeros_like(acc_ref)

    xb = xb_ref[...]
    gate = _dot(xb, wg_ref[...])
    up = _dot(xb, wu_ref[...])
    act = (gate * jax.nn.sigmoid(gate)) * up
    acc_ref[...] += _dot(act.astype(BF16), wd_ref[...])

    @pl.when(j == pl.num_programs(1) - 1)
    def _():
        y = alpha * x_ref[...] + 0.5 * acc_ref[...]
        o_ref[...] = _layer_norm(y, g_ref[...], b_ref[...])


def _ffn(x, w_up, w_down, g, b, alpha, tm, tf):
    m, d = x.shape
    d_ff = w_down.shape[0]
    nf = d_ff // tf
    assert m % tm == 0 and d_ff % tf == 0
    return pl.pallas_call(
        functools.partial(_ffn_kernel, alpha=alpha),
        grid=(m // tm, nf),
        in_specs=[
            pl.BlockSpec((tm, d), lambda i, j: (i, 0)),
            pl.BlockSpec((d, tf), lambda i, j: (0, j)),
            pl.BlockSpec((d, tf), lambda i, j: (0, j + nf)),
            pl.BlockSpec((tf, d), lambda i, j: (j, 0)),
            pl.BlockSpec((1, d), lambda i, j: (0, 0)),
            pl.BlockSpec((1, d), lambda i, j: (0, 0)),
        ],
        out_specs=pl.BlockSpec((tm, d), lambda i, j: (i, 0)),
        out_shape=jax.ShapeDtypeStruct((m, d), F32),
        scratch_shapes=[pltpu.VMEM((tm, d), BF16), pltpu.VMEM((tm, d), F32)],
        compiler_params=_cparams("parallel", "arbitrary"),
        name="swiglu_ln",
    )(x, w_up, w_up, w_down, g, b)


def _inproj_kernel(x_ref, w_ref, *o_refs):
    y = _dot(x_ref[...].astype(BF16), w_ref[...])
    off = 0
    for o_ref in o_refs:
        n = o_ref.shape[1]
        o_ref[...] = y[:, off:off + n]
        off += n


def _inproj(x, w, widths, tm):
    m, d = x.shape
    n = w.shape[1]
    assert sum(widths) == n and m % tm == 0
    return pl.pallas_call(
        _inproj_kernel,
        grid=(m // tm,),
        in_specs=[pl.BlockSpec((tm, d), lambda i: (i, 0)),
                  pl.BlockSpec((d, n), lambda i: (0, 0))],
        out_specs=[pl.BlockSpec((tm, wd), lambda i: (i, 0)) for wd in widths],
        out_shape=[jax.ShapeDtypeStruct((m, wd), F32) for wd in widths],
        compiler_params=_cparams("parallel"),
        name="in_proj",
    )(x, w)


def _inproj_prompt_kernel(x_ref, wn_ref, wt_ref, u_ref, kb_ref, kib_ref,
                          qt_ref, vtb_ref, qit_ref, wt_out_ref, kt_ref, vt_ref, kit_ref, *, sw, aw):
    xb = x_ref[...].astype(BF16)
    y = _dot(xb, wn_ref[...])
    u_ref[...] = y[:, 0:sw]
    kb_ref[...] = y[:, sw:sw + aw].astype(BF16)
    kib_ref[...] = y[:, sw + aw:sw + aw + IDX_DIM].astype(BF16)
    yt = _dot_nt(wt_ref[...], xb)
    n_qi = IDX_HEADS * IDX_DIM
    o_k, o_v, o_qi, o_ki, o_wi = aw, 2 * aw, 3 * aw, 3 * aw + n_qi, 3 * aw + n_qi + IDX_DIM
    qt_ref[0, 0] = yt[0:o_k]
    kt_ref[0] = yt[o_k:o_v]
    v_t = yt[o_v:o_qi]
    vt_ref[0] = v_t
    vtb_ref[0, 0] = v_t.astype(BF16)
    qit_ref[0, 0] = yt[o_qi:o_ki]
    kit_ref[0] = yt[o_ki:o_wi]
    wt_out_ref[0, 0] = yt[o_wi:o_wi + IDX_HEADS]


def _inproj_prompt(x, w_nat, w_t, bsz, seq, sw, aw, tm):
    m, d = x.shape
    nt = seq // tm
    n_qi = IDX_HEADS * IDX_DIM
    assert m == bsz * seq and seq % tm == 0
    row = lambda width: pl.BlockSpec((tm, width), lambda i: (i, 0))
    tiled = lambda rows: pl.BlockSpec((1, 1, rows, tm), lambda i: (i // nt, i % nt, 0, 0))
    whole = lambda rows: pl.BlockSpec((1, rows, tm), lambda i: (i // nt, 0, i % nt))
    return pl.pallas_call(
        functools.partial(_inproj_prompt_kernel, sw=sw, aw=aw),
        grid=(m // tm,),
        in_specs=[row(d), pl.BlockSpec(w_nat.shape, lambda i: (0, 0)), pl.BlockSpec(w_t.shape, lambda i: (0, 0))],
        out_specs=[row(sw), row(aw), row(IDX_DIM),
                   tiled(aw), tiled(aw), tiled(n_qi), tiled(IDX_HEADS),
                   whole(aw), whole(aw), whole(IDX_DIM)],
        out_shape=[jax.ShapeDtypeStruct((m, sw), F32),
                   jax.ShapeDtypeStruct((m, aw), BF16), jax.ShapeDtypeStruct((m, IDX_DIM), BF16),
                   jax.ShapeDtypeStruct((bsz, nt, aw, tm), F32), jax.ShapeDtypeStruct((bsz, nt, aw, tm), BF16),
                   jax.ShapeDtypeStruct((bsz, nt, n_qi, tm), F32), jax.ShapeDtypeStruct((bsz, nt, IDX_HEADS, tm), F32),
                   jax.ShapeDtypeStruct((bsz, aw, seq), F32), jax.ShapeDtypeStruct((bsz, aw, seq), F32),
                   jax.ShapeDtypeStruct((bsz, IDX_DIM, seq), F32)],
        compiler_params=_cparams("parallel"),
        name="in_proj_prompt",
    )(x, w_nat, w_t)


def _s5_param_kernel(lre_ref, lim_ref, ls_ref, bre_ref, bim_ref, abre_ref, abim_ref, bbre_ref, bbim_ref):
    lam_re = lre_ref[...]
    lam_im = lim_ref[...]
    step = jnp.exp(ls_ref[...])
    mag = jnp.exp(lam_re * step)
    ang = lam_im * step
    ab_re = mag * jnp.cos(ang)
    ab_im = mag * jnp.sin(ang)
    nr = ab_re - 1.0
    ni = ab_im
    den = lam_re * lam_re + lam_im * lam_im
    f_re = (nr * lam_re + ni * lam_im) / den
    f_im = (ni * lam_re - nr * lam_im) / den
    abre_ref[...] = ab_re
    abim_ref[...] = ab_im
    b_re = bre_ref[...]
    b_im = bim_ref[...]
    bbre_ref[...] = f_re[:, None, :] * b_re - f_im[:, None, :] * b_im
    bbim_ref[...] = f_re[:, None, :] * b_im + f_im[:, None, :] * b_re


def _s5_params(lam_re, lam_im, log_step, b_re, b_im):
    g, p = lam_re.shape
    h = b_re.shape[2]
    b_re_t = jnp.swapaxes(b_re, 1, 2)
    b_im_t = jnp.swapaxes(b_im, 1, 2)
    return pl.pallas_call(
        _s5_param_kernel,
        out_shape=[jax.ShapeDtypeStruct((g, p), F32), jax.ShapeDtypeStruct((g, p), F32),
                   jax.ShapeDtypeStruct((g, h, p), F32), jax.ShapeDtypeStruct((g, h, p), F32)],
        name="s5_params",
    )(lam_re, lam_im, log_step.reshape(g, 1), b_re_t, b_im_t)


def _block_diag_in(bb_t):
    g, h, p = bb_t.shape
    per = LANE // h
    x = bb_t.reshape(g // per, per, h, p)
    eye = jnp.eye(per, dtype=F32)
    return jnp.einsum("jahp,ab->jahbp", x, eye).reshape(g // per, per * h, per * p)


def _block_diag_out(c):
    g, h, p = c.shape
    per = LANE // h
    x = c.reshape(g // per, per, h, p)
    eye = jnp.eye(per, dtype=F32)
    return jnp.einsum("jahp,ab->japbh", x, eye).reshape(g // per, per * p, per * h)


def _re_im_stack(w_re, w_im):
    return jnp.stack([w_re, w_im]).astype(BF16)


def _s5_kernel(u_ref, h0re_ref, h0im_ref, abre_ref, abim_ref, wb_ref, wc_ref, d_ref,
               y_ref, hre_out, him_out, hre_s, him_s, stre_s, stim_s, *, n_steps, rows_per_step, row_group):
    c = pl.program_id(1)
    nb = wb_ref.shape[1]
    sw = wb_ref.shape[3]

    @pl.when(c == 0)
    def _():
        stre_s[...] = h0re_ref[0]
        stim_s[...] = h0im_ref[0]

    u = u_ref[0]
    u_b = u.astype(BF16)
    for j in range(nb):
        uj = u_b[:, j * LANE:(j + 1) * LANE]
        hre_s[:, j * sw:(j + 1) * sw] = _dot(uj, wb_ref[0, j])
        him_s[:, j * sw:(j + 1) * sw] = _dot(uj, wb_ref[1, j])

    a_re = abre_ref[...]
    a_im = abim_ref[...]

    def group_body(rg, carry):
        r0 = pl.multiple_of(rg * row_group, row_group)

        def step_body(t, h):
            h_re, h_im = h
            row = t * rows_per_step + r0
            b_re = hre_s[pl.ds(row, row_group), :]
            b_im = him_s[pl.ds(row, row_group), :]
            n_re = a_re * h_re - a_im * h_im + b_re
            n_im = a_re * h_im + a_im * h_re + b_im
            hre_s[pl.ds(row, row_group), :] = n_re
            him_s[pl.ds(row, row_group), :] = n_im
            return n_re, n_im

        h0 = (stre_s[pl.ds(r0, row_group), :], stim_s[pl.ds(r0, row_group), :])
        h_re, h_im = lax.fori_loop(0, n_steps, step_body, h0)
        stre_s[pl.ds(r0, row_group), :] = h_re
        stim_s[pl.ds(r0, row_group), :] = h_im
        return carry

    lax.fori_loop(0, rows_per_step // row_group, group_body, 0)

    for j in range(nb):
        yj = (_dot(hre_s[:, j * sw:(j + 1) * sw].astype(BF16), wc_ref[0, j])
              - _dot(him_s[:, j * sw:(j + 1) * sw].astype(BF16), wc_ref[1, j]))
        sl = slice(j * LANE, (j + 1) * LANE)
        y_ref[0, :, sl] = yj + d_ref[:, sl] * u[:, sl]

    @pl.when(c == pl.num_programs(1) - 1)
    def _():
        hre_out[0] = stre_s[...]
        him_out[0] = stim_s[...]


def _s5_scan(u, h0_re, h0_im, ab_re, ab_im, wb, wc, d, n_steps, rows_per_step):
    nbatch, rows, w = u.shape
    s = ab_re.shape[1]
    r = rows_per_step
    chunk_rows = n_steps * r
    assert rows % chunk_rows == 0
    row_group = min(r, SUBLANE)
    assert r % row_group == 0
    kern = functools.partial(_s5_kernel, n_steps=n_steps, rows_per_step=r, row_group=row_group)
    const = lambda shape: pl.BlockSpec(shape, lambda n, c: (0,) * len(shape))
    return pl.pallas_call(
        kern,
        grid=(nbatch, rows // chunk_rows),
        in_specs=[
            pl.BlockSpec((1, chunk_rows, w), lambda n, c: (n, c, 0)),
            pl.BlockSpec((1, r, s), lambda n, c: (n, 0, 0)),
            pl.BlockSpec((1, r, s), lambda n, c: (n, 0, 0)),
            const((1, s)), const((1, s)), const(wb.shape), const(wc.shape), const((1, w)),
        ],
        out_specs=[
            pl.BlockSpec((1, chunk_rows, w), lambda n, c: (n, c, 0)),
            pl.BlockSpec((1, r, s), lambda n, c: (n, 0, 0)),
            pl.BlockSpec((1, r, s), lambda n, c: (n, 0, 0)),
        ],
        out_shape=[jax.ShapeDtypeStruct((nbatch, rows, w), F32),
                   jax.ShapeDtypeStruct((nbatch, r, s), F32),
                   jax.ShapeDtypeStruct((nbatch, r, s), F32)],
        scratch_shapes=[pltpu.VMEM((chunk_rows, s), F32), pltpu.VMEM((chunk_rows, s), F32),
                        pltpu.VMEM((r, s), F32), pltpu.VMEM((r, s), F32)],
        compiler_params=_cparams("parallel", "arbitrary"),
        name="s5_scan",
    )(u, h0_re, h0_im, ab_re, ab_im, wb, wc, d)


def _key_to_f32(k):
    bits = jnp.where(k >= 0, k, k ^ jnp.int32(0x7FFFFFFF))
    return lax.bitcast_convert_type(bits, F32)


def _kth_threshold(count_ge, shape, topk):
    kf = float(topk)
    c_zero = count_ge(jnp.zeros(shape, F32))
    c_low = count_ge(jnp.full(shape, _LOWEST_F32, F32))
    pos = c_zero >= kf
    lo = jnp.where(pos, 0, KEY_LOWEST).astype(jnp.int32)
    hi = jnp.where(pos, KEY_POS_INF + 1, 0).astype(jnp.int32)
    cnt_lo = jnp.where(pos, c_zero, c_low)

    def body(_, carry):
        lo, hi, cnt_lo = carry
        mid = lo + ((hi - lo) >> 1)
        cnt = count_ge(_key_to_f32(mid))
        ok = cnt >= kf
        return jnp.where(ok, mid, lo), jnp.where(ok, hi, mid), jnp.where(ok, cnt, cnt_lo)

    lo, hi, cnt_lo = lax.fori_loop(0, 31, body, (lo, hi, cnt_lo))
    return _key_to_f32(lo), cnt_lo


def _tie_index_bound(count_eq_le, count_gt, cnt_ge, shape, topk, n_keys):
    need = float(topk) - count_gt
    tied = cnt_ge > float(topk)
    lo = jnp.full(shape, -1, jnp.int32)
    hi = jnp.full(shape, n_keys - 1, jnp.int32)

    def body(_, carry):
        lo, hi = carry
        mid = lo + ((hi - lo) >> 1)
        ok = count_eq_le(mid) >= need
        return jnp.where(ok, lo, mid), jnp.where(ok, mid, hi)

    n_iter = max(1, math.ceil(math.log2(n_keys + 1)))
    lo, hi = lax.fori_loop(0, n_iter, body, (lo, hi))
    return jnp.where(tied, hi, n_keys)


def _prompt_attn_kernel(qt_ref, qit_ref, w_ref, ki_ref, k_ref, vt_ref, o_ref,
                        s_ref, m_ref, l_ref, acc_ref, *, topk, q_block, chunk, n_heads):
    qb = pl.program_id(1)
    n_keys = k_ref.shape[1] * chunk
    n_chunks = (qb * q_block + q_block + chunk - 1) // chunk
    k_row = lax.broadcasted_iota(jnp.int32, (chunk, q_block), 0)
    q_pos = qb * q_block + lax.broadcasted_iota(jnp.int32, (chunk, q_block), 1)

    qi_rows = qit_ref[0, 0].astype(BF16)
    qit = jnp.concatenate([qi_rows[h * IDX_DIM:(h + 1) * IDX_DIM] for h in range(IDX_HEADS)], axis=1)
    w_rows = w_ref[0, 0]
    w_row = jnp.concatenate([w_rows[h:h + 1] for h in range(IDX_HEADS)], axis=1)

    def score_body(c, carry):
        st = _dot(ki_ref[0, c], qit)
        st = jnp.maximum(st, 0.0) * w_row
        sc = st[:, 0:q_block]
        for h in range(1, IDX_HEADS):
            sc = sc + st[:, h * q_block:(h + 1) * q_block]
        s_ref[c] = jnp.where(c * chunk + k_row <= q_pos, sc, NEG_INF)
        return carry

    lax.fori_loop(0, n_chunks, score_body, 0)

    row = (1, q_block)
    fold = min(chunk, 64)

    n_steps = (n_chunks + COUNT_UNROLL - 1) // COUNT_UNROLL
    for extra in range(COUNT_UNROLL - 1):
        @pl.when(n_chunks + extra < n_steps * COUNT_UNROLL)
        def _():
            s_ref[n_chunks + extra] = jnp.full((chunk, q_block), NEG_INF, F32)

    def count(pred):
        def hits(c):
            h = jnp.where(pred(c, s_ref[c]), 1.0, 0.0)
            return jnp.sum(h.reshape(chunk // fold, fold, q_block), axis=0)

        def body(i, acc):
            for r in range(COUNT_UNROLL):
                acc = acc + hits(COUNT_UNROLL * i + r)
            return acc
        acc = lax.fori_loop(0, n_steps, body, jnp.zeros((fold, q_block), F32))
        return jnp.sum(acc, axis=0, keepdims=True)

    thr, cnt_ge = _kth_threshold(lambda t: count(lambda c, s: s >= t), row, topk)

    @pl.when(jnp.max(cnt_ge) > float(topk))
    def _():
        cnt_gt = count(lambda c, s: s > thr)
        bound = _tie_index_bound(
            lambda j: count(lambda c, s: (s == thr) & (c * chunk + k_row <= j)),
            cnt_gt, cnt_ge, row, topk, n_keys)

        def drop_body(c, carry):
            s = s_ref[c]
            s_ref[c] = jnp.where((s == thr) & (c * chunk + k_row > bound), NEG_INF, s)
            return carry

        lax.fori_loop(0, n_chunks, drop_body, 0)

    m_ref[...] = jnp.full(m_ref.shape, NEG_INF, F32)
    l_ref[...] = jnp.zeros(l_ref.shape, F32)
    acc_ref[...] = jnp.zeros(acc_ref.shape, F32)
    qt = qt_ref[0, 0] * (HEAD_DIM ** -0.5 * LOG2_E)
    head_of_dim = lax.broadcasted_iota(jnp.int32, (LANE, 2 * q_block), 0) // HEAD_DIM
    head_of_col = lax.broadcasted_iota(jnp.int32, (LANE, 2 * q_block), 1) // q_block
    q_pairs = []
    for j in range(n_heads // 2):
        qj = qt[j * LANE:(j + 1) * LANE, :]
        q_pairs.append(jnp.where(head_of_dim == head_of_col, jnp.concatenate([qj, qj], axis=1), 0.0).astype(BF16))

    def attn_body(c, carry):
        sel = s_ref[c] >= thr
        logits = [_dot(k_ref[0, c, :, j * LANE:(j + 1) * LANE], q_pairs[j]) for j in range(n_heads // 2)]
        for h in range(n_heads):
            j, hh = divmod(h, 2)
            x = jnp.where(sel, logits[j][:, hh * q_block:(hh + 1) * q_block], NEG_INF)
            m_old = m_ref[h]
            m_new = jnp.maximum(m_old, jnp.max(x, axis=0, keepdims=True))
            m_safe = jnp.where(m_new == NEG_INF, 0.0, m_new)
            p = jnp.exp2(x - m_safe)
            scale = jnp.exp2(m_old - m_safe)
            l_ref[h] = scale * l_ref[h] + jnp.sum(p, axis=0, keepdims=True)
            pv = _dot(vt_ref[0, c, h * HEAD_DIM:(h + 1) * HEAD_DIM, :], p.astype(BF16))
            acc_ref[h] = scale * acc_ref[h] + pv
            m_ref[h] = m_new
        return carry

    lax.fori_loop(0, n_chunks, attn_body, 0)

    out_t = jnp.concatenate([acc_ref[h] / l_ref[h] for h in range(n_heads)], axis=0)
    o_ref[0] = out_t.T


def _prompt_attention(q_t, qi_t, w_t, ki, k, v_t, topk, q_block, chunk):
    b, ntq, aw, tq = q_t.shape
    nch = k.shape[1]
    l = nch * chunk
    nqb = l // q_block
    per_tile = tq // q_block
    n_heads = aw // HEAD_DIM
    assert n_heads % 2 == 0 and ntq * tq == l and tq % q_block == 0 and chunk % q_block == 0
    assert 2 * HEAD_DIM == LANE
    kern = functools.partial(_prompt_attn_kernel, topk=topk, q_block=q_block, chunk=chunk, n_heads=n_heads)
    q_side = lambda rows: pl.BlockSpec((1, 1, rows, q_block), lambda i, j: (i, j // per_tile, 0, j % per_tile))
    return pl.pallas_call(
        kern,
        grid=(b, nqb),
        in_specs=[
            q_side(aw), q_side(IDX_HEADS * IDX_DIM), q_side(IDX_HEADS),
            pl.BlockSpec((1, nch, chunk, IDX_DIM), lambda i, j: (i, 0, 0, 0)),
            pl.BlockSpec((1, nch, chunk, aw), lambda i, j: (i, 0, 0, 0)),
            pl.BlockSpec((1, nch, aw, chunk), lambda i, j: (i, 0, 0, 0)),
        ],
        out_specs=pl.BlockSpec((1, q_block, aw), lambda i, j: (i, j, 0)),
        out_shape=jax.ShapeDtypeStruct((b, l, aw), F32),
        scratch_shapes=[pltpu.VMEM((nch + (-nch) % COUNT_UNROLL, chunk, q_block), F32),
                        pltpu.VMEM((n_heads, 1, q_block), F32),
                        pltpu.VMEM((n_heads, 1, q_block), F32),
                        pltpu.VMEM((n_heads, HEAD_DIM, q_block), F32)],
        compiler_params=_cparams("parallel", "arbitrary"),
        name="prompt_attention",
    )(q_t, qi_t, w_t, ki, k, v_t)


def _sample_select_kernel(pt_ref, qi_ref, w_ref, knew_ref, cache_ref, bias_ref, kbuf_ref, s_ref, sem,
                          *, topk, layer, group):
    step = pl.program_id(0)
    n_pages, page = kbuf_ref.shape[1] - 1, kbuf_ref.shape[3]
    n_q = knew_ref.shape[2]
    n_keys = (n_pages + 1) * page

    def page_copy(g, i):
        return pltpu.make_async_copy(cache_ref.at[layer, pt_ref[step * group + g, i]], kbuf_ref.at[g, i], sem.at[0])

    def start_body(g, carry):
        for i in range(n_pages):
            page_copy(g, i).start()
        return carry

    lax.fori_loop(0, group, start_body, 0)

    for g in range(group):
        kbuf_ref[g, n_pages] = jnp.zeros((IDX_DIM, page), F32)
        kbuf_ref[g, n_pages, :, 0:n_q] = knew_ref[g]

    def wait_body(g, carry):
        for i in range(n_pages):
            page_copy(g, i).wait()
        return carry

    lax.fori_loop(0, group, wait_body, 0)

    for g in range(group):
        qi = qi_ref[g].astype(BF16)
        w_col = w_ref[g]
        for i in range(n_pages + 1):
            st = _dot(qi, kbuf_ref[g, i].astype(BF16))
            st = jnp.maximum(st, 0.0) * w_col
            s_ref[g * n_q:(g + 1) * n_q, i * page:(i + 1) * page] = jnp.sum(st.reshape(IDX_HEADS, n_q, page), axis=0)

    rows = group * n_q
    q_of_row = lax.broadcasted_iota(jnp.int32, (rows, n_keys), 0) % n_q
    key_idx = lax.broadcasted_iota(jnp.int32, (rows, n_keys), 1)
    s_ref[...] = jnp.where(key_idx - n_pages * page <= q_of_row, s_ref[...], NEG_INF)

    col = (rows, 1)

    def count(pred):
        hits = jnp.where(pred(s_ref[...]), 1.0, 0.0)
        part = hits[:, 0:LANE]
        for t in range(1, n_keys // LANE):
            part = part + hits[:, t * LANE:(t + 1) * LANE]
        return jnp.sum(part, axis=1, keepdims=True)

    thr, cnt_ge = _kth_threshold(lambda t: count(lambda s: s >= t), col, topk)

    @pl.when(jnp.max(cnt_ge) > float(topk))
    def _():
        cnt_gt = count(lambda s: s > thr)
        bound = _tie_index_bound(lambda j: count(lambda s: (s == thr) & (key_idx <= j)),
                                 cnt_gt, cnt_ge, col, topk, n_keys)
        s = s_ref[...]
        s_ref[...] = jnp.where((s == thr) & (key_idx > bound), NEG_INF, s)

    bias_ref[...] = jnp.where(s_ref[...] >= thr, 0.0, NEG_INF).reshape(group, n_q, n_keys)


def _sample_select(page_table, qi_hm, w_col, cache_kidx_t, ki_new_t, layer, topk, group):
    db, n_pages = page_table.shape
    page = cache_kidx_t.shape[3]
    n_q = ki_new_t.shape[2]
    rows = IDX_HEADS * n_q
    n_keys = (n_pages + 1) * page
    assert db % group == 0 and n_q <= page
    kern = functools.partial(_sample_select_kernel, topk=topk, layer=layer, group=group)
    grid_spec = pltpu.PrefetchScalarGridSpec(
        num_scalar_prefetch=1,
        grid=(db // group,),
        in_specs=[
            pl.BlockSpec((group, rows, IDX_DIM), lambda s, pt: (s, 0, 0)),
            pl.BlockSpec((group, rows, 1), lambda s, pt: (s, 0, 0)),
            pl.BlockSpec((group, IDX_DIM, n_q), lambda s, pt: (s, 0, 0)),
            pl.BlockSpec(memory_space=pl.ANY),
        ],
        out_specs=pl.BlockSpec((group, n_q, n_keys), lambda s, pt: (s, 0, 0)),
        scratch_shapes=[pltpu.VMEM((group, n_pages + 1, IDX_DIM, page), F32),
                        pltpu.VMEM((group * n_q, n_keys), F32),
                        pltpu.SemaphoreType.DMA((1,))],
    )
    return pl.pallas_call(
        kern,
        grid_spec=grid_spec,
        out_shape=jax.ShapeDtypeStruct((db, n_q, n_keys), F32),
        compiler_params=_cparams("arbitrary"),
        name="sample_select",
    )(page_table, qi_hm, w_col, ki_new_t, cache_kidx_t)


def _sample_attn_kernel(pt_ref, q_ref, bias_ref, knew_ref, vnew_ref, kcache_ref, vcache_ref, o_ref,
                        kbuf_ref, vbuf_ref, lg_ref, kpad_ref, vpad_ref, sem, *, layer, n_heads):
    b = pl.program_id(0)
    n_seq = pl.num_programs(0)
    n_pages, page = kbuf_ref.shape[1], kbuf_ref.shape[4]
    n_q, aw = q_ref.shape[1], q_ref.shape[2]
    past = n_pages * page
    rows = n_heads * n_q
    slot = b % 2

    def page_copies(seq, i, to_slot):
        idx = pt_ref[seq, i]
        return (pltpu.make_async_copy(kcache_ref.at[layer, idx], kbuf_ref.at[to_slot, i], sem.at[to_slot, 0]),
                pltpu.make_async_copy(vcache_ref.at[layer, idx], vbuf_ref.at[to_slot, i], sem.at[to_slot, 1]))

    def start_fetch(seq, to_slot):
        for i in range(n_pages):
            for cp in page_copies(seq, i, to_slot):
                cp.start()

    @pl.when(b == 0)
    def _():
        start_fetch(0, 0)

    @pl.when(b + 1 < n_seq)
    def _():
        start_fetch(b + 1, 1 - slot)

    kpad_ref[...] = jnp.zeros(kpad_ref.shape, F32)
    vpad_ref[...] = jnp.zeros(vpad_ref.shape, F32)
    kpad_ref[0:n_q, :] = knew_ref[0]
    vpad_ref[0:n_q, :] = vnew_ref[0]

    q = q_ref[0] * (HEAD_DIM ** -0.5 * LOG2_E)
    head_of_row = lax.broadcasted_iota(jnp.int32, (rows, aw), 0) // n_q
    head_of_lane = lax.broadcasted_iota(jnp.int32, (rows, aw), 1) // HEAD_DIM
    own_head = head_of_row == head_of_lane
    q_diag = jnp.where(own_head, jnp.concatenate([q] * n_heads, axis=0), 0.0).astype(BF16)
    lg_ref[:, past:past + page] = _dot_nt(q_diag, kpad_ref[...].astype(BF16))

    for i in range(n_pages):
        for cp in page_copies(b, i, slot):
            cp.wait()

    for i in range(n_pages):
        k_t = kbuf_ref[slot, i].reshape(aw, page).astype(BF16)
        lg_ref[:, i * page:(i + 1) * page] = _dot(q_diag, k_t)

    logits = lg_ref[...] + jnp.concatenate([bias_ref[0]] * n_heads, axis=0)
    m = jnp.max(logits, axis=1, keepdims=True)
    pr = jnp.exp2(logits - m)
    denom = jnp.sum(pr, axis=1, keepdims=True)
    pr = pr.astype(BF16)

    acc = _dot(pr[:, past:past + page], vpad_ref[...].astype(BF16))
    for i in range(n_pages):
        v_t = vbuf_ref[slot, i].reshape(aw, page).astype(BF16)
        acc = acc + _dot_nt(pr[:, i * page:(i + 1) * page], v_t)
    out = jnp.where(own_head, acc / denom, 0.0)
    o_ref[0] = jnp.sum(out.reshape(n_heads, n_q, aw), axis=0)


def _sample_attention(page_table, q, bias, cache_k_t, cache_v_t, k_new, v_new, layer):
    db, n_pages = page_table.shape
    _, n_q, aw = q.shape
    n_heads = aw // HEAD_DIM
    page = cache_k_t.shape[4]
    n_keys = (n_pages + 1) * page
    rows = n_heads * n_q
    kern = functools.partial(_sample_attn_kernel, layer=layer, n_heads=n_heads)
    per_b = lambda shape: pl.BlockSpec(shape, lambda b, pt: (b,) + (0,) * (len(shape) - 1))
    grid_spec = pltpu.PrefetchScalarGridSpec(
        num_scalar_prefetch=1,
        grid=(db,),
        in_specs=[per_b((1, n_q, aw)), per_b((1, n_q, n_keys)), per_b((1, n_q, aw)), per_b((1, n_q, aw)),
                  pl.BlockSpec(memory_space=pl.ANY), pl.BlockSpec(memory_space=pl.ANY)],
        out_specs=per_b((1, n_q, aw)),
        scratch_shapes=[pltpu.VMEM((2, n_pages, n_heads, HEAD_DIM, page), F32),
                        pltpu.VMEM((2, n_pages, n_heads, HEAD_DIM, page), F32),
                        pltpu.VMEM((rows, n_keys), F32),
                        pltpu.VMEM((page, aw), F32), pltpu.VMEM((page, aw), F32),
                        pltpu.SemaphoreType.DMA((2, 2))],
    )
    return pl.pallas_call(
        kern,
        grid_spec=grid_spec,
        out_shape=jax.ShapeDtypeStruct((db, n_q, aw), F32),
        compiler_params=_cparams("arbitrary"),
        name="sample_attention",
    )(page_table, q, bias, k_new, v_new, cache_k_t, cache_v_t)


def _mix_kernel(x_ref, ys_ref, ya_ref, wglu_ref, bglu_ref, wos_ref, woa_ref, g_ref, b_ref, o_ref, *, alpha):
    y = ys_ref[...]
    gel = 0.5 * y * (1.0 + jnp.tanh(math.sqrt(2.0 / math.pi) * (y + 0.044715 * (y * y * y))))
    gate = jax.nn.sigmoid(_dot(gel.astype(BF16), wglu_ref[...]) + bglu_ref[...])
    y_ssm = gel * gate
    mix = _dot(y_ssm.astype(BF16), wos_ref[...]) + _dot(ya_ref[...].astype(BF16), woa_ref[...])
    o_ref[...] = _layer_norm(alpha * x_ref[...] + mix, g_ref[...], b_ref[...])


def _mix(x, y_ssm, y_att, w_glu, b_glu, w_out_ssm, w_out_att, g, b, alpha, tm):
    m, d = x.shape
    sw = y_ssm.shape[1]
    aw = y_att.shape[1]
    const = lambda shape: pl.BlockSpec(shape, lambda i: (0, 0))
    return pl.pallas_call(
        functools.partial(_mix_kernel, alpha=alpha),
        grid=(m // tm,),
        in_specs=[pl.BlockSpec((tm, d), lambda i: (i, 0)),
                  pl.BlockSpec((tm, sw), lambda i: (i, 0)),
                  pl.BlockSpec((tm, aw), lambda i: (i, 0)),
                  const((sw, sw)), const((1, sw)), const((sw, d)), const((aw, d)), const((1, d)), const((1, d))],
        out_specs=pl.BlockSpec((tm, d), lambda i: (i, 0)),
        out_shape=jax.ShapeDtypeStruct((m, d), F32),
        compiler_params=_cparams("parallel"),
        name="glu_outproj_ln",
    )(x, y_ssm, y_att, w_glu, b_glu, w_out_ssm, w_out_att, g, b)


def _row_tile(m, cap=512):
    t = cap
    while m % t:
        t //= 2
    return t


def _ff_tile(d_ff, cap=1536):
    best = LANE
    for t in range(LANE, cap + 1, LANE):
        if d_ff % t == 0:
            best = t
    return best


def _head_major(x, n_blocks, block, n_heads, dim):
    b = x.shape[0]
    x = x.reshape(b, n_blocks, block, n_heads, dim)
    return jnp.swapaxes(x, 2, 3).reshape(b, n_blocks, n_heads * block, dim)


def kernel(x_prompt, x_sample, cache_k, cache_v, cache_kidx, state_ssm_re, state_ssm_im, page_table,
           w_in, ssm_lambda_re, ssm_lambda_im, ssm_log_step, ssm_b_re, ssm_b_im, ssm_c_re, ssm_c_im,
           ssm_d, w_glu, b_glu, w_out, ffn1_up, ffn1_down, ffn2_up, ffn2_down, ln_g, ln_b):
    bsz, seq, d_model = x_prompt.shape
    db, dseq, _ = x_sample.shape
    depth = w_in.shape[0]
    sw = d_model // 2
    aw = d_model - sw
    n_groups = sw // SSM_GROUP
    n_state = n_groups * SSM_STATE
    n_heads = aw // HEAD_DIM
    page = cache_k.shape[2]
    n_pages = page_table.shape[1]
    past = n_pages * page
    alpha = (2.0 * depth) ** 0.25
    mp = bsz * seq
    ms = db * dseq
    tm_p = _row_tile(mp)
    tm_s = _row_tile(ms)
    q_block = min(128, seq)
    chunk = tm_p
    nch = seq // chunk
    topk_p = min(TOPK_MAX, seq // 4)
    topk_s = min(TOPK_MAX, (past + dseq) // 4)
    s5_rows = min(256, seq)
    s_groups = max(1, ms // 256)
    s_per = db // s_groups
    select_group = math.gcd(db, SUBLANE)
    cache_k_t = jnp.transpose(cache_k, (0, 1, 3, 4, 2))
    cache_v_t = jnp.transpose(cache_v, (0, 1, 3, 4, 2))
    cache_kidx_t = jnp.swapaxes(cache_kidx, 2, 3)
    pad_w = LANE - IDX_DIM - IDX_HEADS
    widths = (sw, aw, aw, aw, IDX_HEADS * IDX_DIM, LANE)

    xp = x_prompt.reshape(mp, d_model)
    xs = x_sample.reshape(ms, d_model)
    outs = {name: [] for name in ("kp", "vp", "kip", "hrp", "hip", "ks", "vs", "kis", "hrs", "his")}

    for l in range(depth):
        g_ln = ln_g[l][:, None, :]
        b_ln = ln_b[l][:, None, :]
        up1, down1 = ffn1_up[l].astype(BF16), ffn1_down[l].astype(BF16)
        tf1 = _ff_tile(down1.shape[0])
        xp = _ffn(xp, up1, down1, g_ln[0], b_ln[0], alpha, tm_p, tf1)
        xs = _ffn(xs, up1, down1, g_ln[0], b_ln[0], alpha, tm_s, tf1)

        w_l = jnp.pad(w_in[l], ((0, 0), (0, pad_w))).astype(BF16)
        o_q, o_k, o_v, o_qi = sw, sw + aw, sw + 2 * aw, sw + 3 * aw
        o_ki = o_qi + IDX_HEADS * IDX_DIM
        w_nat = jnp.concatenate([w_l[:, 0:o_q], w_l[:, o_k:o_v], w_l[:, o_ki:]], axis=1)
        w_t = w_l[:, o_q:o_ki + IDX_DIM + 16].T
        (u_p, k_pb, ki_pb, q_t, v_tb, qi_t, w_t_p, k_pt, v_pt, ki_pt) = _inproj_prompt(
            xp, w_nat, w_t, bsz, seq, sw, aw, tm_p)
        u_s, q_s, k_s, v_s, qi_s, kiwi_s = _inproj(xs, w_l, widths, tm_s)
        ki_s, wi_s = kiwi_s[:, :IDX_DIM], kiwi_s[:, IDX_DIM:IDX_DIM + IDX_HEADS]

        q_s = q_s.reshape(db, dseq, aw)
        k_s = k_s.reshape(db, dseq, aw)
        v_s = v_s.reshape(db, dseq, aw)
        ki_s = ki_s.reshape(db, dseq, IDX_DIM)
        qi_hm = _head_major(qi_s.reshape(db, dseq, IDX_HEADS * IDX_DIM), 1, dseq, IDX_HEADS, IDX_DIM)[:, 0]
        w_hm = _head_major(wi_s.reshape(db, dseq, IDX_HEADS), 1, dseq, IDX_HEADS, 1)[:, 0]
        bias = _sample_select(page_table, qi_hm, w_hm, cache_kidx_t, jnp.swapaxes(ki_s, 1, 2), l, topk_s, select_group)
        y_att_s = _sample_attention(page_table, q_s, bias, cache_k_t, cache_v_t, k_s, v_s, l)

        ab_re, ab_im, bb_re_t, bb_im_t = _s5_params(ssm_lambda_re[l], ssm_lambda_im[l], ssm_log_step[l],
                                                     ssm_b_re[l], ssm_b_im[l])
        wb = _re_im_stack(_block_diag_in(bb_re_t), _block_diag_in(bb_im_t))
        wc = _re_im_stack(_block_diag_out(ssm_c_re[l]), _block_diag_out(ssm_c_im[l]))
        ab_re = ab_re.reshape(1, n_state)
        ab_im = ab_im.reshape(1, n_state)
        d_row = ssm_d[l].reshape(1, sw)

        zeros_state = jnp.zeros((bsz, 1, n_state), F32)
        y_ssm_p, hr_p, hi_p = _s5_scan(u_p.reshape(bsz, seq, sw), zeros_state, zeros_state,
                                       ab_re, ab_im, wb, wc, d_row, s5_rows, 1)
        u_s = jnp.swapaxes(u_s.reshape(s_groups, s_per, dseq, sw), 1, 2)
        y_ssm_s, hr_s, hi_s = _s5_scan(u_s.reshape(s_groups, dseq * s_per, sw),
                                       state_ssm_re[l].reshape(s_groups, s_per, n_state),
                                       state_ssm_im[l].reshape(s_groups, s_per, n_state),
                                       ab_re, ab_im, wb, wc, d_row, dseq, s_per)
        y_ssm_s = jnp.swapaxes(y_ssm_s.reshape(s_groups, dseq, s_per, sw), 1, 2).reshape(ms, sw)

        y_att_p = _prompt_attention(q_t, qi_t, w_t_p, ki_pb.reshape(bsz, nch, chunk, IDX_DIM),
                                    k_pb.reshape(bsz, nch, chunk, aw), v_tb, topk_p, q_block, chunk)

        w_g = w_glu[l].astype(BF16)
        b_g = b_glu[l].reshape(1, sw)
        w_o = w_out[l].astype(BF16)
        xp = _mix(xp, y_ssm_p.reshape(mp, sw), y_att_p.reshape(mp, aw), w_g, b_g, w_o[:sw], w_o[sw:],
                  g_ln[1], b_ln[1], alpha, tm_p)
        xs = _mix(xs, y_ssm_s, y_att_s.reshape(ms, aw), w_g, b_g, w_o[:sw], w_o[sw:],
                  g_ln[1], b_ln[1], alpha, tm_s)
        up2, down2 = ffn2_up[l].astype(BF16), ffn2_down[l].astype(BF16)
        tf2 = _ff_tile(down2.shape[0])
        xp = _ffn(xp, up2, down2, g_ln[2], b_ln[2], alpha, tm_p, tf2)
        xs = _ffn(xs, up2, down2, g_ln[2], b_ln[2], alpha, tm_s, tf2)

        outs["kp"].append(jnp.transpose(k_pt.reshape(bsz, n_heads, HEAD_DIM, seq), (0, 3, 1, 2)))
        outs["vp"].append(jnp.transpose(v_pt.reshape(bsz, n_heads, HEAD_DIM, seq), (0, 3, 1, 2)))
        outs["kip"].append(jnp.swapaxes(ki_pt, 1, 2))
        outs["hrp"].append(hr_p.reshape(bsz, n_groups, SSM_STATE))
        outs["hip"].append(hi_p.reshape(bsz, n_groups, SSM_STATE))
        outs["ks"].append(k_s.reshape(db, dseq, n_heads, HEAD_DIM))
        outs["vs"].append(v_s.reshape(db, dseq, n_heads, HEAD_DIM))
        outs["kis"].append(ki_s)
        outs["hrs"].append(hr_s.reshape(db, n_groups, SSM_STATE))
        outs["his"].append(hi_s.reshape(db, n_groups, SSM_STATE))

    st = lambda name: jnp.stack(outs[name])
    return (xp.reshape(bsz, seq, d_model), xs.reshape(db, dseq, d_model),
            st("kp"), st("vp"), st("kip"), st("hrp"), st("hip"),
            st("ks"), st("vs"), st("kis"), st("hrs"), st("his"))
```
